```python
import math
import jax, jax.numpy as jnp
from jax import lax
import numpy as np

D_MODEL = 1024
BATCH = 4
SEQ = 4096
DEPTH = 1

MEM_LEN = 256
EPS = 1e-6

SSD_HEADS = 16
SSD_HEAD_DIM = 64
SSD_WIDTH = SSD_HEADS * SSD_HEAD_DIM
SSD_GROUPS = 4
SSD_STATE = 128
CONV_WIDTH = 4
SSD_CHUNK = 128
CONV_CH = SSD_WIDTH + 2 * SSD_GROUPS * SSD_STATE

DA_HEADS = 8
DA_HEAD_DIM = 64
DA_V_DIM = 2 * DA_HEAD_DIM
DA_WIDTH = DA_HEADS * DA_V_DIM
Q_BLOCK = 128

NUM_BUCKETS = 32
MAX_DISTANCE = 128

IN_SIZES = (SSD_WIDTH, CONV_CH, SSD_HEADS, DA_WIDTH, DA_WIDTH, DA_WIDTH)
IN_WIDTH = SSD_WIDTH + CONV_CH + SSD_HEADS + 3 * DA_WIDTH
MIX_WIDTH = SSD_WIDTH + DA_WIDTH

CROSS_HEADS = 4
CROSS_HEAD_DIM = D_MODEL // CROSS_HEADS

D_FF = 2816

kernel_name = 'hymba_ssd_diffattn_macaron_layer'


def rms_norm(x, w):
    x32 = x.astype(jnp.float32)
    y = x32 * lax.rsqrt(jnp.mean(x32 * x32, axis=-1, keepdims=True) + EPS)
    return y.astype(x.dtype) * w


def swiglu(h, w_in, w_out):
    gate, up = jnp.split(h @ w_in, 2, axis=-1)
    return (jax.nn.silu(gate) * up) @ w_out


def split_sizes(u, sizes):
    idx, acc = [], 0
    for s in sizes[:-1]:
        acc += s
        idx.append(acc)
    return jnp.split(u, idx, axis=-1)


def causal_dwconv(u, w, b):
    y = lax.conv_general_dilated(u, w[:, None, :], window_strides=(1,),
                                 padding=[(CONV_WIDTH - 1, 0)],
                                 dimension_numbers=('NWC', 'WIO', 'NWC'),
                                 feature_group_count=u.shape[-1])
    return y + b


def ssd_scan(xh, dt, a, b_in, c_in):
    bsz, seqlen, nh, hp = xh.shape
    ng, ns = b_in.shape[2], b_in.shape[3]
    r = nh // ng
    nc = seqlen // SSD_CHUNK
    xdt = (xh * dt[..., None]).reshape(bsz, nc, SSD_CHUNK, ng, r, hp)
    la = (dt * a).reshape(bsz, nc, SSD_CHUNK, ng, r)
    bc = b_in.reshape(bsz, nc, SSD_CHUNK, ng, ns)
    cc = c_in.reshape(bsz, nc, SSD_CHUNK, ng, ns)
    la_cum = jnp.cumsum(la, axis=2)
    causal = jnp.tril(jnp.ones((SSD_CHUNK, SSD_CHUNK), dtype=bool))
    seg = la_cum[:, :, :, None] - la_cum[:, :, None, :]
    decay = jnp.exp(jnp.where(causal[None, None, :, :, None, None], seg, -jnp.inf))
    cb = jnp.einsum('bclgn,bcsgn->bclsg', cc, bc)
    y_diag = jnp.einsum('bclsg,bclsgr,bcsgrp->bclgrp', cb, decay, xdt)
    decay_to_end = jnp.exp(la_cum[:, :, -1:] - la_cum)
    chunk_states = jnp.einsum('bclgn,bclgr,bclgrp->bcgrpn', bc, decay_to_end, xdt)
    chunk_decay = jnp.exp(la_cum[:, :, -1])

    def step(state, inp):
        st, dec = inp
        return state * dec[..., None, None] + st, state

    init = jnp.zeros_like(chunk_states[:, 0])
    _, prev = lax.scan(step, init, (jnp.moveaxis(chunk_states, 1, 0),
                                    jnp.moveaxis(chunk_decay, 1, 0)))
    prev = jnp.moveaxis(prev, 0, 1)
    y_off = jnp.einsum('bclgn,bcgrpn,bclgr->bclgrp', cc, prev, jnp.exp(la_cum))
    return (y_diag + y_off).reshape(bsz, seqlen, nh, hp)


def rel_bucket(rel):
    n = jnp.maximum(-rel, 0)
    max_exact = NUM_BUCKETS // 2
    nf = jnp.maximum(n, 1).astype(jnp.float32)
    large = max_exact + (jnp.log(nf / max_exact) / math.log(MAX_DISTANCE / max_exact)
                         * (NUM_BUCKETS - max_exact)).astype(jnp.int32)
    large = jnp.minimum(large, NUM_BUCKETS - 1)
    return jnp.where(n < max_exact, n, large)


def diff_attention(q, k, v, lam, rel_bias):
    bsz, seqlen, nh, _, dh = q.shape
    nb = seqlen // Q_BLOCK
    scale = dh ** -0.5
    qb = jnp.moveaxis(q.reshape(bsz, nb, Q_BLOCK, nh, 2, dh), 1, 0)
    k_pos = jnp.arange(seqlen)

    def block(args):
        q_blk, i = args
        q_pos = i * Q_BLOCK + jnp.arange(Q_BLOCK)
        rel = k_pos[None, :] - q_pos[:, None]
        bias = jnp.transpose(rel_bias[rel_bucket(rel)], (2, 0, 1)).astype(jnp.float32)
        logits = jnp.einsum('bqhcd,bkhcd->bhcqk', q_blk, k,
                            preferred_element_type=jnp.float32) * scale
        logits = logits + bias[None, :, None]
        logits = jnp.where(rel[None, None, None] <= 0, logits, -jnp.inf)
        p = jax.nn.softmax(logits, axis=-1)
        w = p[:, :, 0] - lam * p[:, :, 1]
        return jnp.einsum('bhqk,bkhe->bqhe', w.astype(v.dtype), v,
                          preferred_element_type=jnp.float32)

    out = lax.map(block, (qb, jnp.arange(nb)))
    return jnp.moveaxis(out, 0, 1).reshape(bsz, seqlen, nh, 2 * dh)


def hybrid_mixer(h, w_in, conv_w, conv_b, dt_bias, a_log, d_skip, ssd_norm_w,
                 lq1, lk1, lq2, lk2, subln_w, rel_bias, w_out, lambda_init):
    bsz, seqlen, _ = h.shape
    z, xbc, dt_raw, q, k, v = split_sizes(h @ w_in, IN_SIZES)

    xbc = jax.nn.silu(causal_dwconv(xbc, conv_w, conv_b))
    xs, bs, cs = jnp.split(xbc, [SSD_WIDTH, SSD_WIDTH + SSD_GROUPS * SSD_STATE], axis=-1)
    dt = jax.nn.softplus(dt_raw.astype(jnp.float32) + dt_bias.astype(jnp.float32))
    a = -jnp.exp(a_log.astype(jnp.float32))
    xh = xs.reshape(bsz, seqlen, SSD_HEADS, SSD_HEAD_DIM).astype(jnp.float32)
    y = ssd_scan(xh, dt, a,
                 bs.reshape(bsz, seqlen, SSD_GROUPS, SSD_STATE).astype(jnp.float32),
                 cs.reshape(bsz, seqlen, SSD_GROUPS, SSD_STATE).astype(jnp.float32))
    y = y + d_skip.astype(jnp.float32)[:, None] * xh
    y = y.reshape(bsz, seqlen, SSD_WIDTH) * jax.nn.silu(z.astype(jnp.float32))
    yg = y.reshape(bsz, seqlen, SSD_GROUPS, SSD_WIDTH // SSD_GROUPS)
    yg = yg * lax.rsqrt(jnp.mean(yg * yg, axis=-1, keepdims=True) + EPS)
    y_ssd = yg.reshape(bsz, seqlen, SSD_WIDTH).astype(h.dtype) * ssd_norm_w

    q = q.reshape(bsz, seqlen, DA_HEADS, 2, DA_HEAD_DIM)
    k = k.reshape(bsz, seqlen, DA_HEADS, 2, DA_HEAD_DIM)
    v = v.reshape(bsz, seqlen, DA_HEADS, DA_V_DIM)
    lam = (jnp.exp(jnp.sum(lq1.astype(jnp.float32) * lk1.astype(jnp.float32)))
           - jnp.exp(jnp.sum(lq2.astype(jnp.float32) * lk2.astype(jnp.float32)))
           + lambda_init)
    o = diff_attention(q, k, v, lam, rel_bias)
    o = o * lax.rsqrt(jnp.mean(o * o, axis=-1, keepdims=True) + EPS)
    y_attn = (o.astype(h.dtype) * subln_w * (1.0 - lambda_init)).reshape(bsz, seqlen, DA_WIDTH)

    return jnp.concatenate([y_ssd, y_attn], axis=-1) @ w_out


def memory_cross_attention(h, mem_n, w_cq, w_ck, w_cv, w_co):
    bsz, seqlen, _ = h.shape
    mlen = mem_n.shape[1]
    q = (h @ w_cq).reshape(bsz, seqlen, CROSS_HEADS, CROSS_HEAD_DIM)
    k = (mem_n @ w_ck).reshape(bsz, mlen, CROSS_HEADS, CROSS_HEAD_DIM)
    v = (mem_n @ w_cv).reshape(bsz, mlen, CROSS_HEADS, CROSS_HEAD_DIM)
    logits = jnp.einsum('bqhd,bkhd->bhqk', q, k,
                        preferred_element_type=jnp.float32) * (CROSS_HEAD_DIM ** -0.5)
    p = jax.nn.softmax(logits, axis=-1)
    o = jnp.einsum('bhqk,bkhd->bqhd', p.astype(v.dtype), v).reshape(bsz, seqlen, D_MODEL)
    return o @ w_co


def setup_inputs(seed: int = 0) -> dict:
    key = jax.random.key(seed)
    ks = jax.random.split(key, 32)
    f32 = jnp.float32

    def nrm(k, shape, scale):
        return jax.random.normal(k, shape, f32) * scale

    def gain(k, shape):
        return 1.0 + 0.02 * jax.random.normal(k, shape, f32)

    L = DEPTH
    dt0 = jnp.exp(jax.random.uniform(ks[10], (L, SSD_HEADS), f32,
                                     math.log(1e-3), math.log(1e-1)))
    dt_bias = dt0 + jnp.log(-jnp.expm1(-dt0))
    a_log = jnp.log(jax.random.uniform(ks[11], (L, SSD_HEADS), f32, 1.0, 16.0))
    return {
        'x': jax.random.normal(ks[0], (BATCH, SEQ, D_MODEL), f32),
        'mem': jax.random.normal(ks[1], (BATCH, MEM_LEN, D_MODEL), f32),
        'norm_ffn1_w': gain(ks[2], (L, D_MODEL)),
        'ffn1_w_in': nrm(ks[3], (L, D_MODEL, 2 * D_FF), D_MODEL ** -0.5),
        'ffn1_w_out': nrm(ks[4], (L, D_FF, D_MODEL), D_FF ** -0.5),
        'norm_mix_w': gain(ks[5], (L, D_MODEL)),
        'w_in_mix': nrm(ks[6], (L, D_MODEL, IN_WIDTH), D_MODEL ** -0.5),
        'conv_w': nrm(ks[7], (L, CONV_WIDTH, CONV_CH), CONV_WIDTH ** -0.5),
        'conv_b': nrm(ks[8], (L, CONV_CH), 0.01),
        'dt_bias': dt_bias,
        'a_log': a_log,
        'd_skip': gain(ks[9], (L, SSD_HEADS)),
        'ssd_norm_w': gain(ks[12], (L, SSD_WIDTH)),
        'lambda_q1': nrm(ks[13], (L, DA_HEAD_DIM), 0.1),
        'lambda_k1': nrm(ks[14], (L, DA_HEAD_DIM), 0.1),
        'lambda_q2': nrm(ks[15], (L, DA_HEAD_DIM), 0.1),
        'lambda_k2': nrm(ks[16], (L, DA_HEAD_DIM), 0.1),
        'subln_w': gain(ks[17], (L, DA_V_DIM)),
        'rel_bias': nrm(ks[18], (NUM_BUCKETS, DA_HEADS), 0.2),
        'w_out_mix': nrm(ks[19], (L, MIX_WIDTH, D_MODEL), MIX_WIDTH ** -0.5),
        'norm_cross_w': gain(ks[20], (L, D_MODEL)),
        'norm_mem_w': gain(ks[21], (L, D_MODEL)),
        'w_cq': nrm(ks[22], (L, D_MODEL, D_MODEL), D_MODEL ** -0.5),
        'w_ck': nrm(ks[23], (L, D_MODEL, D_MODEL), D_MODEL ** -0.5),
        'w_cv': nrm(ks[24], (L, D_MODEL, D_MODEL), D_MODEL ** -0.5),
        'w_co': nrm(ks[25], (L, D_MODEL, D_MODEL), D_MODEL ** -0.5),
        'norm_ffn2_w': gain(ks[26], (L, D_MODEL)),
        'ffn2_w_in': nrm(ks[27], (L, D_MODEL, 2 * D_FF), D_MODEL ** -0.5),
        'ffn2_w_out': nrm(ks[28], (L, D_FF, D_MODEL), D_FF ** -0.5),
        'norm_final_w': gain(ks[29], (D_MODEL,)),
    }


def reference(x, mem, norm_ffn1_w, ffn1_w_in, ffn1_w_out, norm_mix_w, w_in_mix,
              conv_w, conv_b, dt_bias, a_log, d_skip, ssd_norm_w,
              lambda_q1, lambda_k1, lambda_q2, lambda_k2, subln_w, rel_bias,
              w_out_mix, norm_cross_w, norm_mem_w, w_cq, w_ck, w_cv, w_co,
              norm_ffn2_w, ffn2_w_in, ffn2_w_out, norm_final_w):
    for l in range(DEPTH):
        lambda_init = 0.8 - 0.6 * math.exp(-0.3 * l)
        x = x + 0.5 * swiglu(rms_norm(x, norm_ffn1_w[l]), ffn1_w_in[l], ffn1_w_out[l])
        x = x + hybrid_mixer(rms_norm(x, norm_mix_w[l]), w_in_mix[l], conv_w[l], conv_b[l],
                             dt_bias[l], a_log[l], d_skip[l], ssd_norm_w[l],
                             lambda_q1[l], lambda_k1[l], lambda_q2[l], lambda_k2[l],
                             subln_w[l], rel_bias, w_out_mix[l], lambda_init)
        x = x + memory_cross_attention(rms_norm(x, norm_cross_w[l]),
                                       rms_norm(mem, norm_mem_w[l]),
                                       w_cq[l], w_ck[l], w_cv[l], w_co[l])
        x = x + 0.5 * swiglu(rms_norm(x, norm_ffn2_w[l]), ffn2_w_in[l], ffn2_w_out[l])
    return rms_norm(x, norm_final_w)
```

```python
import functools
import math

import jax
import jax.numpy as jnp
from jax import lax
from jax.experimental import pallas as pl
from jax.experimental.pallas import tpu as pltpu

F32 = jnp.float32
BF16 = jnp.bfloat16

D_MODEL = 1024
MEM_LEN = 256
EPS = 1e-6

SSD_HEADS = 16
SSD_HEAD_DIM = 64
SSD_WIDTH = SSD_HEADS * SSD_HEAD_DIM
SSD_GROUPS = 4
SSD_STATE = 128
CONV_WIDTH = 4
SSD_CHUNK = 128
CONV_CH = SSD_WIDTH + 2 * SSD_GROUPS * SSD_STATE
HEADS_PER_GROUP = SSD_HEADS // SSD_GROUPS

DA_HEADS = 8
DA_HEAD_DIM = 64
DA_V_DIM = 2 * DA_HEAD_DIM
DA_WIDTH = DA_HEADS * DA_V_DIM

NUM_BUCKETS = 32
MAX_DISTANCE = 128

CROSS_HEADS = 4
CROSS_HEAD_DIM = D_MODEL // CROSS_HEADS

D_FF = 2816

LANES = 128
SUBLANES = 8
MXU_DIM = 256
VMEM_LIMIT_BYTES = 56 * 1024 * 1024

ROW_TILE = 512
FF_CHUNK = MXU_DIM
ATTN_BLOCK = 256
DT_PAD = LANES
CONV_HALO = SUBLANES


def _resident(shape):
    nd = len(shape)
    return pl.BlockSpec(shape, lambda *_: (0,) * nd, pipeline_mode=pl.Buffered(1))


def _params(*sem):
    return pltpu.CompilerParams(dimension_semantics=sem,
                                vmem_limit_bytes=VMEM_LIMIT_BYTES)


def _rms(x, w):
    ms = jnp.mean(x * x, axis=-1, keepdims=True)
    return x * lax.rsqrt(ms + EPS) * w


def _silu(x):
    return x / (1.0 + jnp.exp(-x))


def _dot(a, b):
    return jnp.dot(a, b, preferred_element_type=F32)


def _dot_nt(a, b):
    return lax.dot_general(a, b, (((1,), (1,)), ((), ())),
                           preferred_element_type=F32)


def _ffn_kernel(x_ref, nw_ref, win_ref, wout_ref, fw_ref, o_ref, *, final):
    x = x_ref[...]
    h = _rms(x, nw_ref[...]).astype(BF16)
    acc = None
    for c in range(D_FF // FF_CHUNK):
        lo = c * FF_CHUNK
        g = _dot(h, win_ref[:, lo:lo + FF_CHUNK])
        u = _dot(h, win_ref[:, D_FF + lo:D_FF + lo + FF_CHUNK])
        a = (_silu(g) * u).astype(BF16)
        d = _dot(a, wout_ref[lo:lo + FF_CHUNK, :])
        acc = d if acc is None else acc + d
    y = x + 0.5 * acc
    if final:
        y = _rms(y, fw_ref[...])
    o_ref[...] = y


def _ffn(x, nw, win, wout, fw, *, final, name):
    t = x.shape[0]
    row = pl.BlockSpec((ROW_TILE, D_MODEL), lambda i: (i, 0))
    return pl.pallas_call(
        functools.partial(_ffn_kernel, final=final),
        out_shape=jax.ShapeDtypeStruct((t, D_MODEL), F32),
        grid=(t // ROW_TILE,),
        in_specs=[row, _resident((1, D_MODEL)), _resident((D_MODEL, 2 * D_FF)),
                  _resident((D_FF, D_MODEL)), _resident((1, D_MODEL))],
        out_specs=row,
        compiler_params=_params("parallel"),
        name=name,
    )(x, nw, win, wout, fw)


def _in_proj_kernel(x_ref, nw_ref, wz_ref, wxbc_ref, wdt_ref, wq_ref, wk_ref, wv_ref,
                    z_ref, xbc_ref, dt_ref, q_ref, k_ref, v_ref):
    h = _rms(x_ref[...], nw_ref[...]).astype(BF16)
    z_ref[...] = _dot(h, wz_ref[...]).astype(BF16)
    xbc_ref[...] = _dot(h, wxbc_ref[...]).astype(BF16)
    dt_ref[...] = _dot(h, wdt_ref[...])
    q_ref[...] = (_dot(h, wq_ref[...]) * (DA_HEAD_DIM ** -0.5)).astype(BF16)
    k_ref[...] = _dot(h, wk_ref[...]).astype(BF16)
    v_ref[...] = _dot(h, wv_ref[...]).astype(BF16)


def _in_proj(x1, nw, wz, wxbc, wdt, wq, wk, wv):
    t = x1.shape[0]

    def row(n):
        return pl.BlockSpec((ROW_TILE, n), lambda i: (i, 0))

    return pl.pallas_call(
        _in_proj_kernel,
        out_shape=(jax.ShapeDtypeStruct((t, SSD_WIDTH), BF16),
                   jax.ShapeDtypeStruct((t, CONV_CH), BF16),
                   jax.ShapeDtypeStruct((t, DT_PAD), F32),
                   jax.ShapeDtypeStruct((t, DA_WIDTH), BF16),
                   jax.ShapeDtypeStruct((t, DA_WIDTH), BF16),
                   jax.ShapeDtypeStruct((t, DA_WIDTH), BF16)),
        grid=(t // ROW_TILE,),
        in_specs=[row(D_MODEL), _resident((1, D_MODEL)),
                  _resident((D_MODEL, SSD_WIDTH)), _resident((D_MODEL, CONV_CH)),
                  _resident((D_MODEL, DT_PAD)), _resident((D_MODEL, DA_WIDTH)),
                  _resident((D_MODEL, DA_WIDTH)), _resident((D_MODEL, DA_WIDTH))],
        out_specs=(row(SSD_WIDTH), row(CONV_CH), row(DT_PAD),
                   row(DA_WIDTH), row(DA_WIDTH), row(DA_WIDTH)),
        compiler_params=_params("parallel"),
        name="in_proj",
    )(x1, nw, wz, wxbc, wdt, wq, wk, wv)


def _ssd_kernel(xbc_ref, z_ref, dt_ref, cw_ref, cb_ref, dtb_ref, alog_ref, dskip_ref,
                nw_ref, o_ref, ext_ref, state_ref):
    c = pl.program_id(1)
    L = SSD_CHUNK

    @pl.when(c == 0)
    def _():
        ext_ref[0:CONV_HALO, :] = jnp.zeros((CONV_HALO, CONV_CH), F32)
        state_ref[...] = jnp.zeros_like(state_ref)

    ext_ref[CONV_HALO:CONV_HALO + L, :] = xbc_ref[...].astype(F32)
    acc = cb_ref[...]
    for j in range(CONV_WIDTH):
        lo = CONV_HALO - (CONV_WIDTH - 1) + j
        acc = acc + cw_ref[j:j + 1, :] * ext_ref[lo:lo + L, :]
    ext_ref[0:CONV_HALO, :] = ext_ref[L:L + CONV_HALO, :]
    xbc = _silu(acc)
    xs = xbc[:, :SSD_WIDTH]
    xs_b = xs.astype(BF16)

    x_dt = dt_ref[...] + dtb_ref[...]
    dt = jnp.maximum(x_dt, 0.0) + jnp.log1p(jnp.exp(-jnp.abs(x_dt)))
    a = -jnp.exp(alog_ref[...])
    la = dt * a
    row_i = lax.broadcasted_iota(jnp.int32, (L, L), 0)
    col_i = lax.broadcasted_iota(jnp.int32, (L, L), 1)
    causal = col_i <= row_i
    tril = causal.astype(F32)
    cum_col = jnp.dot(tril, la, precision=lax.Precision.HIGHEST,
                      preferred_element_type=F32)
    cum_row = cum_col.T
    dt_row = dt.T

    ys = []
    for g in range(SSD_GROUPS):
        b_g = xbc[:, SSD_WIDTH + g * SSD_STATE:SSD_WIDTH + (g + 1) * SSD_STATE]
        c_g = xbc[:, SSD_WIDTH + (SSD_GROUPS + g) * SSD_STATE:
                  SSD_WIDTH + (SSD_GROUPS + g + 1) * SSD_STATE]
        cb = _dot_nt(c_g.astype(BF16), b_g.astype(BF16))
        bt_g = b_g.T
        for r in range(HEADS_PER_GROUP):
            h = g * HEADS_PER_GROUP + r
            col_b = jnp.broadcast_to(cum_col[:, h:h + 1], (L, L))
            row_b = jnp.broadcast_to(cum_row[h:h + 1, :], (L, L))
            decay = jnp.exp(jnp.where(causal, col_b - row_b, -jnp.inf))
            dtr = dt_row[h:h + 1, :]
            w_ls = (cb * decay * dtr).astype(BF16)
            c_exp = (c_g * jnp.exp(col_b)).astype(BF16)
            x_h = xs_b[:, h * SSD_HEAD_DIM:(h + 1) * SSD_HEAD_DIM]
            s_prev = state_ref[h]
            ys.append(_dot(w_ls, x_h) + _dot(c_exp, s_prev.astype(BF16)))
            to_end = decay[L - 1:L, :] * dtr
            new = _dot((bt_g * to_end).astype(BF16), x_h)
            chunk_decay = jnp.exp(col_b[L - 1:L, :SSD_HEAD_DIM])
            state_ref[h] = s_prev * chunk_decay + new
    y = jnp.concatenate(ys, axis=-1)

    y = y + dskip_ref[...] * xs
    y = y * _silu(z_ref[...].astype(F32))
    gw = SSD_WIDTH // SSD_GROUPS
    outs = []
    for g in range(SSD_GROUPS):
        yg = y[:, g * gw:(g + 1) * gw]
        ms = jnp.mean(yg * yg, axis=-1, keepdims=True)
        outs.append(yg * lax.rsqrt(ms + EPS))
    o_ref[...] = (jnp.concatenate(outs, axis=-1) * nw_ref[...]).astype(BF16)


def _ssd(xbc, z, dt, cw, cb, dtb, alog, dskip, nw, *, batch, seq):
    nc = seq // SSD_CHUNK

    def row(n):
        return pl.BlockSpec((SSD_CHUNK, n), lambda b, c: (b * nc + c, 0))

    return pl.pallas_call(
        _ssd_kernel,
        out_shape=jax.ShapeDtypeStruct((batch * seq, SSD_WIDTH), BF16),
        grid=(batch, nc),
        in_specs=[row(CONV_CH), row(SSD_WIDTH), row(DT_PAD),
                  _resident((CONV_WIDTH, CONV_CH)), _resident((1, CONV_CH)),
                  _resident((1, DT_PAD)), _resident((1, DT_PAD)),
                  _resident((1, SSD_WIDTH)), _resident((1, SSD_WIDTH))],
        out_specs=row(SSD_WIDTH),
        scratch_shapes=[pltpu.VMEM((CONV_HALO + SSD_CHUNK + CONV_HALO, CONV_CH), F32),
                        pltpu.VMEM((SSD_HEADS, SSD_STATE, SSD_HEAD_DIM), F32)],
        compiler_params=_params("parallel", "arbitrary"),
        name="ssd",
    )(xbc, z, dt, cw, cb, dtb, alog, dskip, nw)


def _rel_bias_kernel(rb_ref, o_ref):
    h = pl.program_id(0)
    d = pl.program_id(1)
    q_i = lax.broadcasted_iota(jnp.int32, (ATTN_BLOCK, ATTN_BLOCK), 0)
    k_i = lax.broadcasted_iota(jnp.int32, (ATTN_BLOCK, ATTN_BLOCK), 1)
    n = q_i - k_i + d * ATTN_BLOCK
    max_exact = NUM_BUCKETS // 2
    nf = jnp.maximum(n, 1).astype(F32)
    large = max_exact + (jnp.log(nf / max_exact) / math.log(MAX_DISTANCE / max_exact)
                         * (NUM_BUCKETS - max_exact)).astype(jnp.int32)
    large = jnp.minimum(large, NUM_BUCKETS - 1)
    bucket = jnp.where(n < max_exact, n, large)
    tile = jnp.zeros((ATTN_BLOCK, ATTN_BLOCK), F32)
    for b in range(NUM_BUCKETS):
        tile = jnp.where(bucket == b, rb_ref[b, h], tile)
    tile = tile - rb_ref[NUM_BUCKETS - 1, h]
    o_ref[...] = jnp.where(n >= 0, tile, -jnp.inf)


def _rel_bias_tiles(rel_bias):
    assert ATTN_BLOCK >= MAX_DISTANCE
    return pl.pallas_call(
        _rel_bias_kernel,
        out_shape=jax.ShapeDtypeStruct((DA_HEADS, 2, ATTN_BLOCK, ATTN_BLOCK), F32),
        grid=(DA_HEADS, 2),
        in_specs=[pl.BlockSpec(memory_space=pltpu.SMEM)],
        out_specs=pl.BlockSpec((None, None, ATTN_BLOCK, ATTN_BLOCK),
                               lambda h, d: (h, d, 0, 0)),
        compiler_params=_params("parallel", "parallel"),
        name="rel_bias",
    )(rel_bias)


def _attn_kernel(q_ref, k_ref, v_ref, bt_ref, lam_ref, sw_ref, o_ref,
                 m_ref, l_ref, acc_ref, *, lambda_init):
    i = pl.program_id(2)
    tb = ATTN_BLOCK
    dh = DA_HEAD_DIM
    q = q_ref[...]
    m_ref[...] = jnp.full(m_ref.shape, -jnp.inf, F32)
    l_ref[...] = jnp.zeros_like(l_ref)
    acc_ref[...] = jnp.zeros_like(acc_ref)

    def step(j, bias):
        start = pl.multiple_of(j * tb, tb)
        kb = k_ref[pl.ds(start, tb), :]
        vb = v_ref[pl.ds(start, tb), :]
        for c in range(2):
            s = _dot_nt(q[:, c * dh:(c + 1) * dh], kb[:, c * dh:(c + 1) * dh])
            if bias is not None:
                s = s + bias
            m_old = m_ref[c]
            m_new = jnp.maximum(m_old, jnp.max(s, axis=-1, keepdims=True))
            alpha = jnp.exp(m_old - m_new)
            p = jnp.exp(s - m_new)
            l_ref[c] = alpha * l_ref[c] + jnp.sum(p, axis=-1, keepdims=True)
            acc_ref[c] = alpha * acc_ref[c] + _dot(p.astype(BF16), vb)
            m_ref[c] = m_new

    def far(j, carry):
        step(j, None)
        return carry

    lax.fori_loop(0, jnp.maximum(i - 1, 0), far, 0)

    @pl.when(i >= 1)
    def _():
        step(i - 1, bt_ref[1])

    step(i, bt_ref[0])

    lp = lam_ref[...]
    lam = (jnp.exp(jnp.sum(lp[0:1] * lp[1:2], axis=-1, keepdims=True))
           - jnp.exp(jnp.sum(lp[2:3] * lp[3:4], axis=-1, keepdims=True))
           + lambda_init)
    o = acc_ref[0] / l_ref[0] - lam * (acc_ref[1] / l_ref[1])
    o = o * lax.rsqrt(jnp.mean(o * o, axis=-1, keepdims=True) + EPS)
    o_ref[...] = (o * sw_ref[...] * (1.0 - lambda_init)).astype(BF16)


def _attn(q, k, v, tiles, lam_p, sw, *, batch, seq, lambda_init):
    nq = seq // ATTN_BLOCK
    kv = pl.BlockSpec((seq, DA_V_DIM), lambda b, h, i: (b, h))
    qo = pl.BlockSpec((ATTN_BLOCK, DA_V_DIM), lambda b, h, i: (b * nq + i, h))
    return pl.pallas_call(
        functools.partial(_attn_kernel, lambda_init=lambda_init),
        out_shape=jax.ShapeDtypeStruct((batch * seq, DA_WIDTH), BF16),
        grid=(batch, DA_HEADS, nq),
        in_specs=[qo, kv, kv,
                  pl.BlockSpec((None, 2, ATTN_BLOCK, ATTN_BLOCK),
                               lambda b, h, i: (h, 0, 0, 0)),
                  _resident((4, DA_HEAD_DIM)), _resident((1, DA_V_DIM))],
        out_specs=qo,
        scratch_shapes=[pltpu.VMEM((2, ATTN_BLOCK, 1), F32),
                        pltpu.VMEM((2, ATTN_BLOCK, 1), F32),
                        pltpu.VMEM((2, ATTN_BLOCK, DA_V_DIM), F32)],
        compiler_params=_params("parallel", "parallel", "arbitrary"),
        name="diff_attn",
    )(q, k, v, tiles, lam_p, sw)


def _out_proj_kernel(x_ref, ys_ref, ya_ref, ws_ref, wa_ref, o_ref):
    o_ref[...] = (x_ref[...] + _dot(ys_ref[...], ws_ref[...])
                  + _dot(ya_ref[...], wa_ref[...]))


def _out_proj(x1, y_ssd, y_attn, w_s, w_a):
    t = x1.shape[0]
    row = pl.BlockSpec((ROW_TILE, D_MODEL), lambda i: (i, 0))
    return pl.pallas_call(
        _out_proj_kernel,
        out_shape=jax.ShapeDtypeStruct((t, D_MODEL), F32),
        grid=(t // ROW_TILE,),
        in_specs=[row, row, row, _resident((SSD_WIDTH, D_MODEL)),
                  _resident((DA_WIDTH, D_MODEL))],
        out_specs=row,
        compiler_params=_params("parallel"),
        name="out_proj",
    )(x1, y_ssd, y_attn, w_s, w_a)


def _mem_kv_kernel(mem_ref, nw_ref, wk_ref, wv_ref, k_ref, v_ref):
    mn = _rms(mem_ref[...], nw_ref[...]).astype(BF16)
    k_ref[...] = _dot(mn, wk_ref[...]).astype(BF16)
    v_ref[...] = _dot(mn, wv_ref[...]).astype(BF16)


def _mem_kv(mem, nw, wk, wv):
    t = mem.shape[0]
    row = pl.BlockSpec((MEM_LEN, D_MODEL), lambda i: (i, 0))
    return pl.pallas_call(
        _mem_kv_kernel,
        out_shape=(jax.ShapeDtypeStruct((t, D_MODEL), BF16),
                   jax.ShapeDtypeStruct((t, D_MODEL), BF16)),
        grid=(t // MEM_LEN,),
        in_specs=[row, _resident((1, D_MODEL)), _resident((D_MODEL, D_MODEL)),
                  _resident((D_MODEL, D_MODEL))],
        out_specs=(row, row),
        compiler_params=_params("parallel"),
        name="mem_kv",
    )(mem, nw, wk, wv)


def _cross_kernel(x_ref, nw_ref, wq_ref, k_ref, v_ref, wo_ref, o_ref):
    x = x_ref[...]
    h = _rms(x, nw_ref[...]).astype(BF16)
    q = (_dot(h, wq_ref[...]) * (CROSS_HEAD_DIM ** -0.5)).astype(BF16)
    outs = []
    for hh in range(CROSS_HEADS):
        sl = slice(hh * CROSS_HEAD_DIM, (hh + 1) * CROSS_HEAD_DIM)
        s = _dot_nt(q[:, sl], k_ref[:, sl])
        p = jnp.exp(s - jnp.max(s, axis=-1, keepdims=True))
        o = _dot(p.astype(BF16), v_ref[:, sl])
        outs.append(o / jnp.sum(p, axis=-1, keepdims=True))
    o = jnp.concatenate(outs, axis=-1).astype(BF16)
    o_ref[...] = x + _dot(o, wo_ref[...])


def _cross(x2, nw, wq, k, v, wo, *, seq):
    t = x2.shape[0]
    tiles_per_batch = seq // ROW_TILE
    row = pl.BlockSpec((ROW_TILE, D_MODEL), lambda i: (i, 0))
    mem = pl.BlockSpec((MEM_LEN, D_MODEL), lambda i: (i // tiles_per_batch, 0))
    return pl.pallas_call(
        _cross_kernel,
        out_shape=jax.ShapeDtypeStruct((t, D_MODEL), F32),
        grid=(t // ROW_TILE,),
        in_specs=[row, _resident((1, D_MODEL)), _resident((D_MODEL, D_MODEL)),
                  mem, mem, _resident((D_MODEL, D_MODEL))],
        out_specs=row,
        compiler_params=_params("parallel"),
        name="cross_attn",
    )(x2, nw, wq, k, v, wo)


def kernel(x, mem, norm_ffn1_w, ffn1_w_in, ffn1_w_out, norm_mix_w, w_in_mix, conv_w, conv_b, dt_bias, a_log, d_skip, ssd_norm_w, lambda_q1, lambda_k1, lambda_q2, lambda_k2, subln_w, rel_bias, w_out_mix, norm_cross_w, norm_mem_w, w_cq, w_ck, w_cv, w_co, norm_ffn2_w, ffn2_w_in, ffn2_w_out, norm_final_w):
    batch, seq, _ = x.shape
    depth = ffn1_w_in.shape[0]
    t = batch * seq
    xf = x.reshape(t, D_MODEL)
    memf = mem.reshape(batch * MEM_LEN, D_MODEL)
    fw = norm_final_w.reshape(1, D_MODEL)
    tiles = _rel_bias_tiles(rel_bias)

    def vec(p):
        return p.reshape(1, -1)

    def pad_lanes(p):
        return jnp.pad(p.reshape(1, -1), ((0, 0), (0, DT_PAD - p.shape[-1])))

    for l in range(depth):
        lambda_init = 0.8 - 0.6 * math.exp(-0.3 * l)
        xf = _ffn(xf, vec(norm_ffn1_w[l]), ffn1_w_in[l].astype(BF16),
                  ffn1_w_out[l].astype(BF16), fw, final=False, name="ffn1")

        wm = w_in_mix[l].astype(BF16)
        o_xbc = SSD_WIDTH
        o_dt = o_xbc + CONV_CH
        o_q = o_dt + SSD_HEADS
        o_k = o_q + DA_WIDTH
        o_v = o_k + DA_WIDTH
        wdt = jnp.pad(wm[:, o_dt:o_q], ((0, 0), (0, DT_PAD - SSD_HEADS)))
        z, xbc, dt, q, k, v = _in_proj(
            xf, vec(norm_mix_w[l]), wm[:, :o_xbc], wm[:, o_xbc:o_dt], wdt,
            wm[:, o_q:o_k], wm[:, o_k:o_v], wm[:, o_v:])

        y_ssd = _ssd(xbc, z, dt, conv_w[l], vec(conv_b[l]), pad_lanes(dt_bias[l]),
                     pad_lanes(a_log[l]), vec(jnp.repeat(d_skip[l], SSD_HEAD_DIM)),
                     vec(ssd_norm_w[l]), batch=batch, seq=seq)

        lam_p = jnp.stack([lambda_q1[l], lambda_k1[l], lambda_q2[l], lambda_k2[l]])
        y_attn = _attn(q, k, v, tiles, lam_p, vec(subln_w[l]),
                       batch=batch, seq=seq, lambda_init=lambda_init)

        wo = w_out_mix[l].astype(BF16)
        xf = _out_proj(xf, y_ssd, y_attn, wo[:SSD_WIDTH], wo[SSD_WIDTH:])

        mk, mv = _mem_kv(memf, vec(norm_mem_w[l]), w_ck[l].astype(BF16),
                         w_cv[l].astype(BF16))
        xf = _cross(xf, vec(norm_cross_w[l]), w_cq[l].astype(BF16), mk, mv,
                    w_co[l].astype(BF16), seq=seq)

        xf = _ffn(xf, vec(norm_ffn2_w[l]), ffn2_w_in[l].astype(BF16),
                  ffn2_w_out[l].astype(BF16), fw, final=(l == depth - 1),
                  name="ffn2")
    return xf.reshape(batch, seq, D_MODEL)
```

```python
import functools
import math

import jax
import jax.numpy as jnp
from jax import lax
from jax.experimental import pallas as pl
from jax.experimental.pallas import tpu as pltpu

F32 = jnp.float32
BF16 = jnp.bfloat16

D_MODEL = 1024
MEM_LEN = 256
EPS = 1e-6

SSD_HEADS = 16
SSD_HEAD_DIM = 64
SSD_WIDTH = SSD_HEADS * SSD_HEAD_DIM
SSD_GROUPS = 4
SSD_STATE = 128
CONV_WIDTH = 4
SSD_CHUNK = 128
CONV_CH = SSD_WIDTH + 2 * SSD_GROUPS * SSD_STATE
HEADS_PER_GROUP = SSD_HEADS // SSD_GROUPS

DA_HEADS = 8
DA_HEAD_DIM = 64
DA_V_DIM = 2 * DA_HEAD_DIM
DA_WIDTH = DA_HEADS * DA_V_DIM

NUM_BUCKETS = 32
MAX_DISTANCE = 128

CROSS_HEADS = 4
CROSS_HEAD_DIM = D_MODEL // CROSS_HEADS

D_FF = 2816

LANES = 128
SUBLANES = 8
MXU_DIM = 256
VMEM_LIMIT_BYTES = 56 * 1024 * 1024

ROW_TILE = 512
FF_CHUNK = MXU_DIM
ATTN_BLOCK = 512
ATTN_ONES_ROWS = 16
DT_PAD = LANES
CONV_HALO = SUBLANES


def _resident(shape):
    nd = len(shape)
    return pl.BlockSpec(shape, lambda *_: (0,) * nd, pipeline_mode=pl.Buffered(1))


def _params(*sem):
    return pltpu.CompilerParams(dimension_semantics=sem,
                                vmem_limit_bytes=VMEM_LIMIT_BYTES)


def _rms(x, w):
    ms = jnp.mean(x * x, axis=-1, keepdims=True)
    return x * lax.rsqrt(ms + EPS) * w


def _silu(x):
    return x / (1.0 + jnp.exp(-x))


def _dot(a, b):
    return jnp.dot(a, b, preferred_element_type=F32)


def _dot_nt(a, b):
    return lax.dot_general(a, b, (((1,), (1,)), ((), ())),
                           preferred_element_type=F32)


def _ffn_kernel(x_ref, nw_ref, win_ref, wout_ref, fw_ref, o_ref, *, final):
    x = x_ref[...]
    h = _rms(x, nw_ref[...]).astype(BF16)
    acc = None
    for c in range(D_FF // FF_CHUNK):
        lo = c * FF_CHUNK
        g = _dot(h, win_ref[:, lo:lo + FF_CHUNK])
        u = _dot(h, win_ref[:, D_FF + lo:D_FF + lo + FF_CHUNK])
        a = (_silu(g) * u).astype(BF16)
        d = _dot(a, wout_ref[lo:lo + FF_CHUNK, :])
        acc = d if acc is None else acc + d
    y = x + 0.5 * acc
    if final:
        y = _rms(y, fw_ref[...])
    o_ref[...] = y


def _ffn(x, nw, win, wout, fw, *, final, name):
    t = x.shape[0]
    row = pl.BlockSpec((ROW_TILE, D_MODEL), lambda i: (i, 0))
    return pl.pallas_call(
        functools.partial(_ffn_kernel, final=final),
        out_shape=jax.ShapeDtypeStruct((t, D_MODEL), F32),
        grid=(t // ROW_TILE,),
        in_specs=[row, _resident((1, D_MODEL)), _resident((D_MODEL, 2 * D_FF)),
                  _resident((D_FF, D_MODEL)), _resident((1, D_MODEL))],
        out_specs=row,
        compiler_params=_params("parallel"),
        name=name,
    )(x, nw, win, wout, fw)


def _in_proj_kernel(x_ref, nw_ref, wz_ref, wxbc_ref, wdt_ref, wq_ref, wk_ref, wv_ref,
                    z_ref, xbc_ref, dt_ref, q_ref, k_ref, v_ref):
    h = _rms(x_ref[...], nw_ref[...]).astype(BF16)
    z_ref[...] = _dot(h, wz_ref[...]).astype(BF16)
    xbc_ref[...] = _dot(h, wxbc_ref[...]).astype(BF16)
    dt_ref[...] = _dot(h, wdt_ref[...])
    q_ref[...] = (_dot(h, wq_ref[...]) * (DA_HEAD_DIM ** -0.5)).astype(BF16)
    k_ref[...] = _dot(h, wk_ref[...]).astype(BF16)
    v_ref[...] = _dot(h, wv_ref[...]).astype(BF16)


def _in_proj(x1, nw, wz, wxbc, wdt, wq, wk, wv):
    t = x1.shape[0]

    def row(n):
        return pl.BlockSpec((ROW_TILE, n), lambda i: (i, 0))

    return pl.pallas_call(
        _in_proj_kernel,
        out_shape=(jax.ShapeDtypeStruct((t, SSD_WIDTH), BF16),
                   jax.ShapeDtypeStruct((t, CONV_CH), BF16),
                   jax.ShapeDtypeStruct((t, DT_PAD), F32),
                   jax.ShapeDtypeStruct((t, DA_WIDTH), BF16),
                   jax.ShapeDtypeStruct((t, DA_WIDTH), BF16),
                   jax.ShapeDtypeStruct((t, DA_WIDTH), BF16)),
        grid=(t // ROW_TILE,),
        in_specs=[row(D_MODEL), _resident((1, D_MODEL)),
                  _resident((D_MODEL, SSD_WIDTH)), _resident((D_MODEL, CONV_CH)),
                  _resident((D_MODEL, DT_PAD)), _resident((D_MODEL, DA_WIDTH)),
                  _resident((D_MODEL, DA_WIDTH)), _resident((D_MODEL, DA_WIDTH))],
        out_specs=(row(SSD_WIDTH), row(CONV_CH), row(DT_PAD),
                   row(DA_WIDTH), row(DA_WIDTH), row(DA_WIDTH)),
        compiler_params=_params("parallel"),
        name="in_proj",
    )(x1, nw, wz, wxbc, wdt, wq, wk, wv)


def _ssd_kernel(xbc_ref, z_ref, dt_ref, cw_ref, cb_ref, dtb_ref, alog_ref, dskip_ref,
                nw_ref, o_ref, ext_ref, state_ref):
    c = pl.program_id(1)
    L = SSD_CHUNK

    @pl.when(c == 0)
    def _():
        ext_ref[0:CONV_HALO, :] = jnp.zeros((CONV_HALO, CONV_CH), F32)
        state_ref[...] = jnp.zeros_like(state_ref)

    ext_ref[CONV_HALO:CONV_HALO + L, :] = xbc_ref[...].astype(F32)
    acc = cb_ref[...]
    for j in range(CONV_WIDTH):
        lo = CONV_HALO - (CONV_WIDTH - 1) + j
        acc = acc + cw_ref[j:j + 1, :] * ext_ref[lo:lo + L, :]
    ext_ref[0:CONV_HALO, :] = ext_ref[L:L + CONV_HALO, :]
    xbc = _silu(acc)
    xs = xbc[:, :SSD_WIDTH]
    xs_b = xs.astype(BF16)

    x_dt = dt_ref[...] + dtb_ref[...]
    dt = jnp.maximum(x_dt, 0.0) + jnp.log1p(jnp.exp(-jnp.abs(x_dt)))
    a = -jnp.exp(alog_ref[...])
    la = dt * a
    row_i = lax.broadcasted_iota(jnp.int32, (L, L), 0)
    col_i = lax.broadcasted_iota(jnp.int32, (L, L), 1)
    causal = col_i <= row_i
    tril = causal.astype(F32)
    cum_col = jnp.dot(tril, la, precision=lax.Precision.HIGHEST,
                      preferred_element_type=F32)
    cum_row = cum_col.T
    dt_row = dt.T

    ys = []
    for g in range(SSD_GROUPS):
        b_g = xbc[:, SSD_WIDTH + g * SSD_STATE:SSD_WIDTH + (g + 1) * SSD_STATE]
        c_g = xbc[:, SSD_WIDTH + (SSD_GROUPS + g) * SSD_STATE:
                  SSD_WIDTH + (SSD_GROUPS + g + 1) * SSD_STATE]
        cb = _dot_nt(c_g.astype(BF16), b_g.astype(BF16))
        bt_g = b_g.T
        for r in range(HEADS_PER_GROUP):
            h = g * HEADS_PER_GROUP + r
            col_b = jnp.broadcast_to(cum_col[:, h:h + 1], (L, L))
            row_b = jnp.broadcast_to(cum_row[h:h + 1, :], (L, L))
            decay = jnp.exp(jnp.where(causal, col_b - row_b, -jnp.inf))
            dtr = dt_row[h:h + 1, :]
            w_ls = (cb * decay * dtr).astype(BF16)
            c_exp = (c_g * jnp.exp(col_b)).astype(BF16)
            x_h = xs_b[:, h * SSD_HEAD_DIM:(h + 1) * SSD_HEAD_DIM]
            s_prev = state_ref[h]
            ys.append(_dot(w_ls, x_h) + _dot(c_exp, s_prev.astype(BF16)))
            to_end = decay[L - 1:L, :] * dtr
            new = _dot((bt_g * to_end).astype(BF16), x_h)
            chunk_decay = jnp.exp(col_b[L - 1:L, :SSD_HEAD_DIM])
            state_ref[h] = s_prev * chunk_decay + new
    y = jnp.concatenate(ys, axis=-1)

    y = y + dskip_ref[...] * xs
    y = y * _silu(z_ref[...].astype(F32))
    gw = SSD_WIDTH // SSD_GROUPS
    outs = []
    for g in range(SSD_GROUPS):
        yg = y[:, g * gw:(g + 1) * gw]
        ms = jnp.mean(yg * yg, axis=-1, keepdims=True)
        outs.append(yg * lax.rsqrt(ms + EPS))
    o_ref[...] = (jnp.concatenate(outs, axis=-1) * nw_ref[...]).astype(BF16)


def _ssd(xbc, z, dt, cw, cb, dtb, alog, dskip, nw, *, batch, seq):
    nc = seq // SSD_CHUNK

    def row(n):
        return pl.BlockSpec((SSD_CHUNK, n), lambda b, c: (b * nc + c, 0))

    return pl.pallas_call(
        _ssd_kernel,
        out_shape=jax.ShapeDtypeStruct((batch * seq, SSD_WIDTH), BF16),
        grid=(batch, nc),
        in_specs=[row(CONV_CH), row(SSD_WIDTH), row(DT_PAD),
                  _resident((CONV_WIDTH, CONV_CH)), _resident((1, CONV_CH)),
                  _resident((1, DT_PAD)), _resident((1, DT_PAD)),
                  _resident((1, SSD_WIDTH)), _resident((1, SSD_WIDTH))],
        out_specs=row(SSD_WIDTH),
        scratch_shapes=[pltpu.VMEM((CONV_HALO + SSD_CHUNK + CONV_HALO, CONV_CH), F32),
                        pltpu.VMEM((SSD_HEADS, SSD_STATE, SSD_HEAD_DIM), F32)],
        compiler_params=_params("parallel", "arbitrary"),
        name="ssd",
    )(xbc, z, dt, cw, cb, dtb, alog, dskip, nw)


def _rel_bias_kernel(rb_ref, o_ref):
    h = pl.program_id(0)
    d = pl.program_id(1)
    k_i = lax.broadcasted_iota(jnp.int32, (ATTN_BLOCK, ATTN_BLOCK), 0)
    q_i = lax.broadcasted_iota(jnp.int32, (ATTN_BLOCK, ATTN_BLOCK), 1)
    n = q_i - k_i + d * ATTN_BLOCK
    max_exact = NUM_BUCKETS // 2
    nf = jnp.maximum(n, 1).astype(F32)
    large = max_exact + (jnp.log(nf / max_exact) / math.log(MAX_DISTANCE / max_exact)
                         * (NUM_BUCKETS - max_exact)).astype(jnp.int32)
    large = jnp.minimum(large, NUM_BUCKETS - 1)
    bucket = jnp.where(n < max_exact, n, large)
    tile = jnp.zeros((ATTN_BLOCK, ATTN_BLOCK), F32)
    for b in range(NUM_BUCKETS):
        tile = jnp.where(bucket == b, rb_ref[b, h], tile)
    tile = tile - rb_ref[NUM_BUCKETS - 1, h]
    o_ref[...] = jnp.where(n >= 0, tile, -jnp.inf)


def _rel_bias_tiles(rel_bias):
    assert ATTN_BLOCK >= MAX_DISTANCE
    return pl.pallas_call(
        _rel_bias_kernel,
        out_shape=jax.ShapeDtypeStruct((DA_HEADS, 2, ATTN_BLOCK, ATTN_BLOCK), F32),
        grid=(DA_HEADS, 2),
        in_specs=[pl.BlockSpec(memory_space=pltpu.SMEM)],
        out_specs=pl.BlockSpec((None, None, ATTN_BLOCK, ATTN_BLOCK),
                               lambda h, d: (h, d, 0, 0)),
        compiler_params=_params("parallel", "parallel"),
        name="rel_bias",
    )(rel_bias)


def _attn_kernel(q_ref, k_ref, v_ref, bt_ref, lam_ref, sw_ref, o_ref,
                 vt_ref, qz_ref, s0_ref, s1_ref, m_ref, acc_ref, *, lambda_init, seq):
    i = pl.program_id(2)
    tb = ATTN_BLOCK
    dh = DA_HEAD_DIM
    dv = DA_V_DIM

    @pl.when(i == 0)
    def _():
        for j in range(seq // tb):
            vt = v_ref[j * tb:(j + 1) * tb, :].astype(F32).T
            vt_ref[j, 0:dv, :] = vt.astype(BF16)
            vt_ref[j, dv:dv + ATTN_ONES_ROWS, :] = jnp.ones((ATTN_ONES_ROWS, tb), BF16)

    qt = q_ref[...].astype(F32).T
    feat = lax.broadcasted_iota(jnp.int32, qt.shape, 0)
    for c in range(2):
        in_map = (feat >= c * dh) & (feat < (c + 1) * dh)
        qz_ref[c] = jnp.where(in_map, qt, 0.0).astype(BF16)
    acc_ref[...] = jnp.zeros_like(acc_ref)
    m_ref[...] = jnp.full(m_ref.shape, -jnp.inf, F32)

    s_refs = (s0_ref, s1_ref)

    def scores(c, j):
        kb = k_ref[pl.ds(pl.multiple_of(j * tb, tb), tb), :]
        s_refs[c][...] = _dot(kb, qz_ref[c])

    def softmax_pv(c, j, bias):
        s = s_refs[c][...]
        if bias is not None:
            s = s + bias
        m_old = m_ref[c]
        m_new = jnp.maximum(m_old, jnp.max(s, axis=0, keepdims=True))
        p = jnp.exp(s - m_new).astype(BF16)
        acc_ref[c] = acc_ref[c] * jnp.exp(m_old - m_new) + _dot(vt_ref[j], p)
        m_ref[c] = m_new

    def block(j, bias, nxt):
        scores(1, j)
        softmax_pv(0, j, bias)
        if nxt is not None:
            scores(0, nxt)
        softmax_pv(1, j, bias)

    scores(0, 0)

    def far(j, carry):
        block(j, None, j + 1)
        return carry

    lax.fori_loop(0, jnp.maximum(i - 1, 0), far, 0)

    @pl.when(i >= 1)
    def _():
        block(i - 1, bt_ref[1], i)

    block(i, bt_ref[0], None)

    lp = lam_ref[...]
    lam = (jnp.exp(jnp.sum(lp[0:1] * lp[1:2], axis=-1, keepdims=True))
           - jnp.exp(jnp.sum(lp[2:3] * lp[3:4], axis=-1, keepdims=True))
           + lambda_init)
    a0 = acc_ref[0]
    a1 = acc_ref[1]
    o = a0[0:dv] / a0[dv:dv + 1] - lam * (a1[0:dv] / a1[dv:dv + 1])
    o = o * lax.rsqrt(jnp.mean(o * o, axis=0, keepdims=True) + EPS)
    o = o * sw_ref[...] * (1.0 - lambda_init)
    o_ref[...] = o.T.astype(BF16)


def _attn(q, k, v, tiles, lam_p, sw, *, batch, seq, lambda_init):
    nq = seq // ATTN_BLOCK
    kv = pl.BlockSpec((seq, DA_V_DIM), lambda b, h, i: (b, h))
    qo = pl.BlockSpec((ATTN_BLOCK, DA_V_DIM), lambda b, h, i: (b * nq + i, h))
    acc_rows = DA_V_DIM + ATTN_ONES_ROWS
    return pl.pallas_call(
        functools.partial(_attn_kernel, lambda_init=lambda_init, seq=seq),
        out_shape=jax.ShapeDtypeStruct((batch * seq, DA_WIDTH), BF16),
        grid=(batch, DA_HEADS, nq),
        in_specs=[qo, kv, kv,
                  pl.BlockSpec((None, 2, ATTN_BLOCK, ATTN_BLOCK),
                               lambda b, h, i: (h, 0, 0, 0)),
                  _resident((4, DA_HEAD_DIM)), _resident((DA_V_DIM, ATTN_BLOCK))],
        out_specs=qo,
        scratch_shapes=[pltpu.VMEM((nq, acc_rows, ATTN_BLOCK), BF16),
                        pltpu.VMEM((2, DA_V_DIM, ATTN_BLOCK), BF16),
                        pltpu.VMEM((ATTN_BLOCK, ATTN_BLOCK), F32),
                        pltpu.VMEM((ATTN_BLOCK, ATTN_BLOCK), F32),
                        pltpu.VMEM((2, 1, ATTN_BLOCK), F32),
                        pltpu.VMEM((2, acc_rows, ATTN_BLOCK), F32)],
        compiler_params=_params("parallel", "parallel", "arbitrary"),
        name="diff_attn",
    )(q, k, v, tiles, lam_p, sw)


def _out_proj_kernel(x_ref, ys_ref, ya_ref, ws_ref, wa_ref, o_ref):
    o_ref[...] = (x_ref[...] + _dot(ys_ref[...], ws_ref[...])
                  + _dot(ya_ref[...], wa_ref[...]))


def _out_proj(x1, y_ssd, y_attn, w_s, w_a):
    t = x1.shape[0]
    row = pl.BlockSpec((ROW_TILE, D_MODEL), lambda i: (i, 0))
    return pl.pallas_call(
        _out_proj_kernel,
        out_shape=jax.ShapeDtypeStruct((t, D_MODEL), F32),
        grid=(t // ROW_TILE,),
        in_specs=[row, row, row, _resident((SSD_WIDTH, D_MODEL)),
                  _resident((DA_WIDTH, D_MODEL))],
        out_specs=row,
        compiler_params=_params("parallel"),
        name="out_proj",
    )(x1, y_ssd, y_attn, w_s, w_a)


def _mem_kv_kernel(mem_ref, nw_ref, wk_ref, wv_ref, k_ref, v_ref):
    mn = _rms(mem_ref[...], nw_ref[...]).astype(BF16)
    k_ref[...] = _dot(mn, wk_ref[...]).astype(BF16)
    v_ref[...] = _dot(mn, wv_ref[...]).astype(BF16)


def _mem_kv(mem, nw, wk, wv):
    t = mem.shape[0]
    row = pl.BlockSpec((MEM_LEN, D_MODEL), lambda i: (i, 0))
    return pl.pallas_call(
        _mem_kv_kernel,
        out_shape=(jax.ShapeDtypeStruct((t, D_MODEL), BF16),
                   jax.ShapeDtypeStruct((t, D_MODEL), BF16)),
        grid=(t // MEM_LEN,),
        in_specs=[row, _resident((1, D_MODEL)), _resident((D_MODEL, D_MODEL)),
                  _resident((D_MODEL, D_MODEL))],
        out_specs=(row, row),
        compiler_params=_params("parallel"),
        name="mem_kv",
    )(mem, nw, wk, wv)


def _cross_kernel(x_ref, nw_ref, wq_ref, k_ref, v_ref, wo_ref, o_ref):
    x = x_ref[...]
    h = _rms(x, nw_ref[...]).astype(BF16)
    q = (_dot(h, wq_ref[...]) * (CROSS_HEAD_DIM ** -0.5)).astype(BF16)
    outs = []
    for hh in range(CROSS_HEADS):
        sl = slice(hh * CROSS_HEAD_DIM, (hh + 1) * CROSS_HEAD_DIM)
        s = _dot_nt(q[:, sl], k_ref[:, sl])
        p = jnp.exp(s - jnp.max(s, axis=-1, keepdims=True))
        o = _dot(p.astype(BF16), v_ref[:, sl])
        outs.append(o / jnp.sum(p, axis=-1, keepdims=True))
    o = jnp.concatenate(outs, axis=-1).astype(BF16)
    o_ref[...] = x + _dot(o, wo_ref[...])


def _cross(x2, nw, wq, k, v, wo, *, seq):
    t = x2.shape[0]
    tiles_per_batch = seq // ROW_TILE
    row = pl.BlockSpec((ROW_TILE, D_MODEL), lambda i: (i, 0))
    mem = pl.BlockSpec((MEM_LEN, D_MODEL), lambda i: (i // tiles_per_batch, 0))
    return pl.pallas_call(
        _cross_kernel,
        out_shape=jax.ShapeDtypeStruct((t, D_MODEL), F32),
        grid=(t // ROW_TILE,),
        in_specs=[row, _resident((1, D_MODEL)), _resident((D_MODEL, D_MODEL)),
                  mem, mem, _resident((D_MODEL, D_MODEL))],
        out_specs=row,
        compiler_params=_params("parallel"),
        name="cross_attn",
    )(x2, nw, wq, k, v, wo)


def kernel(x, mem, norm_ffn1_w, ffn1_w_in, ffn1_w_out, norm_mix_w, w_in_mix, conv_w, conv_b, dt_bias, a_log, d_skip, ssd_norm_w, lambda_q1, lambda_k1, lambda_q2, lambda_k2, subln_w, rel_bias, w_out_mix, norm_cross_w, norm_mem_w, w_cq, w_ck, w_cv, w_co, norm_ffn2_w, ffn2_w_in, ffn2_w_out, norm_final_w):
    batch, seq, _ = x.shape
    depth = ffn1_w_in.shape[0]
    t = batch * seq
    xf = x.reshape(t, D_MODEL)
    memf = mem.reshape(batch * MEM_LEN, D_MODEL)
    fw = norm_final_w.reshape(1, D_MODEL)
    tiles = _rel_bias_tiles(rel_bias)

    def vec(p):
        return p.reshape(1, -1)

    def pad_lanes(p):
        return jnp.pad(p.reshape(1, -1), ((0, 0), (0, DT_PAD - p.shape[-1])))

    for l in range(depth):
        lambda_init = 0.8 - 0.6 * math.exp(-0.3 * l)
        xf = _ffn(xf, vec(norm_ffn1_w[l]), ffn1_w_in[l].astype(BF16),
                  ffn1_w_out[l].astype(BF16), fw, final=False, name="ffn1")

        wm = w_in_mix[l].astype(BF16)
        o_xbc = SSD_WIDTH
        o_dt = o_xbc + CONV_CH
        o_q = o_dt + SSD_HEADS
        o_k = o_q + DA_WIDTH
        o_v = o_k + DA_WIDTH
        wdt = jnp.pad(wm[:, o_dt:o_q], ((0, 0), (0, DT_PAD - SSD_HEADS)))
        z, xbc, dt, q, k, v = _in_proj(
            xf, vec(norm_mix_w[l]), wm[:, :o_xbc], wm[:, o_xbc:o_dt], wdt,
            wm[:, o_q:o_k], wm[:, o_k:o_v], wm[:, o_v:])

        y_ssd = _ssd(xbc, z, dt, conv_w[l], vec(conv_b[l]), pad_lanes(dt_bias[l]),
                     pad_lanes(a_log[l]), vec(jnp.repeat(d_skip[l], SSD_HEAD_DIM)),
                     vec(ssd_norm_w[l]), batch=batch, seq=seq)

        lam_p = jnp.stack([lambda_q1[l], lambda_k1[l], lambda_q2[l], lambda_k2[l]])
        sw = jnp.broadcast_to(subln_w[l].reshape(DA_V_DIM, 1), (DA_V_DIM, ATTN_BLOCK))
        y_attn = _attn(q, k, v, tiles, lam_p, sw,
                       batch=batch, seq=seq, lambda_init=lambda_init)

        wo = w_out_mix[l].astype(BF16)
        xf = _out_proj(xf, y_ssd, y_attn, wo[:SSD_WIDTH], wo[SSD_WIDTH:])

        mk, mv = _mem_kv(memf, vec(norm_mem_w[l]), w_ck[l].astype(BF16),
                         w_cv[l].astype(BF16))
        xf = _cross(xf, vec(norm_cross_w[l]), w_cq[l].astype(BF16), mk, mv,
                    w_co[l].astype(BF16), seq=seq)

        xf = _ffn(xf, vec(norm_ffn2_w[l]), ffn2_w_in[l].astype(BF16),
                  ffn2_w_out[l].astype(BF16), fw, final=(l == depth - 1),
                  name="ffn2")
    return xf.reshape(batch, seq, D_MODEL)
```

```python
import functools
import math

import jax
import jax.numpy as jnp
from jax import lax
from jax.experimental import pallas as pl
from jax.experimental.pallas import tpu as pltpu

F32 = jnp.float32
BF16 = jnp.bfloat16

D_MODEL = 1024
MEM_LEN = 256
EPS = 1e-6

SSD_HEADS = 16
SSD_HEAD_DIM = 64
SSD_WIDTH = SSD_HEADS * SSD_HEAD_DIM
SSD_GROUPS = 4
SSD_STATE = 128
CONV_WIDTH = 4
SSD_CHUNK = 128
CONV_CH = SSD_WIDTH + 2 * SSD_GROUPS * SSD_STATE
HEADS_PER_GROUP = SSD_HEADS // SSD_GROUPS

DA_HEADS = 8
DA_HEAD_DIM = 64
DA_V_DIM = 2 * DA_HEAD_DIM
DA_WIDTH = DA_HEADS * DA_V_DIM

NUM_BUCKETS = 32
MAX_DISTANCE = 128

CROSS_HEADS = 4
CROSS_HEAD_DIM = D_MODEL // CROSS_HEADS

D_FF = 2816

LANES = 128
SUBLANES = 8
MXU_DIM = 256
VMEM_LIMIT_BYTES = 56 * 1024 * 1024

ROW_TILE = 512
FF_CHUNK = MXU_DIM
ATTN_BLOCK = 512
ATTN_ONES_ROWS = 16
BIAS_TILE = MAX_DISTANCE
DT_PAD = LANES
CONV_HALO = SUBLANES
SSD_CHUNKS_PER_STEP = 2


def _resident(shape):
    nd = len(shape)
    return pl.BlockSpec(shape, lambda *_: (0,) * nd, pipeline_mode=pl.Buffered(1))


def _params(*sem):
    return pltpu.CompilerParams(dimension_semantics=sem,
                                vmem_limit_bytes=VMEM_LIMIT_BYTES)


def _rms(x, w):
    ms = jnp.mean(x * x, axis=-1, keepdims=True)
    return x * lax.rsqrt(ms + EPS) * w


def _silu(x):
    return x / (1.0 + jnp.exp(-x))


def _dot(a, b):
    return jnp.dot(a, b, preferred_element_type=F32)


def _dot_nt(a, b):
    return lax.dot_general(a, b, (((1,), (1,)), ((), ())),
                           preferred_element_type=F32)


def _ffn_kernel(x_ref, nw_ref, win_ref, wout_ref, fw_ref, o_ref, *, final):
    x = x_ref[...]
    h = _rms(x, nw_ref[...]).astype(BF16)
    acc = None
    for c in range(D_FF // FF_CHUNK):
        lo = c * FF_CHUNK
        g = _dot(h, win_ref[:, lo:lo + FF_CHUNK])
        u = _dot(h, win_ref[:, D_FF + lo:D_FF + lo + FF_CHUNK])
        a = (_silu(g) * u).astype(BF16)
        d = _dot(a, wout_ref[lo:lo + FF_CHUNK, :])
        acc = d if acc is None else acc + d
    y = x + 0.5 * acc
    if final:
        y = _rms(y, fw_ref[...])
    o_ref[...] = y


def _ffn(x, nw, win, wout, fw, *, final, name):
    t = x.shape[0]
    row = pl.BlockSpec((ROW_TILE, D_MODEL), lambda i: (i, 0))
    return pl.pallas_call(
        functools.partial(_ffn_kernel, final=final),
        out_shape=jax.ShapeDtypeStruct((t, D_MODEL), F32),
        grid=(t // ROW_TILE,),
        in_specs=[row, _resident((1, D_MODEL)), _resident((D_MODEL, 2 * D_FF)),
                  _resident((D_FF, D_MODEL)), _resident((1, D_MODEL))],
        out_specs=row,
        compiler_params=_params("parallel"),
        name=name,
    )(x, nw, win, wout, fw)


def _in_proj_kernel(x_ref, nw_ref, wz_ref, wxbc_ref, wdt_ref, wq_ref, wk_ref, wv_ref,
                    cw_ref, cb_ref, dtb_ref, alog_ref,
                    z_ref, xbc_ref, dt_ref, cum_ref, q_ref, k_ref, v_ref, ext_ref,
                    *, tiles_per_seq):
    i = pl.program_id(0)
    tm = ROW_TILE
    h = _rms(x_ref[...], nw_ref[...]).astype(BF16)

    @pl.when(i % tiles_per_seq == 0)
    def _():
        ext_ref[:, 0:CONV_HALO, :] = jnp.zeros((CONV_CH // LANES, CONV_HALO, LANES), F32)

    x_dt = _dot(h, wdt_ref[...]) + dtb_ref[...]
    dt = jnp.maximum(x_dt, 0.0) + jnp.log1p(jnp.exp(-jnp.abs(x_dt)))
    dt_ref[...] = dt
    la = dt * -jnp.exp(alog_ref[...])
    la_hi = la.astype(BF16)
    la_mid = (la - la_hi.astype(F32)).astype(BF16)
    la_lo = (la - la_hi.astype(F32) - la_mid.astype(F32)).astype(BF16)
    row_i = lax.broadcasted_iota(jnp.int32, (SSD_CHUNK, SSD_CHUNK), 0)
    col_i = lax.broadcasted_iota(jnp.int32, (SSD_CHUNK, SSD_CHUNK), 1)
    tril = (col_i <= row_i).astype(BF16)

    wide = ((z_ref, wz_ref, 1.0), (q_ref, wq_ref, DA_HEAD_DIM ** -0.5),
            (k_ref, wk_ref, 1.0), (v_ref, wv_ref, 1.0))
    n_chunks = tm // SSD_CHUNK
    assert n_chunks == len(wide)
    for r in range(n_chunks):
        lo = r * SSD_CHUNK
        rows = slice(lo, lo + SSD_CHUNK)
        cum_ref[rows, :] = (_dot(tril, la_hi[rows]) + _dot(tril, la_mid[rows])
                            + _dot(tril, la_lo[rows]))
        u = _dot(h[rows], wxbc_ref[...])
        base = CONV_HALO + lo
        acts = []
        for t in range(CONV_CH // LANES):
            ln = slice(t * LANES, (t + 1) * LANES)
            ext_ref[t, base:base + SSD_CHUNK, :] = u[:, ln]
            acc = cb_ref[:, ln]
            for j in range(CONV_WIDTH):
                start = base - (CONV_WIDTH - 1) + j
                acc = acc + cw_ref[j:j + 1, ln] * ext_ref[t, start:start + SSD_CHUNK, :]
            acts.append(_silu(acc))
        xbc_ref[rows, :] = jnp.concatenate(acts, axis=-1).astype(BF16)
        o_ref, w_ref, scale = wide[r]
        out = _dot(h, w_ref[...])
        o_ref[...] = (out if scale == 1.0 else out * scale).astype(BF16)
    ext_ref[:, 0:CONV_HALO, :] = ext_ref[:, tm:tm + CONV_HALO, :]


def _in_proj(x1, nw, wz, wxbc, wdt, wq, wk, wv, cw, cb, dtb, alog, *, seq):
    t = x1.shape[0]

    def row(n):
        return pl.BlockSpec((ROW_TILE, n), lambda i: (i, 0))

    return pl.pallas_call(
        functools.partial(_in_proj_kernel, tiles_per_seq=seq // ROW_TILE),
        out_shape=(jax.ShapeDtypeStruct((t, SSD_WIDTH), BF16),
                   jax.ShapeDtypeStruct((t, CONV_CH), BF16),
                   jax.ShapeDtypeStruct((t, DT_PAD), F32),
                   jax.ShapeDtypeStruct((t, DT_PAD), F32),
                   jax.ShapeDtypeStruct((t, DA_WIDTH), BF16),
                   jax.ShapeDtypeStruct((t, DA_WIDTH), BF16),
                   jax.ShapeDtypeStruct((t, DA_WIDTH), BF16)),
        grid=(t // ROW_TILE,),
        in_specs=[row(D_MODEL), _resident((1, D_MODEL)),
                  _resident((D_MODEL, SSD_WIDTH)), _resident((D_MODEL, CONV_CH)),
                  _resident((D_MODEL, DT_PAD)), _resident((D_MODEL, DA_WIDTH)),
                  _resident((D_MODEL, DA_WIDTH)), _resident((D_MODEL, DA_WIDTH)),
                  _resident((CONV_WIDTH, CONV_CH)), _resident((1, CONV_CH)),
                  _resident((1, DT_PAD)), _resident((1, DT_PAD))],
        out_specs=(row(SSD_WIDTH), row(CONV_CH), row(DT_PAD), row(DT_PAD),
                   row(DA_WIDTH), row(DA_WIDTH), row(DA_WIDTH)),
        scratch_shapes=[pltpu.VMEM((CONV_CH // LANES, CONV_HALO + ROW_TILE + CONV_HALO,
                                    LANES), F32)],
        compiler_params=_params("arbitrary"),
        name="in_proj",
    )(x1, nw, wz, wxbc, wdt, wq, wk, wv, cw, cb, dtb, alog)


def _ssd_chunk(xbc, z, dt, cum_col, dskip, nw, state_ref):
    L = SSD_CHUNK
    xs_b = xbc[:, :SSD_WIDTH]
    row_i = lax.broadcasted_iota(jnp.int32, (L, L), 0)
    col_i = lax.broadcasted_iota(jnp.int32, (L, L), 1)
    causal = col_i <= row_i
    cum_row = cum_col.T
    dt_row = dt.T

    ys = []
    for g in range(SSD_GROUPS):
        b_g = xbc[:, SSD_WIDTH + g * SSD_STATE:SSD_WIDTH + (g + 1) * SSD_STATE]
        c_g = xbc[:, SSD_WIDTH + (SSD_GROUPS + g) * SSD_STATE:
                  SSD_WIDTH + (SSD_GROUPS + g + 1) * SSD_STATE]
        cb = _dot_nt(c_g, b_g)
        c_f = c_g.astype(F32)
        bt_g = b_g.astype(F32).T
        for r in range(HEADS_PER_GROUP):
            h = g * HEADS_PER_GROUP + r
            col_b = jnp.broadcast_to(cum_col[:, h:h + 1], (L, L))
            row_b = jnp.broadcast_to(cum_row[h:h + 1, :], (L, L))
            decay = jnp.exp(jnp.where(causal, col_b - row_b, -jnp.inf))
            dtr = dt_row[h:h + 1, :]
            w_ls = (cb * decay * dtr).astype(BF16)
            c_exp = (c_f * jnp.exp(col_b)).astype(BF16)
            x_h = xs_b[:, h * SSD_HEAD_DIM:(h + 1) * SSD_HEAD_DIM]
            s_prev = state_ref[h]
            ys.append(_dot(w_ls, x_h) + _dot(c_exp, s_prev.astype(BF16)))
            to_end = decay[L - 1:L, :] * dtr
            new = _dot((bt_g * to_end).astype(BF16), x_h)
            chunk_decay = jnp.exp(col_b[L - 1:L, :SSD_HEAD_DIM])
            state_ref[h] = s_prev * chunk_decay + new
    y = jnp.concatenate(ys, axis=-1)

    y = y + dskip * xs_b.astype(F32)
    y = y * _silu(z.astype(F32))
    gw = SSD_WIDTH // SSD_GROUPS
    outs = []
    for g in range(SSD_GROUPS):
        yg = y[:, g * gw:(g + 1) * gw]
        ms = jnp.mean(yg * yg, axis=-1, keepdims=True)
        outs.append(yg * lax.rsqrt(ms + EPS))
    return (jnp.concatenate(outs, axis=-1) * nw).astype(BF16)


def _ssd_kernel(xbc_ref, z_ref, dt_ref, cum_ref, dskip_ref, nw_ref, o_ref, state_ref):
    @pl.when(pl.program_id(1) == 0)
    def _():
        state_ref[...] = jnp.zeros_like(state_ref)

    for r in range(SSD_CHUNKS_PER_STEP):
        rows = slice(r * SSD_CHUNK, (r + 1) * SSD_CHUNK)
        o_ref[rows, :] = _ssd_chunk(xbc_ref[rows, :], z_ref[rows, :], dt_ref[rows, :],
                                    cum_ref[rows, :], dskip_ref[...], nw_ref[...],
                                    state_ref)


def _ssd(xbc, z, dt, cum, dskip, nw, *, batch, seq):
    step_rows = SSD_CHUNKS_PER_STEP * SSD_CHUNK
    ns = seq // step_rows

    def row(n):
        return pl.BlockSpec((step_rows, n), lambda b, c: (b * ns + c, 0))

    return pl.pallas_call(
        _ssd_kernel,
        out_shape=jax.ShapeDtypeStruct((batch * seq, SSD_WIDTH), BF16),
        grid=(batch, ns),
        in_specs=[row(CONV_CH), row(SSD_WIDTH), row(DT_PAD), row(DT_PAD),
                  _resident((1, SSD_WIDTH)), _resident((1, SSD_WIDTH))],
        out_specs=row(SSD_WIDTH),
        scratch_shapes=[pltpu.VMEM((SSD_HEADS, SSD_STATE, SSD_HEAD_DIM), F32)],
        compiler_params=_params("parallel", "arbitrary"),
        name="ssd",
    )(xbc, z, dt, cum, dskip, nw)


def _rel_bias_kernel(rb_ref, o_ref):
    h = pl.program_id(0)
    d = pl.program_id(1)
    k_i = lax.broadcasted_iota(jnp.int32, (BIAS_TILE, BIAS_TILE), 0)
    q_i = lax.broadcasted_iota(jnp.int32, (BIAS_TILE, BIAS_TILE), 1)
    n = q_i - k_i + d * BIAS_TILE
    max_exact = NUM_BUCKETS // 2
    nf = jnp.maximum(n, 1).astype(F32)
    large = max_exact + (jnp.log(nf / max_exact) / math.log(MAX_DISTANCE / max_exact)
                         * (NUM_BUCKETS - max_exact)).astype(jnp.int32)
    large = jnp.minimum(large, NUM_BUCKETS - 1)
    bucket = jnp.where(n < max_exact, n, large)
    tile = jnp.zeros((BIAS_TILE, BIAS_TILE), F32)
    for b in range(NUM_BUCKETS):
        tile = jnp.where(bucket == b, rb_ref[b, h], tile)
    tile = tile - rb_ref[NUM_BUCKETS - 1, h]
    o_ref[...] = jnp.where(n >= 0, tile, -jnp.inf)


def _rel_bias_tiles(rel_bias):
    return pl.pallas_call(
        _rel_bias_kernel,
        out_shape=jax.ShapeDtypeStruct((DA_HEADS, 2, BIAS_TILE, BIAS_TILE), F32),
        grid=(DA_HEADS, 2),
        in_specs=[pl.BlockSpec(memory_space=pltpu.SMEM)],
        out_specs=pl.BlockSpec((None, None, BIAS_TILE, BIAS_TILE),
                               lambda h, d: (h, d, 0, 0)),
        compiler_params=_params("parallel", "parallel"),
        name="rel_bias",
    )(rel_bias)


def _attn_kernel(q_ref, k_ref, v_ref, bt_ref, lam_ref, sw_ref, o_ref,
                 vt_ref, qz_ref, s0_ref, s1_ref, m_ref, acc_ref, *, lambda_init, seq):
    i = pl.program_id(2)
    tb = ATTN_BLOCK
    dh = DA_HEAD_DIM
    dv = DA_V_DIM

    @pl.when(i == 0)
    def _():
        for j in range(seq // tb):
            vt = v_ref[j * tb:(j + 1) * tb, :].astype(F32).T
            vt_ref[j, 0:dv, :] = vt.astype(BF16)
            vt_ref[j, dv:dv + ATTN_ONES_ROWS, :] = jnp.ones((ATTN_ONES_ROWS, tb), BF16)

    qt = q_ref[...].astype(F32).T
    feat = lax.broadcasted_iota(jnp.int32, qt.shape, 0)
    for c in range(2):
        in_map = (feat >= c * dh) & (feat < (c + 1) * dh)
        qz_ref[c] = jnp.where(in_map, qt, 0.0).astype(BF16)
    acc_ref[...] = jnp.zeros_like(acc_ref)
    m_ref[...] = jnp.full(m_ref.shape, -jnp.inf, F32)

    s_refs = (s0_ref, s1_ref)
    half = tb // 2
    bt = BIAS_TILE
    nsub = tb // bt
    full = ((0, tb, 0, tb),)
    diag_parts = ((0, half, 0, half), (0, tb, half, tb))

    def scores(c, j, diag):
        k0 = pl.multiple_of(j * tb, tb)
        if not diag:
            s_refs[c][...] = _dot(k_ref[pl.ds(k0, tb), :], qz_ref[c])
        else:
            s_refs[c][0:half, :] = _dot(k_ref[pl.ds(k0, half), :], qz_ref[c])
            s_refs[c][half:tb, half:tb] = _dot(k_ref[pl.ds(k0 + half, half), :],
                                               qz_ref[c, :, half:tb])

    def add_bias(c, kind):
        s_ref = s_refs[c]
        if kind == "near":
            s_ref[tb - bt:tb, 0:bt] = s_ref[tb - bt:tb, 0:bt] + bt_ref[1]
        elif kind == "diag":
            for a in range(nsub):
                ks = slice(a * bt, (a + 1) * bt)
                s_ref[ks, ks] = s_ref[ks, ks] + bt_ref[0]
                if a + 1 < nsub:
                    qs = slice((a + 1) * bt, (a + 2) * bt)
                    s_ref[ks, qs] = s_ref[ks, qs] + bt_ref[1]
            for k0, k1, q0, q1 in diag_parts:
                for a in range(k0 // bt, k1 // bt):
                    for b in range(q0 // bt, q1 // bt):
                        if a > b:
                            s_ref[a * bt:(a + 1) * bt, b * bt:(b + 1) * bt] = (
                                jnp.full((bt, bt), -jnp.inf, F32))

    def softmax_pv(c, j, parts):
        for k0, k1, q0, q1 in parts:
            s = s_refs[c][k0:k1, q0:q1]
            m_old = m_ref[c, :, q0:q1]
            m_new = jnp.maximum(m_old, jnp.max(s, axis=0, keepdims=True))
            p = jnp.exp(s - m_new).astype(BF16)
            acc_ref[c, :, q0:q1] = (acc_ref[c, :, q0:q1] * jnp.exp(m_old - m_new)
                                    + _dot(vt_ref[j, :, k0:k1], p))
            m_ref[c, :, q0:q1] = m_new

    def block(j, kind, nxt, nxt_diag=False):
        parts = diag_parts if kind == "diag" else full
        scores(1, j, kind == "diag")
        add_bias(0, kind)
        softmax_pv(0, j, parts)
        if nxt is not None:
            scores(0, nxt, nxt_diag)
        add_bias(1, kind)
        softmax_pv(1, j, parts)

    @pl.when(i == 0)
    def _():
        scores(0, 0, True)

    @pl.when(i >= 1)
    def _():
        scores(0, 0, False)

    def far(j, carry):
        block(j, "far", j + 1)
        return carry

    lax.fori_loop(0, jnp.maximum(i - 1, 0), far, 0)

    @pl.when(i >= 1)
    def _():
        block(i - 1, "near", i, nxt_diag=True)

    block(i, "diag", None)

    lp = lam_ref[...]
    lam = (jnp.exp(jnp.sum(lp[0:1] * lp[1:2], axis=-1, keepdims=True))
           - jnp.exp(jnp.sum(lp[2:3] * lp[3:4], axis=-1, keepdims=True))
           + lambda_init)
    a0 = acc_ref[0]
    a1 = acc_ref[1]
    o = a0[0:dv] / a0[dv:dv + 1] - lam * (a1[0:dv] / a1[dv:dv + 1])
    o = o * lax.rsqrt(jnp.mean(o * o, axis=0, keepdims=True) + EPS)
    o = o * sw_ref[...] * (1.0 - lambda_init)
    o_ref[...] = o.T.astype(BF16)


def _attn(q, k, v, tiles, lam_p, sw, *, batch, seq, lambda_init):
    nq = seq // ATTN_BLOCK
    kv = pl.BlockSpec((seq, DA_V_DIM), lambda b, h, i: (b, h))
    qo = pl.BlockSpec((ATTN_BLOCK, DA_V_DIM), lambda b, h, i: (b * nq + i, h))
    acc_rows = DA_V_DIM + ATTN_ONES_ROWS
    return pl.pallas_call(
        functools.partial(_attn_kernel, lambda_init=lambda_init, seq=seq),
        out_shape=jax.ShapeDtypeStruct((batch * seq, DA_WIDTH), BF16),
        grid=(batch, DA_HEADS, nq),
        in_specs=[qo, kv, kv,
                  pl.BlockSpec((None, 2, BIAS_TILE, BIAS_TILE),
                               lambda b, h, i: (h, 0, 0, 0)),
                  _resident((4, DA_HEAD_DIM)), _resident((DA_V_DIM, ATTN_BLOCK))],
        out_specs=qo,
        scratch_shapes=[pltpu.VMEM((nq, acc_rows, ATTN_BLOCK), BF16),
                        pltpu.VMEM((2, DA_V_DIM, ATTN_BLOCK), BF16),
                        pltpu.VMEM((ATTN_BLOCK, ATTN_BLOCK), F32),
                        pltpu.VMEM((ATTN_BLOCK, ATTN_BLOCK), F32),
                        pltpu.VMEM((2, 1, ATTN_BLOCK), F32),
                        pltpu.VMEM((2, acc_rows, ATTN_BLOCK), F32)],
        compiler_params=_params("parallel", "parallel", "arbitrary"),
        name="diff_attn",
    )(q, k, v, tiles, lam_p, sw)


def _out_proj_kernel(x_ref, ys_ref, ya_ref, ws_ref, wa_ref, o_ref):
    o_ref[...] = (x_ref[...] + _dot(ys_ref[...], ws_ref[...])
                  + _dot(ya_ref[...], wa_ref[...]))


def _out_proj(x1, y_ssd, y_attn, w_s, w_a):
    t = x1.shape[0]
    row = pl.BlockSpec((ROW_TILE, D_MODEL), lambda i: (i, 0))
    return pl.pallas_call(
        _out_proj_kernel,
        out_shape=jax.ShapeDtypeStruct((t, D_MODEL), F32),
        grid=(t // ROW_TILE,),
        in_specs=[row, row, row, _resident((SSD_WIDTH, D_MODEL)),
                  _resident((DA_WIDTH, D_MODEL))],
        out_specs=row,
        compiler_params=_params("parallel"),
        name="out_proj",
    )(x1, y_ssd, y_attn, w_s, w_a)


def _mem_kv_kernel(mem_ref, nw_ref, wk_ref, wv_ref, k_ref, v_ref):
    mn = _rms(mem_ref[...], nw_ref[...]).astype(BF16)
    k_ref[...] = _dot(mn, wk_ref[...]).astype(BF16)
    v_ref[...] = _dot(mn, wv_ref[...]).astype(BF16)


def _mem_kv(mem, nw, wk, wv):
    t = mem.shape[0]
    row = pl.BlockSpec((MEM_LEN, D_MODEL), lambda i: (i, 0))
    return pl.pallas_call(
        _mem_kv_kernel,
        out_shape=(jax.ShapeDtypeStruct((t, D_MODEL), BF16),
                   jax.ShapeDtypeStruct((t, D_MODEL), BF16)),
        grid=(t // MEM_LEN,),
        in_specs=[row, _resident((1, D_MODEL)), _resident((D_MODEL, D_MODEL)),
                  _resident((D_MODEL, D_MODEL))],
        out_specs=(row, row),
        compiler_params=_params("parallel"),
        name="mem_kv",
    )(mem, nw, wk, wv)


def _cross_kernel(x_ref, nw_ref, wq_ref, k_ref, v_ref, wo_ref, o_ref):
    x = x_ref[...]
    h = _rms(x, nw_ref[...]).astype(BF16)
    q = (_dot(h, wq_ref[...]) * (CROSS_HEAD_DIM ** -0.5)).astype(BF16)
    outs = []
    for hh in range(CROSS_HEADS):
        sl = slice(hh * CROSS_HEAD_DIM, (hh + 1) * CROSS_HEAD_DIM)
        s = _dot_nt(q[:, sl], k_ref[:, sl])
        p = jnp.exp(s - jnp.max(s, axis=-1, keepdims=True))
        o = _dot(p.astype(BF16), v_ref[:, sl])
        outs.append(o / jnp.sum(p, axis=-1, keepdims=True))
    o = jnp.concatenate(outs, axis=-1).astype(BF16)
    o_ref[...] = x + _dot(o, wo_ref[...])


def _cross(x2, nw, wq, k, v, wo, *, seq):
    t = x2.shape[0]
    tiles_per_batch = seq // ROW_TILE
    row = pl.BlockSpec((ROW_TILE, D_MODEL), lambda i: (i, 0))
    mem = pl.BlockSpec((MEM_LEN, D_MODEL), lambda i: (i // tiles_per_batch, 0))
    return pl.pallas_call(
        _cross_kernel,
        out_shape=jax.ShapeDtypeStruct((t, D_MODEL), F32),
        grid=(t // ROW_TILE,),
        in_specs=[row, _resident((1, D_MODEL)), _resident((D_MODEL, D_MODEL)),
                  mem, mem, _resident((D_MODEL, D_MODEL))],
        out_specs=row,
        compiler_params=_params("parallel"),
        name="cross_attn",
    )(x2, nw, wq, k, v, wo)


def kernel(x, mem, norm_ffn1_w, ffn1_w_in, ffn1_w_out, norm_mix_w, w_in_mix, conv_w, conv_b, dt_bias, a_log, d_skip, ssd_norm_w, lambda_q1, lambda_k1, lambda_q2, lambda_k2, subln_w, rel_bias, w_out_mix, norm_cross_w, norm_mem_w, w_cq, w_ck, w_cv, w_co, norm_ffn2_w, ffn2_w_in, ffn2_w_out, norm_final_w):
    batch, seq, _ = x.shape
    depth = ffn1_w_in.shape[0]
    t = batch * seq
    xf = x.reshape(t, D_MODEL)
    memf = mem.reshape(batch * MEM_LEN, D_MODEL)
    fw = norm_final_w.reshape(1, D_MODEL)
    tiles = _rel_bias_tiles(rel_bias)

    def vec(p):
        return p.reshape(1, -1)

    def pad_lanes(p):
        return jnp.pad(p.reshape(1, -1), ((0, 0), (0, DT_PAD - p.shape[-1])))

    for l in range(depth):
        lambda_init = 0.8 - 0.6 * math.exp(-0.3 * l)
        xf = _ffn(xf, vec(norm_ffn1_w[l]), ffn1_w_in[l].astype(BF16),
                  ffn1_w_out[l].astype(BF16), fw, final=False, name="ffn1")

        wm = w_in_mix[l].astype(BF16)
        o_xbc = SSD_WIDTH
        o_dt = o_xbc + CONV_CH
        o_q = o_dt + SSD_HEADS
        o_k = o_q + DA_WIDTH
        o_v = o_k + DA_WIDTH
        wdt = jnp.pad(wm[:, o_dt:o_q], ((0, 0), (0, DT_PAD - SSD_HEADS)))
        z, xbc, dt, cum, q, k, v = _in_proj(
            xf, vec(norm_mix_w[l]), wm[:, :o_xbc], wm[:, o_xbc:o_dt], wdt,
            wm[:, o_q:o_k], wm[:, o_k:o_v], wm[:, o_v:],
            conv_w[l], vec(conv_b[l]), pad_lanes(dt_bias[l]), pad_lanes(a_log[l]),
            seq=seq)

        y_ssd = _ssd(xbc, z, dt, cum, vec(jnp.repeat(d_skip[l], SSD_HEAD_DIM)),
                     vec(ssd_norm_w[l]), batch=batch, seq=seq)

        lam_p = jnp.stack([lambda_q1[l], lambda_k1[l], lambda_q2[l], lambda_k2[l]])
        sw = jnp.broadcast_to(subln_w[l].reshape(DA_V_DIM, 1), (DA_V_DIM, ATTN_BLOCK))
        y_attn = _attn(q, k, v, tiles, lam_p, sw,
                       batch=batch, seq=seq, lambda_init=lambda_init)

        wo = w_out_mix[l].astype(BF16)
        xf = _out_proj(xf, y_ssd, y_attn, wo[:SSD_WIDTH], wo[SSD_WIDTH:])

        mk, mv = _mem_kv(memf, vec(norm_mem_w[l]), w_ck[l].astype(BF16),
                         w_cv[l].astype(BF16))
        xf = _cross(xf, vec(norm_cross_w[l]), w_cq[l].astype(BF16), mk, mv,
                    w_co[l].astype(BF16), seq=seq)

        xf = _ffn(xf, vec(norm_ffn2_w[l]), ffn2_w_in[l].astype(BF16),
                  ffn2_w_out[l].astype(BF16), fw, final=(l == depth - 1),
                  name="ffn2")
    return xf.reshape(batch, seq, D_MODEL)
```

```python
import functools
import math

import jax
import jax.numpy as jnp
from jax import lax
from jax.experimental import pallas as pl
from jax.experimental.pallas import tpu as pltpu

F32 = jnp.float32
BF16 = jnp.bfloat16

D_MODEL = 1024
MEM_LEN = 256
EPS = 1e-6

SSD_HEADS = 16
SSD_HEAD_DIM = 64
SSD_WIDTH = SSD_HEADS * SSD_HEAD_DIM
SSD_GROUPS = 4
SSD_STATE = 128
CONV_WIDTH = 4
SSD_CHUNK = 128
CONV_CH = SSD_WIDTH + 2 * SSD_GROUPS * SSD_STATE
HEADS_PER_GROUP = SSD_HEADS // SSD_GROUPS

DA_HEADS = 8
DA_HEAD_DIM = 64
DA_V_DIM = 2 * DA_HEAD_DIM
DA_WIDTH = DA_HEADS * DA_V_DIM

NUM_BUCKETS = 32
MAX_DISTANCE = 128

CROSS_HEADS = 4
CROSS_HEAD_DIM = D_MODEL // CROSS_HEADS

D_FF = 2816
LOG2_E = math.log2(math.e)

LANES = 128
SUBLANES = 8
MXU_DIM = 256
VMEM_LIMIT_BYTES = 56 * 1024 * 1024

ROW_TILE = 512
FF_CHUNK = MXU_DIM
ATTN_BLOCK = 512
ATTN_ONES_ROWS = 16
BIAS_TILE = MAX_DISTANCE
DT_PAD = LANES
CONV_HALO = SUBLANES
SSD_CHUNKS_PER_STEP = 2


def _resident(shape):
    nd = len(shape)
    return pl.BlockSpec(shape, lambda *_: (0,) * nd, pipeline_mode=pl.Buffered(1))


def _params(*sem):
    return pltpu.CompilerParams(dimension_semantics=sem,
                                vmem_limit_bytes=VMEM_LIMIT_BYTES)


def _rms(x, w):
    ms = jnp.mean(x * x, axis=-1, keepdims=True)
    return x * lax.rsqrt(ms + EPS) * w


def _silu(x):
    return x / (1.0 + jnp.exp(-x))


def _dot(a, b):
    return jnp.dot(a, b, preferred_element_type=F32)


def _dot_nt(a, b):
    return lax.dot_general(a, b, (((1,), (1,)), ((), ())),
                           preferred_element_type=F32)


def _ffn_kernel(x_ref, nw_ref, win_ref, wout_ref, fw_ref, o_ref, *, final):
    x = x_ref[...]
    h = _rms(x, nw_ref[...]).astype(BF16)
    acc = None
    for c in range(D_FF // FF_CHUNK):
        lo = c * FF_CHUNK
        g = _dot(h, win_ref[:, lo:lo + FF_CHUNK])
        u = _dot(h, win_ref[:, D_FF + lo:D_FF + lo + FF_CHUNK])
        a = (_silu(g) * u).astype(BF16)
        d = _dot(a, wout_ref[lo:lo + FF_CHUNK, :])
        acc = d if acc is None else acc + d
    y = x + 0.5 * acc
    if final:
        y = _rms(y, fw_ref[...])
    o_ref[...] = y


def _ffn(x, nw, win, wout, fw, *, final, name):
    t = x.shape[0]
    row = pl.BlockSpec((ROW_TILE, D_MODEL), lambda i: (i, 0))
    return pl.pallas_call(
        functools.partial(_ffn_kernel, final=final),
        out_shape=jax.ShapeDtypeStruct((t, D_MODEL), F32),
        grid=(t // ROW_TILE,),
        in_specs=[row, _resident((1, D_MODEL)), _resident((D_MODEL, 2 * D_FF)),
                  _resident((D_FF, D_MODEL)), _resident((1, D_MODEL))],
        out_specs=row,
        compiler_params=_params("parallel"),
        name=name,
    )(x, nw, win, wout, fw)


def _in_proj_kernel(x_ref, nw_ref, wz_ref, wxbc_ref, wdt_ref, wq_ref, wk_ref, wv_ref,
                    cw_ref, cb_ref, dtb_ref, alog_ref,
                    z_ref, xbc_ref, dt_ref, cum_ref, q_ref, k_ref, v_ref, ext_ref,
                    *, tiles_per_seq):
    i = pl.program_id(0)
    tm = ROW_TILE
    h = _rms(x_ref[...], nw_ref[...]).astype(BF16)

    @pl.when(i % tiles_per_seq == 0)
    def _():
        ext_ref[:, 0:CONV_HALO, :] = jnp.zeros((CONV_CH // LANES, CONV_HALO, LANES), F32)

    x_dt = _dot(h, wdt_ref[...]) + dtb_ref[...]
    dt = jnp.maximum(x_dt, 0.0) + jnp.log1p(jnp.exp(-jnp.abs(x_dt)))
    dt_ref[...] = dt
    la = dt * -jnp.exp(alog_ref[...])
    la_hi = la.astype(BF16)
    la_mid = (la - la_hi.astype(F32)).astype(BF16)
    la_lo = (la - la_hi.astype(F32) - la_mid.astype(F32)).astype(BF16)
    row_i = lax.broadcasted_iota(jnp.int32, (SSD_CHUNK, SSD_CHUNK), 0)
    col_i = lax.broadcasted_iota(jnp.int32, (SSD_CHUNK, SSD_CHUNK), 1)
    tril = (col_i <= row_i).astype(BF16)

    wide = ((z_ref, wz_ref, 1.0), (q_ref, wq_ref, DA_HEAD_DIM ** -0.5 * LOG2_E),
            (k_ref, wk_ref, 1.0), (v_ref, wv_ref, 1.0))
    n_chunks = tm // SSD_CHUNK
    assert n_chunks == len(wide)
    for r in range(n_chunks):
        lo = r * SSD_CHUNK
        rows = slice(lo, lo + SSD_CHUNK)
        cum_ref[rows, :] = (_dot(tril, la_hi[rows]) + _dot(tril, la_mid[rows])
                            + _dot(tril, la_lo[rows]))
        u = _dot(h[rows], wxbc_ref[...])
        base = CONV_HALO + lo
        acts = []
        for t in range(CONV_CH // LANES):
            ln = slice(t * LANES, (t + 1) * LANES)
            ext_ref[t, base:base + SSD_CHUNK, :] = u[:, ln]
            acc = cb_ref[:, ln]
            for j in range(CONV_WIDTH):
                start = base - (CONV_WIDTH - 1) + j
                acc = acc + cw_ref[j:j + 1, ln] * ext_ref[t, start:start + SSD_CHUNK, :]
            acts.append(_silu(acc))
        xbc_ref[rows, :] = jnp.concatenate(acts, axis=-1).astype(BF16)
        o_ref, w_ref, scale = wide[r]
        out = _dot(h, w_ref[...])
        o_ref[...] = (out if scale == 1.0 else out * scale).astype(BF16)
    ext_ref[:, 0:CONV_HALO, :] = ext_ref[:, tm:tm + CONV_HALO, :]


def _in_proj(x1, nw, wz, wxbc, wdt, wq, wk, wv, cw, cb, dtb, alog, *, seq):
    t = x1.shape[0]

    def row(n):
        return pl.BlockSpec((ROW_TILE, n), lambda i: (i, 0))

    return pl.pallas_call(
        functools.partial(_in_proj_kernel, tiles_per_seq=seq // ROW_TILE),
        out_shape=(jax.ShapeDtypeStruct((t, SSD_WIDTH), BF16),
                   jax.ShapeDtypeStruct((t, CONV_CH), BF16),
                   jax.ShapeDtypeStruct((t, DT_PAD), F32),
                   jax.ShapeDtypeStruct((t, DT_PAD), F32),
                   jax.ShapeDtypeStruct((t, DA_WIDTH), BF16),
                   jax.ShapeDtypeStruct((t, DA_WIDTH), BF16),
                   jax.ShapeDtypeStruct((t, DA_WIDTH), BF16)),
        grid=(t // ROW_TILE,),
        in_specs=[row(D_MODEL), _resident((1, D_MODEL)),
                  _resident((D_MODEL, SSD_WIDTH)), _resident((D_MODEL, CONV_CH)),
                  _resident((D_MODEL, DT_PAD)), _resident((D_MODEL, DA_WIDTH)),
                  _resident((D_MODEL, DA_WIDTH)), _resident((D_MODEL, DA_WIDTH)),
                  _resident((CONV_WIDTH, CONV_CH)), _resident((1, CONV_CH)),
                  _resident((1, DT_PAD)), _resident((1, DT_PAD))],
        out_specs=(row(SSD_WIDTH), row(CONV_CH), row(DT_PAD), row(DT_PAD),
                   row(DA_WIDTH), row(DA_WIDTH), row(DA_WIDTH)),
        scratch_shapes=[pltpu.VMEM((CONV_CH // LANES, CONV_HALO + ROW_TILE + CONV_HALO,
                                    LANES), F32)],
        compiler_params=_params("arbitrary"),
        name="in_proj",
    )(x1, nw, wz, wxbc, wdt, wq, wk, wv, cw, cb, dtb, alog)


def _ssd_chunk(xbc, z, dt, cum_col, dskip, nw, state_ref):
    L = SSD_CHUNK
    xs_b = xbc[:, :SSD_WIDTH]
    row_i = lax.broadcasted_iota(jnp.int32, (L, L), 0)
    col_i = lax.broadcasted_iota(jnp.int32, (L, L), 1)
    causal = col_i <= row_i
    cum_row = cum_col.T
    dt_row = dt.T

    ys = []
    for g in range(SSD_GROUPS):
        b_g = xbc[:, SSD_WIDTH + g * SSD_STATE:SSD_WIDTH + (g + 1) * SSD_STATE]
        c_g = xbc[:, SSD_WIDTH + (SSD_GROUPS + g) * SSD_STATE:
                  SSD_WIDTH + (SSD_GROUPS + g + 1) * SSD_STATE]
        cb = _dot_nt(c_g, b_g)
        c_f = c_g.astype(F32)
        bt_g = b_g.astype(F32).T
        for r in range(HEADS_PER_GROUP):
            h = g * HEADS_PER_GROUP + r
            col_b = jnp.broadcast_to(cum_col[:, h:h + 1], (L, L))
            row_b = jnp.broadcast_to(cum_row[h:h + 1, :], (L, L))
            decay = jnp.exp(jnp.where(causal, col_b - row_b, -jnp.inf))
            dtr = dt_row[h:h + 1, :]
            w_ls = (cb * decay * dtr).astype(BF16)
            c_exp = (c_f * jnp.exp(col_b)).astype(BF16)
            x_h = xs_b[:, h * SSD_HEAD_DIM:(h + 1) * SSD_HEAD_DIM]
            s_prev = state_ref[h]
            ys.append(_dot(w_ls, x_h) + _dot(c_exp, s_prev.astype(BF16)))
            to_end = decay[L - 1:L, :] * dtr
            new = _dot((bt_g * to_end).astype(BF16), x_h)
            chunk_decay = jnp.exp(col_b[L - 1:L, :SSD_HEAD_DIM])
            state_ref[h] = s_prev * chunk_decay + new
    y = jnp.concatenate(ys, axis=-1)

    y = y + dskip * xs_b.astype(F32)
    y = y * _silu(z.astype(F32))
    gw = SSD_WIDTH // SSD_GROUPS
    outs = []
    for g in range(SSD_GROUPS):
        yg = y[:, g * gw:(g + 1) * gw]
        ms = jnp.mean(yg * yg, axis=-1, keepdims=True)
        outs.append(yg * lax.rsqrt(ms + EPS))
    return (jnp.concatenate(outs, axis=-1) * nw).astype(BF16)


def _ssd_kernel(xbc_ref, z_ref, dt_ref, cum_ref, dskip_ref, nw_ref, o_ref, state_ref):
    @pl.when(pl.program_id(1) == 0)
    def _():
        state_ref[...] = jnp.zeros_like(state_ref)

    for r in range(SSD_CHUNKS_PER_STEP):
        rows = slice(r * SSD_CHUNK, (r + 1) * SSD_CHUNK)
        o_ref[rows, :] = _ssd_chunk(xbc_ref[rows, :], z_ref[rows, :], dt_ref[rows, :],
                                    cum_ref[rows, :], dskip_ref[...], nw_ref[...],
                                    state_ref)


def _ssd(xbc, z, dt, cum, dskip, nw, *, batch, seq):
    step_rows = SSD_CHUNKS_PER_STEP * SSD_CHUNK
    ns = seq // step_rows

    def row(n):
        return pl.BlockSpec((step_rows, n), lambda b, c: (b * ns + c, 0))

    return pl.pallas_call(
        _ssd_kernel,
        out_shape=jax.ShapeDtypeStruct((batch * seq, SSD_WIDTH), BF16),
        grid=(batch, ns),
        in_specs=[row(CONV_CH), row(SSD_WIDTH), row(DT_PAD), row(DT_PAD),
                  _resident((1, SSD_WIDTH)), _resident((1, SSD_WIDTH))],
        out_specs=row(SSD_WIDTH),
        scratch_shapes=[pltpu.VMEM((SSD_HEADS, SSD_STATE, SSD_HEAD_DIM), F32)],
        compiler_params=_params("parallel", "arbitrary"),
        name="ssd",
    )(xbc, z, dt, cum, dskip, nw)


def _rel_bias_kernel(rb_ref, o_ref):
    h = pl.program_id(0)
    d = pl.program_id(1)
    k_i = lax.broadcasted_iota(jnp.int32, (BIAS_TILE, BIAS_TILE), 0)
    q_i = lax.broadcasted_iota(jnp.int32, (BIAS_TILE, BIAS_TILE), 1)
    n = q_i - k_i + d * BIAS_TILE
    max_exact = NUM_BUCKETS // 2
    nf = jnp.maximum(n, 1).astype(F32)
    large = max_exact + (jnp.log(nf / max_exact) / math.log(MAX_DISTANCE / max_exact)
                         * (NUM_BUCKETS - max_exact)).astype(jnp.int32)
    large = jnp.minimum(large, NUM_BUCKETS - 1)
    bucket = jnp.where(n < max_exact, n, large)
    tile = jnp.zeros((BIAS_TILE, BIAS_TILE), F32)
    for b in range(NUM_BUCKETS):
        tile = jnp.where(bucket == b, rb_ref[b, h], tile)
    tile = (tile - rb_ref[NUM_BUCKETS - 1, h]) * LOG2_E
    o_ref[...] = jnp.where(n >= 0, tile, -jnp.inf)


def _rel_bias_tiles(rel_bias):
    return pl.pallas_call(
        _rel_bias_kernel,
        out_shape=jax.ShapeDtypeStruct((DA_HEADS, 2, BIAS_TILE, BIAS_TILE), F32),
        grid=(DA_HEADS, 2),
        in_specs=[pl.BlockSpec(memory_space=pltpu.SMEM)],
        out_specs=pl.BlockSpec((None, None, BIAS_TILE, BIAS_TILE),
                               lambda h, d: (h, d, 0, 0)),
        compiler_params=_params("parallel", "parallel"),
        name="rel_bias",
    )(rel_bias)


def _attn_kernel(q_ref, k_ref, v_ref, bt_ref, lam_ref, sw_ref, o_ref,
                 vt_ref, qz_ref, s0_ref, s1_ref, m_ref, acc_ref, *, lambda_init, seq):
    tb = ATTN_BLOCK
    dh = DA_HEAD_DIM
    dv = DA_V_DIM
    nb = seq // tb

    for j in range(nb):
        vt = v_ref[j * tb:(j + 1) * tb, :].astype(F32).T
        vt_ref[j, 0:dv, :] = vt.astype(BF16)
        vt_ref[j, dv:dv + ATTN_ONES_ROWS, :] = jnp.ones((ATTN_ONES_ROWS, tb), BF16)

    for i in range(nb):
        qt = q_ref[i * tb:(i + 1) * tb, :].astype(F32).T
        feat = lax.broadcasted_iota(jnp.int32, qt.shape, 0)
        for c in range(2):
            in_map = (feat >= c * dh) & (feat < (c + 1) * dh)
            qz_ref[i, c] = jnp.where(in_map, qt, 0.0).astype(BF16)

    s_refs = (s0_ref, s1_ref)
    half = tb // 2
    bt = BIAS_TILE
    nsub = tb // bt
    full = ((0, tb, 0, tb),)
    diag_parts = ((0, half, 0, half), (0, tb, half, tb))

    def scores(c, step):
        i, j, kind = step
        k0 = j * tb
        if kind != "diag":
            s_refs[c][...] = _dot(k_ref[k0:k0 + tb, :], qz_ref[i, c])
        else:
            s_refs[c][0:half, :] = _dot(k_ref[k0:k0 + half, :], qz_ref[i, c])
            s_refs[c][half:tb, half:tb] = _dot(k_ref[k0 + half:k0 + tb, :],
                                               qz_ref[i, c, :, half:tb])

    def add_bias(c, kind):
        s_ref = s_refs[c]
        if kind == "near":
            s_ref[tb - bt:tb, 0:bt] = s_ref[tb - bt:tb, 0:bt] + bt_ref[1]
        elif kind == "diag":
            for a in range(nsub):
                ks = slice(a * bt, (a + 1) * bt)
                s_ref[ks, ks] = s_ref[ks, ks] + bt_ref[0]
                if a + 1 < nsub:
                    qs = slice((a + 1) * bt, (a + 2) * bt)
                    s_ref[ks, qs] = s_ref[ks, qs] + bt_ref[1]
            for k0, k1, q0, q1 in diag_parts:
                for a in range(k0 // bt, k1 // bt):
                    for b in range(q0 // bt, q1 // bt):
                        if a > b:
                            s_ref[a * bt:(a + 1) * bt, b * bt:(b + 1) * bt] = (
                                jnp.full((bt, bt), -jnp.inf, F32))

    def softmax_pv(c, step):
        i, j, kind = step
        for k0, k1, q0, q1 in (diag_parts if kind == "diag" else full):
            s = s_refs[c][k0:k1, q0:q1]
            m_blk = jnp.max(s, axis=0, keepdims=True)
            pv_args = (vt_ref[j, :, k0:k1],)
            if j == 0:
                m_new = m_blk
                acc_ref[i, c, :, q0:q1] = _dot(*pv_args,
                                               jnp.exp2(s - m_new).astype(BF16))
            else:
                m_old = m_ref[i, c, :, q0:q1]
                m_new = jnp.maximum(m_old, m_blk)
                p = jnp.exp2(s - m_new).astype(BF16)
                acc_ref[i, c, :, q0:q1] = (acc_ref[i, c, :, q0:q1]
                                           * jnp.exp2(m_old - m_new) + _dot(*pv_args, p))
            m_ref[i, c, :, q0:q1] = m_new

    steps = []
    for j in range(nb):
        for i in range(j, nb):
            steps.append((i, j, "diag" if i == j else "near" if i == j + 1 else "far"))

    scores(0, steps[0])
    for t, step in enumerate(steps):
        scores(1, step)
        add_bias(0, step[2])
        softmax_pv(0, step)
        if t + 1 < len(steps):
            scores(0, steps[t + 1])
        add_bias(1, step[2])
        softmax_pv(1, step)

    lp = lam_ref[...]
    lam = (jnp.exp(jnp.sum(lp[0:1] * lp[1:2], axis=-1, keepdims=True))
           - jnp.exp(jnp.sum(lp[2:3] * lp[3:4], axis=-1, keepdims=True))
           + lambda_init)
    for i in range(nb):
        a0 = acc_ref[i, 0]
        a1 = acc_ref[i, 1]
        o = a0[0:dv] / a0[dv:dv + 1] - lam * (a1[0:dv] / a1[dv:dv + 1])
        o = o * lax.rsqrt(jnp.mean(o * o, axis=0, keepdims=True) + EPS)
        o = o * sw_ref[...] * (1.0 - lambda_init)
        o_ref[i * tb:(i + 1) * tb, :] = o.T.astype(BF16)


def _attn(q, k, v, tiles, lam_p, sw, *, batch, seq, lambda_init):
    nb = seq // ATTN_BLOCK
    seq_head = pl.BlockSpec((seq, DA_V_DIM), lambda b, h: (b, h))
    acc_rows = DA_V_DIM + ATTN_ONES_ROWS
    return pl.pallas_call(
        functools.partial(_attn_kernel, lambda_init=lambda_init, seq=seq),
        out_shape=jax.ShapeDtypeStruct((batch * seq, DA_WIDTH), BF16),
        grid=(batch, DA_HEADS),
        in_specs=[seq_head, seq_head, seq_head,
                  pl.BlockSpec((None, 2, BIAS_TILE, BIAS_TILE),
                               lambda b, h: (h, 0, 0, 0)),
                  _resident((4, DA_HEAD_DIM)), _resident((DA_V_DIM, ATTN_BLOCK))],
        out_specs=seq_head,
        scratch_shapes=[pltpu.VMEM((nb, acc_rows, ATTN_BLOCK), BF16),
                        pltpu.VMEM((nb, 2, DA_V_DIM, ATTN_BLOCK), BF16),
                        pltpu.VMEM((ATTN_BLOCK, ATTN_BLOCK), F32),
                        pltpu.VMEM((ATTN_BLOCK, ATTN_BLOCK), F32),
                        pltpu.VMEM((nb, 2, 1, ATTN_BLOCK), F32),
                        pltpu.VMEM((nb, 2, acc_rows, ATTN_BLOCK), F32)],
        compiler_params=_params("parallel", "parallel"),
        name="diff_attn",
    )(q, k, v, tiles, lam_p, sw)


def _out_proj_kernel(x_ref, ys_ref, ya_ref, ws_ref, wa_ref, o_ref):
    o_ref[...] = (x_ref[...] + _dot(ys_ref[...], ws_ref[...])
                  + _dot(ya_ref[...], wa_ref[...]))


def _out_proj(x1, y_ssd, y_attn, w_s, w_a):
    t = x1.shape[0]
    row = pl.BlockSpec((ROW_TILE, D_MODEL), lambda i: (i, 0))
    return pl.pallas_call(
        _out_proj_kernel,
        out_shape=jax.ShapeDtypeStruct((t, D_MODEL), F32),
        grid=(t // ROW_TILE,),
        in_specs=[row, row, row, _resident((SSD_WIDTH, D_MODEL)),
                  _resident((DA_WIDTH, D_MODEL))],
        out_specs=row,
        compiler_params=_params("parallel"),
        name="out_proj",
    )(x1, y_ssd, y_attn, w_s, w_a)


def _mem_kv_kernel(mem_ref, nw_ref, wk_ref, wv_ref, k_ref, v_ref):
    mn = _rms(mem_ref[...], nw_ref[...]).astype(BF16)
    k_ref[...] = _dot(mn, wk_ref[...]).astype(BF16)
    v_ref[...] = _dot(mn, wv_ref[...]).astype(BF16)


def _mem_kv(mem, nw, wk, wv):
    t = mem.shape[0]
    row = pl.BlockSpec((MEM_LEN, D_MODEL), lambda i: (i, 0))
    return pl.pallas_call(
        _mem_kv_kernel,
        out_shape=(jax.ShapeDtypeStruct((t, D_MODEL), BF16),
                   jax.ShapeDtypeStruct((t, D_MODEL), BF16)),
        grid=(t // MEM_LEN,),
        in_specs=[row, _resident((1, D_MODEL)), _resident((D_MODEL, D_MODEL)),
                  _resident((D_MODEL, D_MODEL))],
        out_specs=(row, row),
        compiler_params=_params("parallel"),
        name="mem_kv",
    )(mem, nw, wk, wv)


def _cross_kernel(x_ref, nw_ref, wq_ref, k_ref, v_ref, wo_ref, o_ref):
    x = x_ref[...]
    h = _rms(x, nw_ref[...]).astype(BF16)
    q = (_dot(h, wq_ref[...]) * (CROSS_HEAD_DIM ** -0.5)).astype(BF16)
    outs = []
    for hh in range(CROSS_HEADS):
        sl = slice(hh * CROSS_HEAD_DIM, (hh + 1) * CROSS_HEAD_DIM)
        s = _dot_nt(q[:, sl], k_ref[:, sl])
        p = jnp.exp(s - jnp.max(s, axis=-1, keepdims=True))
        o = _dot(p.astype(BF16), v_ref[:, sl])
        outs.append(o / jnp.sum(p, axis=-1, keepdims=True))
    o = jnp.concatenate(outs, axis=-1).astype(BF16)
    o_ref[...] = x + _dot(o, wo_ref[...])


def _cross(x2, nw, wq, k, v, wo, *, seq):
    t = x2.shape[0]
    tiles_per_batch = seq // ROW_TILE
    row = pl.BlockSpec((ROW_TILE, D_MODEL), lambda i: (i, 0))
    mem = pl.BlockSpec((MEM_LEN, D_MODEL), lambda i: (i // tiles_per_batch, 0))
    return pl.pallas_call(
        _cross_kernel,
        out_shape=jax.ShapeDtypeStruct((t, D_MODEL), F32),
        grid=(t // ROW_TILE,),
        in_specs=[row, _resident((1, D_MODEL)), _resident((D_MODEL, D_MODEL)),
                  mem, mem, _resident((D_MODEL, D_MODEL))],
        out_specs=row,
        compiler_params=_params("parallel"),
        name="cross_attn",
    )(x2, nw, wq, k, v, wo)


def kernel(x, mem, norm_ffn1_w, ffn1_w_in, ffn1_w_out, norm_mix_w, w_in_mix, conv_w, conv_b, dt_bias, a_log, d_skip, ssd_norm_w, lambda_q1, lambda_k1, lambda_q2, lambda_k2, subln_w, rel_bias, w_out_mix, norm_cross_w, norm_mem_w, w_cq, w_ck, w_cv, w_co, norm_ffn2_w, ffn2_w_in, ffn2_w_out, norm_final_w):
    batch, seq, _ = x.shape
    depth = ffn1_w_in.shape[0]
    t = batch * seq
    xf = x.reshape(t, D_MODEL)
    memf = mem.reshape(batch * MEM_LEN, D_MODEL)
    fw = norm_final_w.reshape(1, D_MODEL)
    tiles = _rel_bias_tiles(rel_bias)

    def vec(p):
        return p.reshape(1, -1)

    def pad_lanes(p):
        return jnp.pad(p.reshape(1, -1), ((0, 0), (0, DT_PAD - p.shape[-1])))

    for l in range(depth):
        lambda_init = 0.8 - 0.6 * math.exp(-0.3 * l)
        xf = _ffn(xf, vec(norm_ffn1_w[l]), ffn1_w_in[l].astype(BF16),
                  ffn1_w_out[l].astype(BF16), fw, final=False, name="ffn1")

        wm = w_in_mix[l].astype(BF16)
        o_xbc = SSD_WIDTH
        o_dt = o_xbc + CONV_CH
        o_q = o_dt + SSD_HEADS
        o_k = o_q + DA_WIDTH
        o_v = o_k + DA_WIDTH
        wdt = jnp.pad(wm[:, o_dt:o_q], ((0, 0), (0, DT_PAD - SSD_HEADS)))
        z, xbc, dt, cum, q, k, v = _in_proj(
            xf, vec(norm_mix_w[l]), wm[:, :o_xbc], wm[:, o_xbc:o_dt], wdt,
            wm[:, o_q:o_k], wm[:, o_k:o_v], wm[:, o_v:],
            conv_w[l], vec(conv_b[l]), pad_lanes(dt_bias[l]), pad_lanes(a_log[l]),
            seq=seq)

        y_ssd = _ssd(xbc, z, dt, cum, vec(jnp.repeat(d_skip[l], SSD_HEAD_DIM)),
                     vec(ssd_norm_w[l]), batch=batch, seq=seq)

        lam_p = jnp.stack([lambda_q1[l], lambda_k1[l], lambda_q2[l], lambda_k2[l]])
        sw = jnp.broadcast_to(subln_w[l].reshape(DA_V_DIM, 1), (DA_V_DIM, ATTN_BLOCK))
        y_attn = _attn(q, k, v, tiles, lam_p, sw,
                       batch=batch, seq=seq, lambda_init=lambda_init)

        wo = w_out_mix[l].astype(BF16)
        xf = _out_proj(xf, y_ssd, y_attn, wo[:SSD_WIDTH], wo[SSD_WIDTH:])

        mk, mv = _mem_kv(memf, vec(norm_mem_w[l]), w_ck[l].astype(BF16),
                         w_cv[l].astype(BF16))
        xf = _cross(xf, vec(norm_cross_w[l]), w_cq[l].astype(BF16), mk, mv,
                    w_co[l].astype(BF16), seq=seq)

        xf = _ffn(xf, vec(norm_ffn2_w[l]), ffn2_w_in[l].astype(BF16),
                  ffn2_w_out[l].astype(BF16), fw, final=(l == depth - 1),
                  name="ffn2")
    return xf.reshape(batch, seq, D_MODEL)
```

```python
import functools
import math

import jax
import jax.numpy as jnp
from jax import lax
from jax.experimental import pallas as pl
from jax.experimental.pallas import tpu as pltpu

F32 = jnp.float32
BF16 = jnp.bfloat16

D_MODEL = 1024
MEM_LEN = 256
EPS = 1e-6

SSD_HEADS = 16
SSD_HEAD_DIM = 64
SSD_WIDTH = SSD_HEADS * SSD_HEAD_DIM
SSD_GROUPS = 4
SSD_STATE = 128
CONV_WIDTH = 4
SSD_CHUNK = 128
CONV_CH = SSD_WIDTH + 2 * SSD_GROUPS * SSD_STATE
HEADS_PER_GROUP = SSD_HEADS // SSD_GROUPS

DA_HEADS = 8
DA_HEAD_DIM = 64
DA_V_DIM = 2 * DA_HEAD_DIM
DA_WIDTH = DA_HEADS * DA_V_DIM

NUM_BUCKETS = 32
MAX_DISTANCE = 128

CROSS_HEADS = 4
CROSS_HEAD_DIM = D_MODEL // CROSS_HEADS

D_FF = 2816
LOG2_E = math.log2(math.e)

LANES = 128
SUBLANES = 8
MXU_DIM = 256
VMEM_LIMIT_BYTES = 56 * 1024 * 1024

ROW_TILE = 512
FF_CHUNK = MXU_DIM
ATTN_BLOCK = 512
ATTN_ONES_ROWS = 16
BIAS_TILE = MAX_DISTANCE
DT_PAD = LANES
CONV_HALO = SUBLANES
SSD_CHUNKS_PER_STEP = 2
MIX_ROW_CHUNK = 128


def _resident(shape):
    nd = len(shape)
    return pl.BlockSpec(shape, lambda *_: (0,) * nd, pipeline_mode=pl.Buffered(1))


def _params(*sem):
    return pltpu.CompilerParams(dimension_semantics=sem,
                                vmem_limit_bytes=VMEM_LIMIT_BYTES)


def _rms(x, w):
    ms = jnp.mean(x * x, axis=-1, keepdims=True)
    return x * lax.rsqrt(ms + EPS) * w


def _silu(x):
    return x / (1.0 + jnp.exp(-x))


def _dot(a, b):
    return jnp.dot(a, b, preferred_element_type=F32)


def _dot_nt(a, b):
    return lax.dot_general(a, b, (((1,), (1,)), ((), ())),
                           preferred_element_type=F32)


def _ffn_kernel(x_ref, nw_ref, win_ref, wout_ref, fw_ref, o_ref, *, final):
    x = x_ref[...]
    h = _rms(x, nw_ref[...]).astype(BF16)
    acc = None
    for c in range(D_FF // FF_CHUNK):
        lo = c * FF_CHUNK
        g = _dot(h, win_ref[:, lo:lo + FF_CHUNK])
        u = _dot(h, win_ref[:, D_FF + lo:D_FF + lo + FF_CHUNK])
        a = (_silu(g) * u).astype(BF16)
        d = _dot(a, wout_ref[lo:lo + FF_CHUNK, :])
        acc = d if acc is None else acc + d
    y = x + 0.5 * acc
    if final:
        y = _rms(y, fw_ref[...])
    o_ref[...] = y


def _ffn(x, nw, win, wout, fw, *, final, name):
    t = x.shape[0]
    row = pl.BlockSpec((ROW_TILE, D_MODEL), lambda i: (i, 0))
    return pl.pallas_call(
        functools.partial(_ffn_kernel, final=final),
        out_shape=jax.ShapeDtypeStruct((t, D_MODEL), F32),
        grid=(t // ROW_TILE,),
        in_specs=[row, _resident((1, D_MODEL)), _resident((D_MODEL, 2 * D_FF)),
                  _resident((D_FF, D_MODEL)), _resident((1, D_MODEL))],
        out_specs=row,
        compiler_params=_params("parallel"),
        name=name,
    )(x, nw, win, wout, fw)


def _in_proj_kernel(x_ref, nw_ref, wz_ref, wxbc_ref, wdt_ref, wq_ref, wk_ref, wv_ref,
                    cw_ref, cb_ref, dtb_ref, alog_ref,
                    z_ref, xbc_ref, cum_ref, src_ref, q_ref, k_ref, v_ref, ext_ref,
                    *, tiles_per_seq):
    i = pl.program_id(0)
    tm = ROW_TILE
    h = _rms(x_ref[...], nw_ref[...]).astype(BF16)

    @pl.when(i % tiles_per_seq == 0)
    def _():
        ext_ref[:, 0:CONV_HALO, :] = jnp.zeros((CONV_CH // LANES, CONV_HALO, LANES), F32)

    x_dt = _dot(h, wdt_ref[...]) + dtb_ref[...]
    dt = jnp.maximum(x_dt, 0.0) + jnp.log1p(jnp.exp(-jnp.abs(x_dt)))
    log2_dt = jnp.log2(dt)
    la = dt * (-LOG2_E * jnp.exp(alog_ref[...]))
    la_hi = la.astype(BF16)
    la_mid = (la - la_hi.astype(F32)).astype(BF16)
    la_lo = (la - la_hi.astype(F32) - la_mid.astype(F32)).astype(BF16)
    row_i = lax.broadcasted_iota(jnp.int32, (SSD_CHUNK, SSD_CHUNK), 0)
    col_i = lax.broadcasted_iota(jnp.int32, (SSD_CHUNK, SSD_CHUNK), 1)
    tril = (col_i <= row_i).astype(BF16)

    wide = ((z_ref, wz_ref, 1.0), (q_ref, wq_ref, DA_HEAD_DIM ** -0.5 * LOG2_E),
            (k_ref, wk_ref, 1.0), (v_ref, wv_ref, 1.0))
    n_chunks = tm // SSD_CHUNK
    assert n_chunks == len(wide)
    for r in range(n_chunks):
        lo = r * SSD_CHUNK
        rows = slice(lo, lo + SSD_CHUNK)
        cum = (_dot(tril, la_hi[rows]) + _dot(tril, la_mid[rows])
               + _dot(tril, la_lo[rows]))
        cum_ref[rows, :] = cum
        src_ref[rows, :] = cum - log2_dt[rows]
        u = _dot(h[rows], wxbc_ref[...])
        base = CONV_HALO + lo
        acts = []
        for t in range(CONV_CH // LANES):
            ln = slice(t * LANES, (t + 1) * LANES)
            ext_ref[t, base:base + SSD_CHUNK, :] = u[:, ln]
            acc = cb_ref[:, ln]
            for j in range(CONV_WIDTH):
                start = base - (CONV_WIDTH - 1) + j
                acc = acc + cw_ref[j:j + 1, ln] * ext_ref[t, start:start + SSD_CHUNK, :]
            acts.append(_silu(acc))
        xbc_ref[rows, :] = jnp.concatenate(acts, axis=-1).astype(BF16)
        o_ref, w_ref, scale = wide[r]
        out = _dot(h, w_ref[...])
        o_ref[...] = (out if scale == 1.0 else out * scale).astype(BF16)
    ext_ref[:, 0:CONV_HALO, :] = ext_ref[:, tm:tm + CONV_HALO, :]


def _in_proj(x1, nw, wz, wxbc, wdt, wq, wk, wv, cw, cb, dtb, alog, *, seq):
    t = x1.shape[0]

    def row(n):
        return pl.BlockSpec((ROW_TILE, n), lambda i: (i, 0))

    return pl.pallas_call(
        functools.partial(_in_proj_kernel, tiles_per_seq=seq // ROW_TILE),
        out_shape=(jax.ShapeDtypeStruct((t, SSD_WIDTH), BF16),
                   jax.ShapeDtypeStruct((t, CONV_CH), BF16),
                   jax.ShapeDtypeStruct((t, DT_PAD), F32),
                   jax.ShapeDtypeStruct((t, DT_PAD), F32),
                   jax.ShapeDtypeStruct((t, DA_WIDTH), BF16),
                   jax.ShapeDtypeStruct((t, DA_WIDTH), BF16),
                   jax.ShapeDtypeStruct((t, DA_WIDTH), BF16)),
        grid=(t // ROW_TILE,),
        in_specs=[row(D_MODEL), _resident((1, D_MODEL)),
                  _resident((D_MODEL, SSD_WIDTH)), _resident((D_MODEL, CONV_CH)),
                  _resident((D_MODEL, DT_PAD)), _resident((D_MODEL, DA_WIDTH)),
                  _resident((D_MODEL, DA_WIDTH)), _resident((D_MODEL, DA_WIDTH)),
                  _resident((CONV_WIDTH, CONV_CH)), _resident((1, CONV_CH)),
                  _resident((1, DT_PAD)), _resident((1, DT_PAD))],
        out_specs=(row(SSD_WIDTH), row(CONV_CH), row(DT_PAD), row(DT_PAD),
                   row(DA_WIDTH), row(DA_WIDTH), row(DA_WIDTH)),
        scratch_shapes=[pltpu.VMEM((CONV_CH // LANES, CONV_HALO + ROW_TILE + CONV_HALO,
                                    LANES), F32)],
        compiler_params=_params("arbitrary"),
        name="in_proj",
    )(x1, nw, wz, wxbc, wdt, wq, wk, wv, cw, cb, dtb, alog)


def _ssd_chunk(xbc, cum_col, src_col, state_ref):
    L = SSD_CHUNK
    hp = SSD_HEAD_DIM
    xs_b = xbc[:, :SSD_WIDTH]
    row_i = lax.broadcasted_iota(jnp.int32, (L, L), 0)
    col_i = lax.broadcasted_iota(jnp.int32, (L, L), 1)
    causal = col_i <= row_i
    first = col_i < hp
    keep = (first.astype(BF16), 1.0 - first.astype(BF16))
    src_row = src_col.T

    ys = []
    for g in range(SSD_GROUPS):
        b_g = xbc[:, SSD_WIDTH + g * SSD_STATE:SSD_WIDTH + (g + 1) * SSD_STATE]
        c_g = xbc[:, SSD_WIDTH + (SSD_GROUPS + g) * SSD_STATE:
                  SSD_WIDTH + (SSD_GROUPS + g + 1) * SSD_STATE]
        cb = _dot_nt(c_g, b_g)
        bt_g = b_g.astype(F32).T
        for pr in range(HEADS_PER_GROUP // 2):
            pair = g * (HEADS_PER_GROUP // 2) + pr
            x_pair = xs_b[:, pair * 2 * hp:(pair + 1) * 2 * hp]
            x_bd = jnp.concatenate([x_pair * keep[0], x_pair * keep[1]], axis=0)
            cols, w_parts, b_parts = [], [], []
            for t in range(2):
                h = 2 * pair + t
                col_b = jnp.broadcast_to(cum_col[:, h:h + 1], (L, L))
                row_b = jnp.broadcast_to(src_row[h:h + 1, :], (L, L))
                dec = jnp.exp2(jnp.where(causal, col_b - row_b, -jnp.inf))
                w_parts.append((cb * dec).astype(BF16))
                b_parts.append((bt_g * dec[L - 1:L, :]).astype(BF16))
                cols.append(col_b)
            from_start = jnp.exp2(jnp.where(first, cols[0], cols[1]))
            s_prev = state_ref[pair]
            ys.append(_dot(jnp.concatenate(w_parts, axis=1), x_bd)
                      + _dot(c_g, s_prev.astype(BF16)) * from_start)
            new = _dot(jnp.concatenate(b_parts, axis=1), x_bd)
            state_ref[pair] = s_prev * from_start[L - 1:L, :] + new
    return jnp.concatenate(ys, axis=-1).astype(BF16)


def _ssd_kernel(xbc_ref, cum_ref, src_ref, o_ref, state_ref):
    @pl.when(pl.program_id(1) == 0)
    def _():
        state_ref[...] = jnp.zeros_like(state_ref)

    for r in range(SSD_CHUNKS_PER_STEP):
        rows = slice(r * SSD_CHUNK, (r + 1) * SSD_CHUNK)
        o_ref[rows, :] = _ssd_chunk(xbc_ref[rows, :], cum_ref[rows, :], src_ref[rows, :],
                                    state_ref)


def _ssd_gate_norm(y, xs, z, dskip, nw):
    y = y.astype(F32) + dskip * xs.astype(F32)
    y = y * _silu(z.astype(F32))
    gw = SSD_WIDTH // SSD_GROUPS
    outs = []
    for g in range(SSD_GROUPS):
        yg = y[:, g * gw:(g + 1) * gw]
        ms = jnp.mean(yg * yg, axis=-1, keepdims=True)
        outs.append(yg * lax.rsqrt(ms + EPS))
    return (jnp.concatenate(outs, axis=-1) * nw).astype(BF16)


def _ssd(xbc, cum, src, *, batch, seq):
    step_rows = SSD_CHUNKS_PER_STEP * SSD_CHUNK
    ns = seq // step_rows

    def row(n):
        return pl.BlockSpec((step_rows, n), lambda b, c: (b * ns + c, 0))

    return pl.pallas_call(
        _ssd_kernel,
        out_shape=jax.ShapeDtypeStruct((batch * seq, SSD_WIDTH), BF16),
        grid=(batch, ns),
        in_specs=[row(CONV_CH), row(DT_PAD), row(DT_PAD)],
        out_specs=row(SSD_WIDTH),
        scratch_shapes=[pltpu.VMEM((SSD_HEADS // 2, SSD_STATE, 2 * SSD_HEAD_DIM), F32)],
        compiler_params=_params("parallel", "arbitrary"),
        name="ssd",
    )(xbc, cum, src)


def _rel_bias_kernel(rb_ref, o_ref):
    k_i = lax.broadcasted_iota(jnp.int32, (BIAS_TILE, BIAS_TILE), 0)
    q_i = lax.broadcasted_iota(jnp.int32, (BIAS_TILE, BIAS_TILE), 1)
    max_exact = NUM_BUCKETS // 2
    for d in range(2):
        n = q_i - k_i + d * BIAS_TILE
        nf = jnp.maximum(n, 1).astype(F32)
        large = max_exact + (jnp.log(nf / max_exact) / math.log(MAX_DISTANCE / max_exact)
                             * (NUM_BUCKETS - max_exact)).astype(jnp.int32)
        large = jnp.minimum(large, NUM_BUCKETS - 1)
        bucket = jnp.where(n < max_exact, n, large)
        hits = [bucket == b for b in range(NUM_BUCKETS)]
        for h in range(DA_HEADS):
            tile = jnp.zeros((BIAS_TILE, BIAS_TILE), F32)
            for b in range(NUM_BUCKETS):
                tile = jnp.where(hits[b], rb_ref[b, h], tile)
            tile = (tile - rb_ref[NUM_BUCKETS - 1, h]) * LOG2_E
            o_ref[h, d] = jnp.where(n >= 0, tile, -jnp.inf)


def _rel_bias_tiles(rel_bias):
    return pl.pallas_call(
        _rel_bias_kernel,
        out_shape=jax.ShapeDtypeStruct((DA_HEADS, 2, BIAS_TILE, BIAS_TILE), F32),
        in_specs=[pl.BlockSpec(memory_space=pltpu.SMEM)],
        out_specs=pl.BlockSpec(memory_space=pltpu.VMEM),
        compiler_params=_params(),
        name="rel_bias",
    )(rel_bias)


def _attn_kernel(q_ref, k_ref, v_ref, bt_ref, lam_ref, sw_ref, o_ref,
                 vt_ref, qz_ref, s0_ref, s1_ref, m_ref, acc_ref, *, lambda_init, seq):
    tb = ATTN_BLOCK
    dh = DA_HEAD_DIM
    dv = DA_V_DIM
    nb = seq // tb

    for j in range(nb):
        vt = v_ref[j * tb:(j + 1) * tb, :].astype(F32).T
        vt_ref[j, 0:dv, :] = vt.astype(BF16)
        vt_ref[j, dv:dv + ATTN_ONES_ROWS, :] = jnp.ones((ATTN_ONES_ROWS, tb), BF16)

    for i in range(nb):
        qt = q_ref[i * tb:(i + 1) * tb, :].astype(F32).T
        feat = lax.broadcasted_iota(jnp.int32, qt.shape, 0)
        for c in range(2):
            in_map = (feat >= c * dh) & (feat < (c + 1) * dh)
            qz_ref[i, c] = jnp.where(in_map, qt, 0.0).astype(BF16)

    s_refs = (s0_ref, s1_ref)
    half = tb // 2
    bt = BIAS_TILE
    nsub = tb // bt
    full = ((0, tb, 0, tb),)
    diag_parts = ((0, half, 0, half), (0, tb, half, tb))

    def scores(c, step):
        i, j, kind = step
        k0 = j * tb
        if kind != "diag":
            s_refs[c][...] = _dot(k_ref[k0:k0 + tb, :], qz_ref[i, c])
        else:
            s_refs[c][0:half, :] = _dot(k_ref[k0:k0 + half, :], qz_ref[i, c])
            s_refs[c][half:tb, half:tb] = _dot(k_ref[k0 + half:k0 + tb, :],
                                               qz_ref[i, c, :, half:tb])

    def add_bias(c, kind):
        s_ref = s_refs[c]
        if kind == "near":
            s_ref[tb - bt:tb, 0:bt] = s_ref[tb - bt:tb, 0:bt] + bt_ref[1]
        elif kind == "diag":
            for a in range(nsub):
                ks = slice(a * bt, (a + 1) * bt)
                s_ref[ks, ks] = s_ref[ks, ks] + bt_ref[0]
                if a + 1 < nsub:
                    qs = slice((a + 1) * bt, (a + 2) * bt)
                    s_ref[ks, qs] = s_ref[ks, qs] + bt_ref[1]
            for k0, k1, q0, q1 in diag_parts:
                for a in range(k0 // bt, k1 // bt):
                    for b in range(q0 // bt, q1 // bt):
                        if a > b:
                            s_ref[a * bt:(a + 1) * bt, b * bt:(b + 1) * bt] = (
                                jnp.full((bt, bt), -jnp.inf, F32))

    def softmax_pv(c, step):
        i, j, kind = step
        for k0, k1, q0, q1 in (diag_parts if kind == "diag" else full):
            s = s_refs[c][k0:k1, q0:q1]
            m_blk = jnp.max(s, axis=0, keepdims=True)
            pv_args = (vt_ref[j, :, k0:k1],)
            if j == 0:
                m_new = m_blk
                acc_ref[i, c, :, q0:q1] = _dot(*pv_args,
                                               jnp.exp2(s - m_new).astype(BF16))
            else:
                m_old = m_ref[i, c, :, q0:q1]
                m_new = jnp.maximum(m_old, m_blk)
                p = jnp.exp2(s - m_new).astype(BF16)
                acc_ref[i, c, :, q0:q1] = (acc_ref[i, c, :, q0:q1]
                                           * jnp.exp2(m_old - m_new) + _dot(*pv_args, p))
            m_ref[i, c, :, q0:q1] = m_new

    steps = []
    for j in range(nb):
        for i in range(j, nb):
            steps.append((i, j, "diag" if i == j else "near" if i == j + 1 else "far"))

    scores(0, steps[0])
    for t, step in enumerate(steps):
        scores(1, step)
        add_bias(0, step[2])
        softmax_pv(0, step)
        if t + 1 < len(steps):
            scores(0, steps[t + 1])
        add_bias(1, step[2])
        softmax_pv(1, step)

    lp = lam_ref[...]
    lam = (jnp.exp(jnp.sum(lp[0:1] * lp[1:2], axis=-1, keepdims=True))
           - jnp.exp(jnp.sum(lp[2:3] * lp[3:4], axis=-1, keepdims=True))
           + lambda_init)
    for i in range(nb):
        a0 = acc_ref[i, 0]
        a1 = acc_ref[i, 1]
        o = a0[0:dv] / a0[dv:dv + 1] - lam * (a1[0:dv] / a1[dv:dv + 1])
        o = o * lax.rsqrt(jnp.mean(o * o, axis=0, keepdims=True) + EPS)
        o = o * sw_ref[...] * (1.0 - lambda_init)
        o_ref[i * tb:(i + 1) * tb, :] = o.T.astype(BF16)


def _attn(q, k, v, tiles, lam_p, sw, *, batch, seq, lambda_init):
    nb = seq // ATTN_BLOCK
    seq_head = pl.BlockSpec((seq, DA_V_DIM), lambda b, h: (b, h))
    acc_rows = DA_V_DIM + ATTN_ONES_ROWS
    return pl.pallas_call(
        functools.partial(_attn_kernel, lambda_init=lambda_init, seq=seq),
        out_shape=jax.ShapeDtypeStruct((batch * seq, DA_WIDTH), BF16),
        grid=(batch, DA_HEADS),
        in_specs=[seq_head, seq_head, seq_head,
                  pl.BlockSpec((None, 2, BIAS_TILE, BIAS_TILE),
                               lambda b, h: (h, 0, 0, 0)),
                  _resident((4, DA_HEAD_DIM)), _resident((DA_V_DIM, ATTN_BLOCK))],
        out_specs=seq_head,
        scratch_shapes=[pltpu.VMEM((nb, acc_rows, ATTN_BLOCK), BF16),
                        pltpu.VMEM((nb, 2, DA_V_DIM, ATTN_BLOCK), BF16),
                        pltpu.VMEM((ATTN_BLOCK, ATTN_BLOCK), F32),
                        pltpu.VMEM((ATTN_BLOCK, ATTN_BLOCK), F32),
                        pltpu.VMEM((nb, 2, 1, ATTN_BLOCK), F32),
                        pltpu.VMEM((nb, 2, acc_rows, ATTN_BLOCK), F32)],
        compiler_params=_params("parallel", "parallel"),
        name="diff_attn",
    )(q, k, v, tiles, lam_p, sw)


def _mem_kv_kernel(mem_ref, nw_ref, wk_ref, wv_ref, k_ref, v_ref):
    mn = _rms(mem_ref[...], nw_ref[...]).astype(BF16)
    k_ref[...] = _dot(mn, wk_ref[...]).astype(BF16)
    v_ref[...] = _dot(mn, wv_ref[...]).astype(BF16)


def _mem_kv(mem, nw, wk, wv):
    t = mem.shape[0]
    row = pl.BlockSpec((MEM_LEN, D_MODEL), lambda i: (i, 0))
    return pl.pallas_call(
        _mem_kv_kernel,
        out_shape=(jax.ShapeDtypeStruct((t, D_MODEL), BF16),
                   jax.ShapeDtypeStruct((t, D_MODEL), BF16)),
        grid=(t // MEM_LEN,),
        in_specs=[row, _resident((1, D_MODEL)), _resident((D_MODEL, D_MODEL)),
                  _resident((D_MODEL, D_MODEL))],
        out_specs=(row, row),
        compiler_params=_params("parallel"),
        name="mem_kv",
    )(mem, nw, wk, wv)


def _mix_out_cross_kernel(x_ref, ys_ref, xs_ref, z_ref, ya_ref, dskip_ref, snw_ref,
                          ws_ref, wa_ref, nw_ref, wq_ref, k_ref, v_ref, wo_ref, o_ref):
    parts = []
    for r in range(ROW_TILE // MIX_ROW_CHUNK):
        rows = slice(r * MIX_ROW_CHUNK, (r + 1) * MIX_ROW_CHUNK)
        part = x_ref[rows, :] + _dot(ya_ref[rows, :], wa_ref[...])
        y_ssd = _ssd_gate_norm(ys_ref[rows, :], xs_ref[rows, :], z_ref[rows, :],
                               dskip_ref[...], snw_ref[...])
        parts.append(part + _dot(y_ssd, ws_ref[...]))
    x = jnp.concatenate(parts, axis=0)
    h = _rms(x, nw_ref[...]).astype(BF16)
    q = (_dot(h, wq_ref[...]) * (CROSS_HEAD_DIM ** -0.5)).astype(BF16)
    outs = []
    for hh in range(CROSS_HEADS):
        sl = slice(hh * CROSS_HEAD_DIM, (hh + 1) * CROSS_HEAD_DIM)
        s = _dot_nt(q[:, sl], k_ref[:, sl])
        p = jnp.exp(s - jnp.max(s, axis=-1, keepdims=True))
        o = _dot(p.astype(BF16), v_ref[:, sl])
        outs.append(o / jnp.sum(p, axis=-1, keepdims=True))
    o = jnp.concatenate(outs, axis=-1).astype(BF16)
    o_ref[...] = x + _dot(o, wo_ref[...])


def _mix_out_cross(x1, y_scan, xbc, z, y_attn, dskip, snw, w_s, w_a, nw, wq, k, v, wo,
                   *, seq):
    t = x1.shape[0]
    tiles_per_batch = seq // ROW_TILE
    row = pl.BlockSpec((ROW_TILE, D_MODEL), lambda i: (i, 0))
    mem = pl.BlockSpec((MEM_LEN, D_MODEL), lambda i: (i // tiles_per_batch, 0))
    assert SSD_WIDTH == D_MODEL
    return pl.pallas_call(
        _mix_out_cross_kernel,
        out_shape=jax.ShapeDtypeStruct((t, D_MODEL), F32),
        grid=(t // ROW_TILE,),
        in_specs=[row, row, row, row, row, _resident((1, SSD_WIDTH)),
                  _resident((1, SSD_WIDTH)), _resident((SSD_WIDTH, D_MODEL)),
                  _resident((DA_WIDTH, D_MODEL)), _resident((1, D_MODEL)),
                  _resident((D_MODEL, D_MODEL)), mem, mem,
                  _resident((D_MODEL, D_MODEL))],
        out_specs=row,
        compiler_params=_params("parallel"),
        name="mix_out_cross",
    )(x1, y_scan, xbc, z, y_attn, dskip, snw, w_s, w_a, nw, wq, k, v, wo)


def kernel(x, mem, norm_ffn1_w, ffn1_w_in, ffn1_w_out, norm_mix_w, w_in_mix, conv_w, conv_b, dt_bias, a_log, d_skip, ssd_norm_w, lambda_q1, lambda_k1, lambda_q2, lambda_k2, subln_w, rel_bias, w_out_mix, norm_cross_w, norm_mem_w, w_cq, w_ck, w_cv, w_co, norm_ffn2_w, ffn2_w_in, ffn2_w_out, norm_final_w):
    batch, seq, _ = x.shape
    depth = ffn1_w_in.shape[0]
    t = batch * seq
    xf = x.reshape(t, D_MODEL)
    memf = mem.reshape(batch * MEM_LEN, D_MODEL)
    fw = norm_final_w.reshape(1, D_MODEL)
    tiles = _rel_bias_tiles(rel_bias)

    def vec(p):
        return p.reshape(1, -1)

    def pad_lanes(p):
        return jnp.pad(p.reshape(1, -1), ((0, 0), (0, DT_PAD - p.shape[-1])))

    for l in range(depth):
        lambda_init = 0.8 - 0.6 * math.exp(-0.3 * l)
        xf = _ffn(xf, vec(norm_ffn1_w[l]), ffn1_w_in[l].astype(BF16),
                  ffn1_w_out[l].astype(BF16), fw, final=False, name="ffn1")

        wm = w_in_mix[l]
        o_xbc = SSD_WIDTH
        o_dt = o_xbc + CONV_CH
        o_q = o_dt + SSD_HEADS
        o_k = o_q + DA_WIDTH
        o_v = o_k + DA_WIDTH

        def cols(lo, hi):
            return wm[:, lo:hi].astype(BF16)

        wdt = jnp.pad(cols(o_dt, o_q), ((0, 0), (0, DT_PAD - SSD_HEADS)))
        z, xbc, cum, src, q, k, v = _in_proj(
            xf, vec(norm_mix_w[l]), cols(0, o_xbc), cols(o_xbc, o_dt), wdt,
            cols(o_q, o_k), cols(o_k, o_v), cols(o_v, o_v + DA_WIDTH),
            conv_w[l], vec(conv_b[l]), pad_lanes(dt_bias[l]), pad_lanes(a_log[l]),
            seq=seq)

        y_scan = _ssd(xbc, cum, src, batch=batch, seq=seq)

        lam_p = jnp.stack([lambda_q1[l], lambda_k1[l], lambda_q2[l], lambda_k2[l]])
        sw = jnp.broadcast_to(subln_w[l].reshape(DA_V_DIM, 1), (DA_V_DIM, ATTN_BLOCK))
        y_attn = _attn(q, k, v, tiles, lam_p, sw,
                       batch=batch, seq=seq, lambda_init=lambda_init)

        mk, mv = _mem_kv(memf, vec(norm_mem_w[l]), w_ck[l].astype(BF16),
                         w_cv[l].astype(BF16))
        xf = _mix_out_cross(xf, y_scan, xbc, z, y_attn,
                            vec(jnp.repeat(d_skip[l], SSD_HEAD_DIM)),
                            vec(ssd_norm_w[l]),
                            w_out_mix[l, :SSD_WIDTH].astype(BF16),
                            w_out_mix[l, SSD_WIDTH:].astype(BF16),
                            vec(norm_cross_w[l]), w_cq[l].astype(BF16), mk, mv,
                            w_co[l].astype(BF16), seq=seq)

        xf = _ffn(xf, vec(norm_ffn2_w[l]), ffn2_w_in[l].astype(BF16),
                  ffn2_w_out[l].astype(BF16), fw, final=(l == depth - 1),
                  name="ffn2")
    return xf.reshape(batch, seq, D_MODEL)
```

```python
import functools
import math

import jax
import jax.numpy as jnp
from jax import lax
from jax.experimental import pallas as pl
from jax.experimental.pallas import tpu as pltpu

F32 = jnp.float32
BF16 = jnp.bfloat16

D_MODEL = 1024
MEM_LEN = 256
EPS = 1e-6

SSD_HEADS = 16
SSD_HEAD_DIM = 64
SSD_WIDTH = SSD_HEADS * SSD_HEAD_DIM
SSD_GROUPS = 4
SSD_STATE = 128
CONV_WIDTH = 4
SSD_CHUNK = 128
CONV_CH = SSD_WIDTH + 2 * SSD_GROUPS * SSD_STATE
HEADS_PER_GROUP = SSD_HEADS // SSD_GROUPS

DA_HEADS = 8
DA_HEAD_DIM = 64
DA_V_DIM = 2 * DA_HEAD_DIM
DA_WIDTH = DA_HEADS * DA_V_DIM

NUM_BUCKETS = 32
MAX_DISTANCE = 128

CROSS_HEADS = 4
CROSS_HEAD_DIM = D_MODEL // CROSS_HEADS

D_FF = 2816
LOG2_E = math.log2(math.e)

LANES = 128
SUBLANES = 8
MXU_DIM = 256
VMEM_LIMIT_BYTES = 56 * 1024 * 1024

ROW_TILE = 512
FF_CHUNK = MXU_DIM
ATTN_BLOCK = 512
ATTN_ONES_ROWS = 16
BIAS_TILE = MAX_DISTANCE
DT_PAD = LANES
CONV_HALO = SUBLANES
SSD_CHUNKS_PER_STEP = 2


def _resident(shape):
    nd = len(shape)
    return pl.BlockSpec(shape, lambda *_: (0,) * nd, pipeline_mode=pl.Buffered(1))


def _params(*sem):
    return pltpu.CompilerParams(dimension_semantics=sem,
                                vmem_limit_bytes=VMEM_LIMIT_BYTES)


def _rms(x, w):
    ms = jnp.mean(x * x, axis=-1, keepdims=True)
    return x * lax.rsqrt(ms + EPS) * w


def _silu(x):
    return x / (1.0 + jnp.exp(-x))


def _dot(a, b):
    return jnp.dot(a, b, preferred_element_type=F32)


def _dot_nt(a, b):
    return lax.dot_general(a, b, (((1,), (1,)), ((), ())),
                           preferred_element_type=F32)


def _ffn_kernel(x_ref, nw_ref, win_ref, wout_ref, fw_ref, o_ref, *, final):
    x = x_ref[...]
    h = _rms(x, nw_ref[...])
    acc = None
    for c in range(D_FF // FF_CHUNK):
        lo = c * FF_CHUNK
        g = _dot(h, win_ref[:, lo:lo + FF_CHUNK])
        u = _dot(h, win_ref[:, D_FF + lo:D_FF + lo + FF_CHUNK])
        a = _silu(g) * u
        d = _dot(a, wout_ref[lo:lo + FF_CHUNK, :])
        acc = d if acc is None else acc + d
    y = x + 0.5 * acc
    if final:
        y = _rms(y, fw_ref[...])
    o_ref[...] = y


def _ffn(x, nw, win, wout, fw, *, final, name):
    t = x.shape[0]
    row = pl.BlockSpec((ROW_TILE, D_MODEL), lambda i: (i, 0))
    return pl.pallas_call(
        functools.partial(_ffn_kernel, final=final),
        out_shape=jax.ShapeDtypeStruct((t, D_MODEL), F32),
        grid=(t // ROW_TILE,),
        in_specs=[row, _resident((1, D_MODEL)), _resident((D_MODEL, 2 * D_FF)),
                  _resident((D_FF, D_MODEL)), _resident((1, D_MODEL))],
        out_specs=row,
        compiler_params=_params("parallel"),
        name=name,
    )(x, nw, win, wout, fw)


def _in_proj_kernel(x_ref, nw_ref, wzx_ref, wdt_ref, wq_ref, wk_ref, wv_ref,
                    cw_ref, cb_ref, dtb_ref, alog_ref,
                    z_ref, xbc_ref, cum_ref, src_ref, q_ref, k_ref, v_ref, ext_ref,
                    *, tiles_per_seq):
    i = pl.program_id(0)
    tm = ROW_TILE
    h = _rms(x_ref[...], nw_ref[...]).astype(BF16)

    @pl.when(i % tiles_per_seq == 0)
    def _():
        ext_ref[:, 0:CONV_HALO, :] = jnp.zeros((CONV_CH // LANES, CONV_HALO, LANES), F32)

    x_dt = _dot(h, wdt_ref[...]) + dtb_ref[...]
    dt = jnp.maximum(x_dt, 0.0) + jnp.log1p(jnp.exp(-jnp.abs(x_dt)))
    log2_dt = jnp.log2(dt)
    la = dt * (-LOG2_E * jnp.exp(alog_ref[...]))
    la_hi = la.astype(BF16)
    la_mid = (la - la_hi.astype(F32)).astype(BF16)
    la_lo = (la - la_hi.astype(F32) - la_mid.astype(F32)).astype(BF16)
    row_i = lax.broadcasted_iota(jnp.int32, (SSD_CHUNK, SSD_CHUNK), 0)
    col_i = lax.broadcasted_iota(jnp.int32, (SSD_CHUNK, SSD_CHUNK), 1)
    tril = (col_i <= row_i).astype(BF16)

    wide = ((z_ref, wzx_ref.at[:, 0:SSD_WIDTH], 1.0),
            (q_ref, wq_ref, DA_HEAD_DIM ** -0.5 * LOG2_E),
            (k_ref, wk_ref, 1.0), (v_ref, wv_ref, 1.0))
    n_chunks = tm // SSD_CHUNK
    assert n_chunks == len(wide)
    u = _dot(h, wzx_ref[:, SSD_WIDTH:SSD_WIDTH + CONV_CH])
    for t in range(CONV_CH // LANES):
        ext_ref[t, CONV_HALO:CONV_HALO + tm, :] = u[:, t * LANES:(t + 1) * LANES]
    for r in range(n_chunks):
        lo = r * SSD_CHUNK
        rows = slice(lo, lo + SSD_CHUNK)
        cum = (_dot(tril, la_hi[rows]) + _dot(tril, la_mid[rows])
               + _dot(tril, la_lo[rows]))
        cum_ref[rows, :] = cum
        src_ref[rows, :] = cum - log2_dt[rows]
        base = CONV_HALO + lo
        acts = []
        for t in range(CONV_CH // LANES):
            ln = slice(t * LANES, (t + 1) * LANES)
            acc = cb_ref[:, ln]
            for j in range(CONV_WIDTH):
                start = base - (CONV_WIDTH - 1) + j
                acc = acc + cw_ref[j:j + 1, ln] * ext_ref[t, start:start + SSD_CHUNK, :]
            acts.append(_silu(acc))
        xbc_ref[rows, :] = jnp.concatenate(acts, axis=-1).astype(BF16)
        o_ref, w_ref, scale = wide[r]
        out = _dot(h, w_ref[...])
        o_ref[...] = (out if scale == 1.0 else out * scale).astype(BF16)
    ext_ref[:, 0:CONV_HALO, :] = ext_ref[:, tm:tm + CONV_HALO, :]


def _in_proj(x1, nw, w_all, wdt, wq, wk, wv, cw, cb, dtb, alog, *, seq):
    t = x1.shape[0]

    def row(n):
        return pl.BlockSpec((ROW_TILE, n), lambda i: (i, 0))

    return pl.pallas_call(
        functools.partial(_in_proj_kernel, tiles_per_seq=seq // ROW_TILE),
        out_shape=(jax.ShapeDtypeStruct((t, SSD_WIDTH), BF16),
                   jax.ShapeDtypeStruct((t, CONV_CH), BF16),
                   jax.ShapeDtypeStruct((t, DT_PAD), F32),
                   jax.ShapeDtypeStruct((t, DT_PAD), F32),
                   jax.ShapeDtypeStruct((t, DA_WIDTH), BF16),
                   jax.ShapeDtypeStruct((t, DA_WIDTH), BF16),
                   jax.ShapeDtypeStruct((t, DA_WIDTH), BF16)),
        grid=(t // ROW_TILE,),
        in_specs=[row(D_MODEL), _resident((1, D_MODEL)),
                  _resident((D_MODEL, SSD_WIDTH + CONV_CH)),
                  _resident((D_MODEL, DT_PAD)), _resident((D_MODEL, DA_WIDTH)),
                  _resident((D_MODEL, DA_WIDTH)), _resident((D_MODEL, DA_WIDTH)),
                  _resident((CONV_WIDTH, CONV_CH)), _resident((1, CONV_CH)),
                  _resident((1, DT_PAD)), _resident((1, DT_PAD))],
        out_specs=(row(SSD_WIDTH), row(CONV_CH), row(DT_PAD), row(DT_PAD),
                   row(DA_WIDTH), row(DA_WIDTH), row(DA_WIDTH)),
        scratch_shapes=[pltpu.VMEM((CONV_CH // LANES, CONV_HALO + ROW_TILE + CONV_HALO,
                                    LANES), F32)],
        compiler_params=_params("arbitrary"),
        name="in_proj",
    )(x1, nw, w_all, wdt, wq, wk, wv, cw, cb, dtb, alog)


def _ssd_chunk(xbc, cum_col, src_col, state_ref):
    L = SSD_CHUNK
    hp = SSD_HEAD_DIM
    xs_b = xbc[:, :SSD_WIDTH]
    row_i = lax.broadcasted_iota(jnp.int32, (L, L), 0)
    col_i = lax.broadcasted_iota(jnp.int32, (L, L), 1)
    causal = col_i <= row_i
    first = col_i < hp
    keep = (first.astype(BF16), 1.0 - first.astype(BF16))
    src_row = src_col.T

    ys = []
    for g in range(SSD_GROUPS):
        b_g = xbc[:, SSD_WIDTH + g * SSD_STATE:SSD_WIDTH + (g + 1) * SSD_STATE]
        c_g = xbc[:, SSD_WIDTH + (SSD_GROUPS + g) * SSD_STATE:
                  SSD_WIDTH + (SSD_GROUPS + g + 1) * SSD_STATE]
        cb = _dot_nt(c_g, b_g)
        bt_g = b_g.astype(F32).T
        for pr in range(HEADS_PER_GROUP // 2):
            pair = g * (HEADS_PER_GROUP // 2) + pr
            x_pair = xs_b[:, pair * 2 * hp:(pair + 1) * 2 * hp]
            x_bd = jnp.concatenate([x_pair * keep[0], x_pair * keep[1]], axis=0)
            cols, w_parts, b_parts = [], [], []
            for t in range(2):
                h = 2 * pair + t
                col_b = jnp.broadcast_to(cum_col[:, h:h + 1], (L, L))
                row_b = jnp.broadcast_to(src_row[h:h + 1, :], (L, L))
                dec = jnp.exp2(jnp.where(causal, col_b - row_b, -jnp.inf))
                w_parts.append((cb * dec).astype(BF16))
                b_parts.append((bt_g * dec[L - 1:L, :]).astype(BF16))
                cols.append(col_b)
            from_start = jnp.exp2(jnp.where(first, cols[0], cols[1]))
            s_prev = state_ref[pair]
            ys.append(_dot(jnp.concatenate(w_parts, axis=1), x_bd)
                      + _dot(c_g, s_prev.astype(BF16)) * from_start)
            new = _dot(jnp.concatenate(b_parts, axis=1), x_bd)
            state_ref[pair] = s_prev * from_start[L - 1:L, :] + new
    return jnp.concatenate(ys, axis=-1).astype(BF16)


def _ssd_kernel(xbc_ref, cum_ref, src_ref, o_ref, state_ref):
    @pl.when(pl.program_id(1) == 0)
    def _():
        state_ref[...] = jnp.zeros_like(state_ref)

    for r in range(SSD_CHUNKS_PER_STEP):
        rows = slice(r * SSD_CHUNK, (r + 1) * SSD_CHUNK)
        o_ref[rows, :] = _ssd_chunk(xbc_ref[rows, :], cum_ref[rows, :], src_ref[rows, :],
                                    state_ref)


def _ssd_gate_norm(y, xs, z, dskip, nw):
    y = y.astype(F32) + dskip * xs.astype(F32)
    y = y * _silu(z.astype(F32))
    gw = SSD_WIDTH // SSD_GROUPS
    outs = []
    for g in range(SSD_GROUPS):
        yg = y[:, g * gw:(g + 1) * gw]
        ms = jnp.mean(yg * yg, axis=-1, keepdims=True)
        outs.append(yg * lax.rsqrt(ms + EPS))
    return jnp.concatenate(outs, axis=-1) * nw


def _ssd(xbc, cum, src, *, batch, seq):
    step_rows = SSD_CHUNKS_PER_STEP * SSD_CHUNK
    ns = seq // step_rows

    def row(n):
        return pl.BlockSpec((step_rows, n), lambda b, c: (b * ns + c, 0))

    return pl.pallas_call(
        _ssd_kernel,
        out_shape=jax.ShapeDtypeStruct((batch * seq, SSD_WIDTH), BF16),
        grid=(batch, ns),
        in_specs=[row(CONV_CH), row(DT_PAD), row(DT_PAD)],
        out_specs=row(SSD_WIDTH),
        scratch_shapes=[pltpu.VMEM((SSD_HEADS // 2, SSD_STATE, 2 * SSD_HEAD_DIM), F32)],
        compiler_params=_params("parallel", "arbitrary"),
        name="ssd",
    )(xbc, cum, src)


def _rel_bias_kernel(rb_ref, o_ref):
    k_i = lax.broadcasted_iota(jnp.int32, (BIAS_TILE, BIAS_TILE), 0)
    q_i = lax.broadcasted_iota(jnp.int32, (BIAS_TILE, BIAS_TILE), 1)
    max_exact = NUM_BUCKETS // 2
    for d in range(2):
        n = q_i - k_i + d * BIAS_TILE
        nf = jnp.maximum(n, 1).astype(F32)
        large = max_exact + (jnp.log(nf / max_exact) / math.log(MAX_DISTANCE / max_exact)
                             * (NUM_BUCKETS - max_exact)).astype(jnp.int32)
        large = jnp.minimum(large, NUM_BUCKETS - 1)
        bucket = jnp.where(n < max_exact, n, large)
        hits = [bucket == b for b in range(NUM_BUCKETS)]
        for h in range(DA_HEADS):
            tile = jnp.zeros((BIAS_TILE, BIAS_TILE), F32)
            for b in range(NUM_BUCKETS):
                tile = jnp.where(hits[b], rb_ref[b, h], tile)
            tile = (tile - rb_ref[NUM_BUCKETS - 1, h]) * LOG2_E
            o_ref[h, d] = jnp.where(n >= 0, tile, -jnp.inf)


def _rel_bias_tiles(rel_bias):
    return pl.pallas_call(
        _rel_bias_kernel,
        out_shape=jax.ShapeDtypeStruct((DA_HEADS, 2, BIAS_TILE, BIAS_TILE), F32),
        in_specs=[pl.BlockSpec(memory_space=pltpu.SMEM)],
        out_specs=pl.BlockSpec(memory_space=pltpu.VMEM),
        compiler_params=_params(),
        name="rel_bias",
    )(rel_bias)


def _attn_kernel(q_ref, k_ref, v_ref, bt_ref, lam_ref, sw_ref, o_ref,
                 vt_ref, qz_ref, s0_ref, s1_ref, m_ref, acc_ref, *, lambda_init, seq):
    tb = ATTN_BLOCK
    dh = DA_HEAD_DIM
    dv = DA_V_DIM
    nb = seq // tb

    for j in range(nb):
        vt = v_ref[j * tb:(j + 1) * tb, :].astype(F32).T
        vt_ref[j, 0:dv, :] = vt.astype(BF16)
        vt_ref[j, dv:dv + ATTN_ONES_ROWS, :] = jnp.ones((ATTN_ONES_ROWS, tb), BF16)

    for i in range(nb):
        qt = q_ref[i * tb:(i + 1) * tb, :].astype(F32).T
        feat = lax.broadcasted_iota(jnp.int32, qt.shape, 0)
        for c in range(2):
            in_map = (feat >= c * dh) & (feat < (c + 1) * dh)
            qz_ref[i, c] = jnp.where(in_map, qt, 0.0).astype(BF16)

    s_refs = (s0_ref, s1_ref)
    half = tb // 2
    bt = BIAS_TILE
    nsub = tb // bt
    full = ((0, tb, 0, tb),)
    diag_parts = ((0, half, 0, half), (0, tb, half, tb))

    def scores(c, step):
        i, j, kind = step
        k0 = j * tb
        if kind != "diag":
            s_refs[c][...] = _dot(k_ref[k0:k0 + tb, :], qz_ref[i, c])
        else:
            s_refs[c][0:half, :] = _dot(k_ref[k0:k0 + half, :], qz_ref[i, c])
            s_refs[c][half:tb, half:tb] = _dot(k_ref[k0 + half:k0 + tb, :],
                                               qz_ref[i, c, :, half:tb])

    def add_bias(c, kind):
        s_ref = s_refs[c]
        if kind == "near":
            s_ref[tb - bt:tb, 0:bt] = s_ref[tb - bt:tb, 0:bt] + bt_ref[1]
        elif kind == "diag":
            for a in range(nsub):
                ks = slice(a * bt, (a + 1) * bt)
                s_ref[ks, ks] = s_ref[ks, ks] + bt_ref[0]
                if a + 1 < nsub:
                    qs = slice((a + 1) * bt, (a + 2) * bt)
                    s_ref[ks, qs] = s_ref[ks, qs] + bt_ref[1]
            for k0, k1, q0, q1 in diag_parts:
                for a in range(k0 // bt, k1 // bt):
                    for b in range(q0 // bt, q1 // bt):
                        if a > b:
                            s_ref[a * bt:(a + 1) * bt, b * bt:(b + 1) * bt] = (
                                jnp.full((bt, bt), -jnp.inf, F32))

    def softmax_pv(c, step):
        i, j, kind = step
        for k0, k1, q0, q1 in (diag_parts if kind == "diag" else full):
            s = s_refs[c][k0:k1, q0:q1]
            m_blk = jnp.max(s, axis=0, keepdims=True)
            pv_args = (vt_ref[j, :, k0:k1],)
            if j == 0:
                m_new = m_blk
                acc_ref[i, c, :, q0:q1] = _dot(*pv_args,
                                               jnp.exp2(s - m_new).astype(BF16))
            else:
                m_old = m_ref[i, c, :, q0:q1]
                m_new = jnp.maximum(m_old, m_blk)
                p = jnp.exp2(s - m_new).astype(BF16)
                acc_ref[i, c, :, q0:q1] = (acc_ref[i, c, :, q0:q1]
                                           * jnp.exp2(m_old - m_new) + _dot(*pv_args, p))
            m_ref[i, c, :, q0:q1] = m_new

    steps = []
    for j in range(nb):
        for i in range(j, nb):
            steps.append((i, j, "diag" if i == j else "near" if i == j + 1 else "far"))

    scores(0, steps[0])
    for t, step in enumerate(steps):
        scores(1, step)
        add_bias(0, step[2])
        softmax_pv(0, step)
        if t + 1 < len(steps):
            scores(0, steps[t + 1])
        add_bias(1, step[2])
        softmax_pv(1, step)

    lp = lam_ref[...]
    lam = (jnp.exp(jnp.sum(lp[0:1] * lp[1:2], axis=-1, keepdims=True))
           - jnp.exp(jnp.sum(lp[2:3] * lp[3:4], axis=-1, keepdims=True))
           + lambda_init)
    for i in range(nb):
        a0 = acc_ref[i, 0]
        a1 = acc_ref[i, 1]
        o = a0[0:dv] / a0[dv:dv + 1] - lam * (a1[0:dv] / a1[dv:dv + 1])
        o = o * lax.rsqrt(jnp.mean(o * o, axis=0, keepdims=True) + EPS)
        o = o * sw_ref[...] * (1.0 - lambda_init)
        o_ref[i * tb:(i + 1) * tb, :] = o.T.astype(BF16)


def _attn(q, k, v, tiles, lam_p, sw, *, batch, seq, lambda_init):
    nb = seq // ATTN_BLOCK
    seq_head = pl.BlockSpec((seq, DA_V_DIM), lambda b, h: (b, h))
    acc_rows = DA_V_DIM + ATTN_ONES_ROWS
    return pl.pallas_call(
        functools.partial(_attn_kernel, lambda_init=lambda_init, seq=seq),
        out_shape=jax.ShapeDtypeStruct((batch * seq, DA_WIDTH), BF16),
        grid=(batch, DA_HEADS),
        in_specs=[seq_head, seq_head, seq_head,
                  pl.BlockSpec((None, 2, BIAS_TILE, BIAS_TILE),
                               lambda b, h: (h, 0, 0, 0)),
                  _resident((4, DA_HEAD_DIM)), _resident((DA_V_DIM, ATTN_BLOCK))],
        out_specs=seq_head,
        scratch_shapes=[pltpu.VMEM((nb, acc_rows, ATTN_BLOCK), BF16),
                        pltpu.VMEM((nb, 2, DA_V_DIM, ATTN_BLOCK), BF16),
                        pltpu.VMEM((ATTN_BLOCK, ATTN_BLOCK), F32),
                        pltpu.VMEM((ATTN_BLOCK, ATTN_BLOCK), F32),
                        pltpu.VMEM((nb, 2, 1, ATTN_BLOCK), F32),
                        pltpu.VMEM((nb, 2, acc_rows, ATTN_BLOCK), F32)],
        compiler_params=_params("parallel", "parallel"),
        name="diff_attn",
    )(q, k, v, tiles, lam_p, sw)


def _mem_kv_kernel(mem_ref, nw_ref, wk_ref, wv_ref, k_ref, v_ref):
    mn = _rms(mem_ref[...], nw_ref[...])
    k_ref[...] = _dot(mn, wk_ref[...]).astype(BF16)
    v_ref[...] = _dot(mn, wv_ref[...]).astype(BF16)


def _mem_kv(mem, nw, wk, wv):
    t = mem.shape[0]
    row = pl.BlockSpec((MEM_LEN, D_MODEL), lambda i: (i, 0))
    return pl.pallas_call(
        _mem_kv_kernel,
        out_shape=(jax.ShapeDtypeStruct((t, D_MODEL), BF16),
                   jax.ShapeDtypeStruct((t, D_MODEL), BF16)),
        grid=(t // MEM_LEN,),
        in_specs=[row, _resident((1, D_MODEL)), _resident((D_MODEL, D_MODEL)),
                  _resident((D_MODEL, D_MODEL))],
        out_specs=(row, row),
        compiler_params=_params("parallel"),
        name="mem_kv",
    )(mem, nw, wk, wv)


def _mix_out_cross_kernel(x_ref, ys_ref, xs_ref, z_ref, ya_ref, dskip_ref, snw_ref,
                          ws_ref, wa_ref, nw_ref, wq_ref, k_ref, v_ref, wo_ref, o_ref):
    y_ssd = _ssd_gate_norm(ys_ref[...], xs_ref[...], z_ref[...], dskip_ref[...],
                           snw_ref[...])
    x = (x_ref[...] + _dot(ya_ref[...].astype(F32), wa_ref[...])
         + _dot(y_ssd, ws_ref[...]))
    h = _rms(x, nw_ref[...])
    q = (_dot(h, wq_ref[...]) * (CROSS_HEAD_DIM ** -0.5)).astype(BF16)
    outs = []
    for hh in range(CROSS_HEADS):
        sl = slice(hh * CROSS_HEAD_DIM, (hh + 1) * CROSS_HEAD_DIM)
        s = _dot_nt(q[:, sl], k_ref[:, sl])
        p = jnp.exp(s - jnp.max(s, axis=-1, keepdims=True))
        o = _dot(p.astype(BF16), v_ref[:, sl])
        outs.append(o / jnp.sum(p, axis=-1, keepdims=True))
    o_ref[...] = x + _dot(jnp.concatenate(outs, axis=-1), wo_ref[...])


def _mix_out_cross(x1, y_scan, xbc, z, y_attn, dskip, snw, w_out, nw, wq, k, v, wo,
                   *, seq):
    t = x1.shape[0]
    tiles_per_batch = seq // ROW_TILE
    row = pl.BlockSpec((ROW_TILE, D_MODEL), lambda i: (i, 0))
    mem = pl.BlockSpec((MEM_LEN, D_MODEL), lambda i: (i // tiles_per_batch, 0))
    assert SSD_WIDTH == D_MODEL
    assert SSD_WIDTH == DA_WIDTH

    def w_half(j):
        return pl.BlockSpec((SSD_WIDTH, D_MODEL), lambda i: (j, 0),
                            pipeline_mode=pl.Buffered(1))

    return pl.pallas_call(
        _mix_out_cross_kernel,
        out_shape=jax.ShapeDtypeStruct((t, D_MODEL), F32),
        grid=(t // ROW_TILE,),
        in_specs=[row, row, row, row, row, _resident((1, SSD_WIDTH)),
                  _resident((1, SSD_WIDTH)), w_half(0), w_half(1),
                  _resident((1, D_MODEL)), _resident((D_MODEL, D_MODEL)), mem, mem,
                  _resident((D_MODEL, D_MODEL))],
        out_specs=row,
        compiler_params=_params("parallel"),
        name="mix_out_cross",
    )(x1, y_scan, xbc, z, y_attn, dskip, snw, w_out, w_out, nw, wq, k, v, wo)


def kernel(x, mem, norm_ffn1_w, ffn1_w_in, ffn1_w_out, norm_mix_w, w_in_mix, conv_w, conv_b, dt_bias, a_log, d_skip, ssd_norm_w, lambda_q1, lambda_k1, lambda_q2, lambda_k2, subln_w, rel_bias, w_out_mix, norm_cross_w, norm_mem_w, w_cq, w_ck, w_cv, w_co, norm_ffn2_w, ffn2_w_in, ffn2_w_out, norm_final_w):
    batch, seq, _ = x.shape
    depth = ffn1_w_in.shape[0]
    t = batch * seq
    xf = x.reshape(t, D_MODEL)
    memf = mem.reshape(batch * MEM_LEN, D_MODEL)
    fw = norm_final_w.reshape(1, D_MODEL)
    tiles = _rel_bias_tiles(rel_bias)

    def vec(p):
        return p.reshape(1, -1)

    def pad_lanes(p):
        return jnp.pad(p.reshape(1, -1), ((0, 0), (0, DT_PAD - p.shape[-1])))

    for l in range(depth):
        lambda_init = 0.8 - 0.6 * math.exp(-0.3 * l)
        xf = _ffn(xf, vec(norm_ffn1_w[l]), ffn1_w_in[l], ffn1_w_out[l], fw,
                  final=False, name="ffn1")

        wm = w_in_mix[l].astype(BF16)
        o_dt = SSD_WIDTH + CONV_CH
        o_q = o_dt + SSD_HEADS
        o_k = o_q + DA_WIDTH
        o_v = o_k + DA_WIDTH
        wdt = jnp.pad(wm[:, o_dt:o_q], ((0, 0), (0, DT_PAD - SSD_HEADS)))
        z, xbc, cum, src, q, k, v = _in_proj(
            xf, vec(norm_mix_w[l]), wm, wdt,
            wm[:, o_q:o_k], wm[:, o_k:o_v], wm[:, o_v:o_v + DA_WIDTH],
            conv_w[l], vec(conv_b[l]), pad_lanes(dt_bias[l]), pad_lanes(a_log[l]),
            seq=seq)

        y_scan = _ssd(xbc, cum, src, batch=batch, seq=seq)

        lam_p = jnp.stack([lambda_q1[l], lambda_k1[l], lambda_q2[l], lambda_k2[l]])
        sw = jnp.broadcast_to(subln_w[l].reshape(DA_V_DIM, 1), (DA_V_DIM, ATTN_BLOCK))
        y_attn = _attn(q, k, v, tiles, lam_p, sw,
                       batch=batch, seq=seq, lambda_init=lambda_init)

        mk, mv = _mem_kv(memf, vec(norm_mem_w[l]), w_ck[l], w_cv[l])
        xf = _mix_out_cross(xf, y_scan, xbc, z, y_attn,
                            vec(jnp.repeat(d_skip[l], SSD_HEAD_DIM)),
                            vec(ssd_norm_w[l]), w_out_mix[l],
                            vec(norm_cross_w[l]), w_cq[l], mk, mv, w_co[l], seq=seq)

        xf = _ffn(xf, vec(norm_ffn2_w[l]), ffn2_w_in[l], ffn2_w_out[l], fw,
                  final=(l == depth - 1), name="ffn2")
    return xf.reshape(batch, seq, D_MODEL)
```

```python
import functools
import math

import jax
import jax.numpy as jnp
from jax import lax
from jax.experimental import pallas as pl
from jax.experimental.pallas import tpu as pltpu

F32 = jnp.float32
BF16 = jnp.bfloat16

D_MODEL = 1024
MEM_LEN = 256
EPS = 1e-6

SSD_HEADS = 16
SSD_HEAD_DIM = 64
SSD_WIDTH = SSD_HEADS * SSD_HEAD_DIM
SSD_GROUPS = 4
SSD_STATE = 128
CONV_WIDTH = 4
SSD_CHUNK = 128
CONV_CH = SSD_WIDTH + 2 * SSD_GROUPS * SSD_STATE
HEADS_PER_GROUP = SSD_HEADS // SSD_GROUPS

DA_HEADS = 8
DA_HEAD_DIM = 64
DA_V_DIM = 2 * DA_HEAD_DIM
DA_WIDTH = DA_HEADS * DA_V_DIM

NUM_BUCKETS = 32
MAX_DISTANCE = 128

CROSS_HEADS = 4
CROSS_HEAD_DIM = D_MODEL // CROSS_HEADS

D_FF = 2816
LOG2_E = math.log2(math.e)

LANES = 128
SUBLANES = 8
MXU_DIM = 256
VMEM_LIMIT_BYTES = 56 * 1024 * 1024

ROW_TILE = 512
FF_CHUNK = MXU_DIM
ATTN_BLOCK = 512
ATTN_ONES_ROWS = 16
BIAS_TILE = MAX_DISTANCE
DT_PAD = LANES
CONV_HALO = SUBLANES
SSD_CHUNKS_PER_STEP = 2


def _resident(shape):
    nd = len(shape)
    return pl.BlockSpec(shape, lambda *_: (0,) * nd, pipeline_mode=pl.Buffered(1))


def _params(*sem):
    return pltpu.CompilerParams(dimension_semantics=sem,
                                vmem_limit_bytes=VMEM_LIMIT_BYTES)


def _rms(x, w):
    ms = jnp.mean(x * x, axis=-1, keepdims=True)
    return x * lax.rsqrt(ms + EPS) * w


def _silu(x):
    h = 0.5 * x
    return h + h * jnp.tanh(h)


def _dot(a, b):
    return jnp.dot(a, b, preferred_element_type=F32)


def _dot_nt(a, b):
    return lax.dot_general(a, b, (((1,), (1,)), ((), ())),
                           preferred_element_type=F32)


def _ffn_kernel(x_ref, nw_ref, win_ref, wout_ref, fw_ref, o_ref, *, final):
    x = x_ref[...]
    h = _rms(x, nw_ref[...])
    acc = None
    for c in range(D_FF // FF_CHUNK):
        lo = c * FF_CHUNK
        g = _dot(h, win_ref[:, lo:lo + FF_CHUNK])
        u = _dot(h, win_ref[:, D_FF + lo:D_FF + lo + FF_CHUNK])
        a = _silu(g) * u
        d = _dot(a, wout_ref[lo:lo + FF_CHUNK, :])
        acc = d if acc is None else acc + d
    y = x + 0.5 * acc
    if final:
        y = _rms(y, fw_ref[...])
    o_ref[...] = y


def _ffn(x, nw, win, wout, fw, *, final, name):
    t = x.shape[0]
    row = pl.BlockSpec((ROW_TILE, D_MODEL), lambda i: (i, 0))
    return pl.pallas_call(
        functools.partial(_ffn_kernel, final=final),
        out_shape=jax.ShapeDtypeStruct((t, D_MODEL), F32),
        grid=(t // ROW_TILE,),
        in_specs=[row, _resident((1, D_MODEL)), _resident((D_MODEL, 2 * D_FF)),
                  _resident((D_FF, D_MODEL)), _resident((1, D_MODEL))],
        out_specs=row,
        compiler_params=_params("parallel"),
        name=name,
    )(x, nw, win, wout, fw)


def _in_proj_kernel(x_ref, nw_ref, wzx_ref, wdt_ref, wq_ref, wk_ref, wv_ref,
                    cw_ref, cb_ref, dtb_ref, alog_ref,
                    z_ref, xbc_ref, cum_ref, src_ref, q_ref, k_ref, v_ref, ext_ref,
                    *, tiles_per_seq):
    i = pl.program_id(0)
    tm = ROW_TILE
    h = _rms(x_ref[...], nw_ref[...]).astype(BF16)

    @pl.when(i % tiles_per_seq == 0)
    def _():
        ext_ref[:, 0:CONV_HALO, :] = jnp.zeros((CONV_CH // LANES, CONV_HALO, LANES), F32)

    x_dt = _dot(h, wdt_ref[...]) + dtb_ref[...]
    dt = jnp.maximum(x_dt, 0.0) + jnp.log1p(jnp.exp(-jnp.abs(x_dt)))
    log2_dt = jnp.log2(dt)
    la = dt * (-LOG2_E * jnp.exp(alog_ref[...]))
    la_hi = la.astype(BF16)
    la_mid = (la - la_hi.astype(F32)).astype(BF16)
    la_lo = (la - la_hi.astype(F32) - la_mid.astype(F32)).astype(BF16)
    row_i = lax.broadcasted_iota(jnp.int32, (SSD_CHUNK, SSD_CHUNK), 0)
    col_i = lax.broadcasted_iota(jnp.int32, (SSD_CHUNK, SSD_CHUNK), 1)
    tril = (col_i <= row_i).astype(BF16)

    wide = ((z_ref, wzx_ref.at[:, 0:SSD_WIDTH], 1.0),
            (q_ref, wq_ref, DA_HEAD_DIM ** -0.5 * LOG2_E),
            (k_ref, wk_ref, 1.0), (v_ref, wv_ref, 1.0))
    n_chunks = tm // SSD_CHUNK
    assert n_chunks == len(wide)
    u = _dot(h, wzx_ref[:, SSD_WIDTH:SSD_WIDTH + CONV_CH])
    for t in range(CONV_CH // LANES):
        ext_ref[t, CONV_HALO:CONV_HALO + tm, :] = u[:, t * LANES:(t + 1) * LANES]
    for r in range(n_chunks):
        lo = r * SSD_CHUNK
        rows = slice(lo, lo + SSD_CHUNK)
        cum = (_dot(tril, la_hi[rows]) + _dot(tril, la_mid[rows])
               + _dot(tril, la_lo[rows]))
        cum_ref[rows, :] = cum
        src_ref[rows, :] = cum - log2_dt[rows]
        base = CONV_HALO + lo
        acts = []
        for t in range(CONV_CH // LANES):
            ln = slice(t * LANES, (t + 1) * LANES)
            acc = cb_ref[:, ln]
            for j in range(CONV_WIDTH):
                start = base - (CONV_WIDTH - 1) + j
                acc = acc + cw_ref[j:j + 1, ln] * ext_ref[t, start:start + SSD_CHUNK, :]
            acts.append(_silu(acc))
        xbc_ref[rows, :] = jnp.concatenate(acts, axis=-1).astype(BF16)
        o_ref, w_ref, scale = wide[r]
        out = _dot(h, w_ref[...])
        o_ref[...] = (out if scale == 1.0 else out * scale).astype(BF16)
    ext_ref[:, 0:CONV_HALO, :] = ext_ref[:, tm:tm + CONV_HALO, :]


def _in_proj(x1, nw, w_all, wdt, wq, wk, wv, cw, cb, dtb, alog, *, seq):
    t = x1.shape[0]

    def row(n):
        return pl.BlockSpec((ROW_TILE, n), lambda i: (i, 0))

    return pl.pallas_call(
        functools.partial(_in_proj_kernel, tiles_per_seq=seq // ROW_TILE),
        out_shape=(jax.ShapeDtypeStruct((t, SSD_WIDTH), BF16),
                   jax.ShapeDtypeStruct((t, CONV_CH), BF16),
                   jax.ShapeDtypeStruct((t, DT_PAD), F32),
                   jax.ShapeDtypeStruct((t, DT_PAD), F32),
                   jax.ShapeDtypeStruct((t, DA_WIDTH), BF16),
                   jax.ShapeDtypeStruct((t, DA_WIDTH), BF16),
                   jax.ShapeDtypeStruct((t, DA_WIDTH), BF16)),
        grid=(t // ROW_TILE,),
        in_specs=[row(D_MODEL), _resident((1, D_MODEL)),
                  _resident((D_MODEL, SSD_WIDTH + CONV_CH)),
                  _resident((D_MODEL, DT_PAD)), _resident((D_MODEL, DA_WIDTH)),
                  _resident((D_MODEL, DA_WIDTH)), _resident((D_MODEL, DA_WIDTH)),
                  _resident((CONV_WIDTH, CONV_CH)), _resident((1, CONV_CH)),
                  _resident((1, DT_PAD)), _resident((1, DT_PAD))],
        out_specs=(row(SSD_WIDTH), row(CONV_CH), row(DT_PAD), row(DT_PAD),
                   row(DA_WIDTH), row(DA_WIDTH), row(DA_WIDTH)),
        scratch_shapes=[pltpu.VMEM((CONV_CH // LANES, CONV_HALO + ROW_TILE + CONV_HALO,
                                    LANES), F32)],
        compiler_params=_params("arbitrary"),
        name="in_proj",
    )(x1, nw, w_all, wdt, wq, wk, wv, cw, cb, dtb, alog)


def _ssd_chunk(xbc, cum_col, src_col, state_ref):
    L = SSD_CHUNK
    hp = SSD_HEAD_DIM
    xs_b = xbc[:, :SSD_WIDTH]
    row_i = lax.broadcasted_iota(jnp.int32, (L, L), 0)
    col_i = lax.broadcasted_iota(jnp.int32, (L, L), 1)
    causal = col_i <= row_i
    first = col_i < hp
    keep = (first.astype(BF16), 1.0 - first.astype(BF16))
    src_row = src_col.T

    ys = []
    for g in range(SSD_GROUPS):
        b_g = xbc[:, SSD_WIDTH + g * SSD_STATE:SSD_WIDTH + (g + 1) * SSD_STATE]
        c_g = xbc[:, SSD_WIDTH + (SSD_GROUPS + g) * SSD_STATE:
                  SSD_WIDTH + (SSD_GROUPS + g + 1) * SSD_STATE]
        cb = _dot_nt(c_g, b_g)
        bt_g = b_g.astype(F32).T
        for pr in range(HEADS_PER_GROUP // 2):
            pair = g * (HEADS_PER_GROUP // 2) + pr
            x_pair = xs_b[:, pair * 2 * hp:(pair + 1) * 2 * hp]
            x_bd = jnp.concatenate([x_pair * keep[0], x_pair * keep[1]], axis=0)
            cols, w_parts, b_parts = [], [], []
            for t in range(2):
                h = 2 * pair + t
                col_b = jnp.broadcast_to(cum_col[:, h:h + 1], (L, L))
                row_b = jnp.broadcast_to(src_row[h:h + 1, :], (L, L))
                dec = jnp.exp2(jnp.where(causal, col_b - row_b, -jnp.inf))
                w_parts.append((cb * dec).astype(BF16))
                b_parts.append((bt_g * dec[L - 1:L, :]).astype(BF16))
                cols.append(col_b)
            from_start = jnp.exp2(jnp.where(first, cols[0], cols[1]))
            s_prev = state_ref[pair]
            ys.append(_dot(jnp.concatenate(w_parts, axis=1), x_bd)
                      + _dot(c_g, s_prev.astype(BF16)) * from_start)
            new = _dot(jnp.concatenate(b_parts, axis=1), x_bd)
            state_ref[pair] = s_prev * from_start[L - 1:L, :] + new
    return jnp.concatenate(ys, axis=-1).astype(BF16)


def _ssd_kernel(xbc_ref, cum_ref, src_ref, o_ref, state_ref):
    @pl.when(pl.program_id(1) == 0)
    def _():
        state_ref[...] = jnp.zeros_like(state_ref)

    for r in range(SSD_CHUNKS_PER_STEP):
        rows = slice(r * SSD_CHUNK, (r + 1) * SSD_CHUNK)
        o_ref[rows, :] = _ssd_chunk(xbc_ref[rows, :], cum_ref[rows, :], src_ref[rows, :],
                                    state_ref)


def _ssd_gate_norm(y, xs, z, dskip, nw):
    y = y.astype(F32) + dskip * xs.astype(F32)
    y = y * _silu(z.astype(F32))
    gw = SSD_WIDTH // SSD_GROUPS
    outs = []
    for g in range(SSD_GROUPS):
        yg = y[:, g * gw:(g + 1) * gw]
        ms = jnp.mean(yg * yg, axis=-1, keepdims=True)
        outs.append(yg * lax.rsqrt(ms + EPS))
    return jnp.concatenate(outs, axis=-1) * nw


def _ssd(xbc, cum, src, *, batch, seq):
    step_rows = SSD_CHUNKS_PER_STEP * SSD_CHUNK
    ns = seq // step_rows

    def row(n):
        return pl.BlockSpec((step_rows, n), lambda b, c: (b * ns + c, 0))

    return pl.pallas_call(
        _ssd_kernel,
        out_shape=jax.ShapeDtypeStruct((batch * seq, SSD_WIDTH), BF16),
        grid=(batch, ns),
        in_specs=[row(CONV_CH), row(DT_PAD), row(DT_PAD)],
        out_specs=row(SSD_WIDTH),
        scratch_shapes=[pltpu.VMEM((SSD_HEADS // 2, SSD_STATE, 2 * SSD_HEAD_DIM), F32)],
        compiler_params=_params("parallel", "arbitrary"),
        name="ssd",
    )(xbc, cum, src)


def _rel_bias_kernel(rb_ref, o_ref):
    k_i = lax.broadcasted_iota(jnp.int32, (BIAS_TILE, BIAS_TILE), 0)
    q_i = lax.broadcasted_iota(jnp.int32, (BIAS_TILE, BIAS_TILE), 1)
    max_exact = NUM_BUCKETS // 2
    for d in range(2):
        n = q_i - k_i + d * BIAS_TILE
        nf = jnp.maximum(n, 1).astype(F32)
        large = max_exact + (jnp.log(nf / max_exact) / math.log(MAX_DISTANCE / max_exact)
                             * (NUM_BUCKETS - max_exact)).astype(jnp.int32)
        large = jnp.minimum(large, NUM_BUCKETS - 1)
        bucket = jnp.where(n < max_exact, n, large)
        hits = [bucket == b for b in range(NUM_BUCKETS)]
        for h in range(DA_HEADS):
            tile = jnp.zeros((BIAS_TILE, BIAS_TILE), F32)
            for b in range(NUM_BUCKETS):
                tile = jnp.where(hits[b], rb_ref[b, h], tile)
            tile = (tile - rb_ref[NUM_BUCKETS - 1, h]) * LOG2_E
            o_ref[h, d] = jnp.where(n >= 0, tile, -jnp.inf)


def _rel_bias_tiles(rel_bias):
    return pl.pallas_call(
        _rel_bias_kernel,
        out_shape=jax.ShapeDtypeStruct((DA_HEADS, 2, BIAS_TILE, BIAS_TILE), F32),
        in_specs=[pl.BlockSpec(memory_space=pltpu.SMEM)],
        out_specs=pl.BlockSpec(memory_space=pltpu.VMEM),
        compiler_params=_params(),
        name="rel_bias",
    )(rel_bias)


def _attn_kernel(q_ref, k_ref, v_ref, bt_ref, lam_ref, sw_ref, o_ref,
                 vt_ref, qz_ref, s0_ref, s1_ref, m_ref, acc_ref, *, lambda_init, seq):
    tb = ATTN_BLOCK
    dh = DA_HEAD_DIM
    dv = DA_V_DIM
    nb = seq // tb

    for j in range(nb):
        vt = v_ref[j * tb:(j + 1) * tb, :].astype(F32).T
        vt_ref[j, 0:dv, :] = vt.astype(BF16)
        vt_ref[j, dv:dv + ATTN_ONES_ROWS, :] = jnp.ones((ATTN_ONES_ROWS, tb), BF16)

    for i in range(nb):
        qt = q_ref[i * tb:(i + 1) * tb, :].astype(F32).T
        feat = lax.broadcasted_iota(jnp.int32, qt.shape, 0)
        for c in range(2):
            in_map = (feat >= c * dh) & (feat < (c + 1) * dh)
            qz_ref[i, c] = jnp.where(in_map, qt, 0.0).astype(BF16)

    s_refs = (s0_ref, s1_ref)
    half = tb // 2
    bt = BIAS_TILE
    nsub = tb // bt
    full = ((0, tb, 0, tb),)
    diag_parts = ((0, half, 0, half), (0, tb, half, tb))

    zero = jnp.minimum(pl.program_id(0), 0)

    def rows(lo, hi):
        return pl.ds(pl.multiple_of(zero + lo, BIAS_TILE), hi - lo)

    def scores(c, step):
        i, j, kind = step
        k0 = j * tb
        if kind != "diag":
            s_refs[c][rows(0, tb), :] = _dot(k_ref[k0:k0 + tb, :], qz_ref[i, c])
        else:
            s_refs[c][rows(0, half), :] = _dot(k_ref[k0:k0 + half, :], qz_ref[i, c])
            s_refs[c][rows(half, tb), half:tb] = _dot(k_ref[k0 + half:k0 + tb, :],
                                                      qz_ref[i, c, :, half:tb])

    def add_bias(c, kind):
        s_ref = s_refs[c]
        if kind == "near":
            near = (rows(tb - bt, tb), slice(0, bt))
            s_ref[near] = s_ref[near] + bt_ref[1]
        elif kind == "diag":
            for a in range(nsub):
                on = (rows(a * bt, (a + 1) * bt), slice(a * bt, (a + 1) * bt))
                s_ref[on] = s_ref[on] + bt_ref[0]
                if a + 1 < nsub:
                    off = (rows(a * bt, (a + 1) * bt), slice((a + 1) * bt, (a + 2) * bt))
                    s_ref[off] = s_ref[off] + bt_ref[1]
            for k0, k1, q0, q1 in diag_parts:
                for a in range(k0 // bt, k1 // bt):
                    for b in range(q0 // bt, q1 // bt):
                        if a > b:
                            s_ref[rows(a * bt, (a + 1) * bt), b * bt:(b + 1) * bt] = (
                                jnp.full((bt, bt), -jnp.inf, F32))

    def softmax_pv(c, step):
        i, j, kind = step
        for k0, k1, q0, q1 in (diag_parts if kind == "diag" else full):
            s = s_refs[c][rows(k0, k1), q0:q1]
            m_blk = jnp.max(s, axis=0, keepdims=True)
            if j == 0:
                m_new = m_blk
            else:
                m_old = m_ref[i, c, :, q0:q1]
                m_new = jnp.maximum(m_old, m_blk)
            pv = _dot(vt_ref[j, :, k0:k1], jnp.exp2(s - m_new).astype(BF16))
            if j == 0:
                acc_ref[i, c, :, q0:q1] = pv
            else:
                acc_ref[i, c, :, q0:q1] = (acc_ref[i, c, :, q0:q1]
                                           * jnp.exp2(m_old - m_new) + pv)
            m_ref[i, c, :, q0:q1] = m_new

    steps = []
    for j in range(nb):
        for i in range(j, nb):
            steps.append((i, j, "diag" if i == j else "near" if i == j + 1 else "far"))

    scores(0, steps[0])
    for t, step in enumerate(steps):
        scores(1, step)
        add_bias(0, step[2])
        softmax_pv(0, step)
        if t + 1 < len(steps):
            scores(0, steps[t + 1])
        add_bias(1, step[2])
        softmax_pv(1, step)

    lp = lam_ref[...]
    lam = (jnp.exp(jnp.sum(lp[0:1] * lp[1:2], axis=-1, keepdims=True))
           - jnp.exp(jnp.sum(lp[2:3] * lp[3:4], axis=-1, keepdims=True))
           + lambda_init)
    for i in range(nb):
        a0 = acc_ref[i, 0]
        a1 = acc_ref[i, 1]
        o = a0[0:dv] / a0[dv:dv + 1] - lam * (a1[0:dv] / a1[dv:dv + 1])
        o = o * lax.rsqrt(jnp.mean(o * o, axis=0, keepdims=True) + EPS)
        o = o * sw_ref[...] * (1.0 - lambda_init)
        o_ref[i * tb:(i + 1) * tb, :] = o.T.astype(BF16)


def _attn(q, k, v, tiles, lam_p, sw, *, batch, seq, lambda_init):
    nb = seq // ATTN_BLOCK
    seq_head = pl.BlockSpec((seq, DA_V_DIM), lambda b, h: (b, h))
    acc_rows = DA_V_DIM + ATTN_ONES_ROWS
    return pl.pallas_call(
        functools.partial(_attn_kernel, lambda_init=lambda_init, seq=seq),
        out_shape=jax.ShapeDtypeStruct((batch * seq, DA_WIDTH), BF16),
        grid=(batch, DA_HEADS),
        in_specs=[seq_head, seq_head, seq_head,
                  pl.BlockSpec((None, 2, BIAS_TILE, BIAS_TILE),
                               lambda b, h: (h, 0, 0, 0)),
                  _resident((4, DA_HEAD_DIM)), _resident((DA_V_DIM, ATTN_BLOCK))],
        out_specs=seq_head,
        scratch_shapes=[pltpu.VMEM((nb, acc_rows, ATTN_BLOCK), BF16),
                        pltpu.VMEM((nb, 2, DA_V_DIM, ATTN_BLOCK), BF16),
                        pltpu.VMEM((ATTN_BLOCK, ATTN_BLOCK), F32),
                        pltpu.VMEM((ATTN_BLOCK, ATTN_BLOCK), F32),
                        pltpu.VMEM((nb, 2, 1, ATTN_BLOCK), F32),
                        pltpu.VMEM((nb, 2, acc_rows, ATTN_BLOCK), F32)],
        compiler_params=_params("parallel", "parallel"),
        name="diff_attn",
    )(q, k, v, tiles, lam_p, sw)


def _mem_kv_kernel(mem_ref, nw_ref, wk_ref, wv_ref, k_ref, v_ref):
    mn = _rms(mem_ref[...], nw_ref[...])
    k_ref[...] = _dot(mn, wk_ref[...]).astype(BF16)
    v_ref[...] = _dot(mn, wv_ref[...]).astype(BF16)


def _mem_kv(mem, nw, wk, wv):
    t = mem.shape[0]
    row = pl.BlockSpec((MEM_LEN, D_MODEL), lambda i: (i, 0))
    return pl.pallas_call(
        _mem_kv_kernel,
        out_shape=(jax.ShapeDtypeStruct((t, D_MODEL), BF16),
                   jax.ShapeDtypeStruct((t, D_MODEL), BF16)),
        grid=(t // MEM_LEN,),
        in_specs=[row, _resident((1, D_MODEL)), _resident((D_MODEL, D_MODEL)),
                  _resident((D_MODEL, D_MODEL))],
        out_specs=(row, row),
        compiler_params=_params("parallel"),
        name="mem_kv",
    )(mem, nw, wk, wv)


def _mix_out_cross_kernel(x_ref, ys_ref, xs_ref, z_ref, ya_ref, dskip_ref, snw_ref,
                          ws_ref, wa_ref, nw_ref, wq_ref, k_ref, v_ref, wo_ref, o_ref):
    y_ssd = _ssd_gate_norm(ys_ref[...], xs_ref[...], z_ref[...], dskip_ref[...],
                           snw_ref[...])
    x = (x_ref[...] + _dot(ya_ref[...].astype(F32), wa_ref[...])
         + _dot(y_ssd, ws_ref[...]))
    h = _rms(x, nw_ref[...])
    q = (_dot(h, wq_ref[...]) * (CROSS_HEAD_DIM ** -0.5)).astype(BF16)
    outs = []
    for hh in range(CROSS_HEADS):
        sl = slice(hh * CROSS_HEAD_DIM, (hh + 1) * CROSS_HEAD_DIM)
        s = _dot_nt(q[:, sl], k_ref[:, sl])
        p = jnp.exp(s - jnp.max(s, axis=-1, keepdims=True))
        o = _dot(p.astype(BF16), v_ref[:, sl])
        outs.append(o / jnp.sum(p, axis=-1, keepdims=True))
    o_ref[...] = x + _dot(jnp.concatenate(outs, axis=-1), wo_ref[...])


def _mix_out_cross(x1, y_scan, xbc, z, y_attn, dskip, snw, w_out, nw, wq, k, v, wo,
                   *, seq):
    t = x1.shape[0]
    tiles_per_batch = seq // ROW_TILE
    row = pl.BlockSpec((ROW_TILE, D_MODEL), lambda i: (i, 0))
    mem = pl.BlockSpec((MEM_LEN, D_MODEL), lambda i: (i // tiles_per_batch, 0))
    assert SSD_WIDTH == D_MODEL
    assert SSD_WIDTH == DA_WIDTH

    def w_half(j):
        return pl.BlockSpec((SSD_WIDTH, D_MODEL), lambda i: (j, 0),
                            pipeline_mode=pl.Buffered(1))

    return pl.pallas_call(
        _mix_out_cross_kernel,
        out_shape=jax.ShapeDtypeStruct((t, D_MODEL), F32),
        grid=(t // ROW_TILE,),
        in_specs=[row, row, row, row, row, _resident((1, SSD_WIDTH)),
                  _resident((1, SSD_WIDTH)), w_half(0), w_half(1),
                  _resident((1, D_MODEL)), _resident((D_MODEL, D_MODEL)), mem, mem,
                  _resident((D_MODEL, D_MODEL))],
        out_specs=row,
        compiler_params=_params("parallel"),
        name="mix_out_cross",
    )(x1, y_scan, xbc, z, y_attn, dskip, snw, w_out, w_out, nw, wq, k, v, wo)


def kernel(x, mem, norm_ffn1_w, ffn1_w_in, ffn1_w_out, norm_mix_w, w_in_mix, conv_w, conv_b, dt_bias, a_log, d_skip, ssd_norm_w, lambda_q1, lambda_k1, lambda_q2, lambda_k2, subln_w, rel_bias, w_out_mix, norm_cross_w, norm_mem_w, w_cq, w_ck, w_cv, w_co, norm_ffn2_w, ffn2_w_in, ffn2_w_out, norm_final_w):
    batch, seq, _ = x.shape
    depth = ffn1_w_in.shape[0]
    t = batch * seq
    xf = x.reshape(t, D_MODEL)
    memf = mem.reshape(batch * MEM_LEN, D_MODEL)
    fw = norm_final_w.reshape(1, D_MODEL)
    tiles = _rel_bias_tiles(rel_bias)

    def vec(p):
        return p.reshape(1, -1)

    def pad_lanes(p):
        return jnp.pad(p.reshape(1, -1), ((0, 0), (0, DT_PAD - p.shape[-1])))

    for l in range(depth):
        lambda_init = 0.8 - 0.6 * math.exp(-0.3 * l)
        xf = _ffn(xf, vec(norm_ffn1_w[l]), ffn1_w_in[l], ffn1_w_out[l], fw,
                  final=False, name="ffn1")

        wm = w_in_mix[l].astype(BF16)
        o_dt = SSD_WIDTH + CONV_CH
        o_q = o_dt + SSD_HEADS
        o_k = o_q + DA_WIDTH
        o_v = o_k + DA_WIDTH
        wdt = jnp.pad(wm[:, o_dt:o_q], ((0, 0), (0, DT_PAD - SSD_HEADS)))
        z, xbc, cum, src, q, k, v = _in_proj(
            xf, vec(norm_mix_w[l]), wm, wdt,
            wm[:, o_q:o_k], wm[:, o_k:o_v], wm[:, o_v:o_v + DA_WIDTH],
            conv_w[l], vec(conv_b[l]), pad_lanes(dt_bias[l]), pad_lanes(a_log[l]),
            seq=seq)

        y_scan = _ssd(xbc, cum, src, batch=batch, seq=seq)

        lam_p = jnp.stack([lambda_q1[l], lambda_k1[l], lambda_q2[l], lambda_k2[l]])
        sw = jnp.broadcast_to(subln_w[l].reshape(DA_V_DIM, 1), (DA_V_DIM, ATTN_BLOCK))
        y_attn = _attn(q, k, v, tiles, lam_p, sw,
                       batch=batch, seq=seq, lambda_init=lambda_init)

        mk, mv = _mem_kv(memf, vec(norm_mem_w[l]), w_ck[l], w_cv[l])
        xf = _mix_out_cross(xf, y_scan, xbc, z, y_attn,
                            vec(jnp.repeat(d_skip[l], SSD_HEAD_DIM)),
                            vec(ssd_norm_w[l]), w_out_mix[l],
                            vec(norm_cross_w[l]), w_cq[l], mk, mv, w_co[l], seq=seq)

        xf = _ffn(xf, vec(norm_ffn2_w[l]), ffn2_w_in[l], ffn2_w_out[l], fw,
                  final=(l == depth - 1), name="ffn2")
    return xf.reshape(batch, seq, D_MODEL)
```

```python
import functools
import math

import jax
import jax.numpy as jnp
from jax import lax
from jax.experimental import pallas as pl
from jax.experimental.pallas import tpu as pltpu

F32 = jnp.float32
BF16 = jnp.bfloat16

D_MODEL = 1024
MEM_LEN = 256
EPS = 1e-6

SSD_HEADS = 16
SSD_HEAD_DIM = 64
SSD_WIDTH = SSD_HEADS * SSD_HEAD_DIM
SSD_GROUPS = 4
SSD_STATE = 128
CONV_WIDTH = 4
SSD_CHUNK = 128
CONV_CH = SSD_WIDTH + 2 * SSD_GROUPS * SSD_STATE
HEADS_PER_GROUP = SSD_HEADS // SSD_GROUPS

DA_HEADS = 8
DA_HEAD_DIM = 64
DA_V_DIM = 2 * DA_HEAD_DIM
DA_WIDTH = DA_HEADS * DA_V_DIM

NUM_BUCKETS = 32
MAX_DISTANCE = 128

CROSS_HEADS = 4
CROSS_HEAD_DIM = D_MODEL // CROSS_HEADS

D_FF = 2816
LOG2_E = math.log2(math.e)

LANES = 128
SUBLANES = 8
MXU_DIM = 256
VMEM_LIMIT_BYTES = 56 * 1024 * 1024

ROW_TILE = 512
FF_CHUNK = MXU_DIM
ATTN_BLOCK = 512
ATTN_ONES_ROWS = 16
BIAS_TILE = MAX_DISTANCE
DT_PAD = LANES
CONV_HALO = SUBLANES
SSD_CHUNKS_PER_STEP = 4


def _resident(shape):
    nd = len(shape)
    return pl.BlockSpec(shape, lambda *_: (0,) * nd, pipeline_mode=pl.Buffered(1))


def _params(*sem):
    return pltpu.CompilerParams(dimension_semantics=sem,
                                vmem_limit_bytes=VMEM_LIMIT_BYTES)


def _rms(x, w):
    ms = jnp.mean(x * x, axis=-1, keepdims=True)
    return x * lax.rsqrt(ms + EPS) * w


def _silu(x):
    h = 0.5 * x
    return h + h * jnp.tanh(h)


def _dot(a, b):
    return jnp.dot(a, b, preferred_element_type=F32)


def _dot_nt(a, b):
    return lax.dot_general(a, b, (((1,), (1,)), ((), ())),
                           preferred_element_type=F32)


def _ffn_kernel(x_ref, nw_ref, win_ref, wout_ref, fw_ref, o_ref, *, final):
    x = x_ref[...]
    h = _rms(x, nw_ref[...])
    acc = None
    for c in range(D_FF // FF_CHUNK):
        lo = c * FF_CHUNK
        g = _dot(h, win_ref[:, lo:lo + FF_CHUNK])
        u = _dot(h, win_ref[:, D_FF + lo:D_FF + lo + FF_CHUNK])
        a = _silu(g) * u
        d = _dot(a, wout_ref[lo:lo + FF_CHUNK, :])
        acc = d if acc is None else acc + d
    y = x + 0.5 * acc
    if final:
        y = _rms(y, fw_ref[...])
    o_ref[...] = y


def _ffn(x, nw, win, wout, fw, *, final, name):
    t = x.shape[0]
    row = pl.BlockSpec((ROW_TILE, D_MODEL), lambda i: (i, 0))
    return pl.pallas_call(
        functools.partial(_ffn_kernel, final=final),
        out_shape=jax.ShapeDtypeStruct((t, D_MODEL), F32),
        grid=(t // ROW_TILE,),
        in_specs=[row, _resident((1, D_MODEL)), _resident((D_MODEL, 2 * D_FF)),
                  _resident((D_FF, D_MODEL)), _resident((1, D_MODEL))],
        out_specs=row,
        compiler_params=_params("parallel"),
        name=name,
    )(x, nw, win, wout, fw)


def _in_proj_kernel(x_ref, nw_ref, wzx_ref, wdt_ref, wq_ref, wk_ref, wv_ref,
                    cw_ref, cb_ref, dtb_ref, alog_ref,
                    z_ref, xbc_ref, cum_ref, src_ref, q_ref, k_ref, v_ref, ext_ref,
                    *, tiles_per_seq):
    i = pl.program_id(0)
    tm = ROW_TILE
    h = _rms(x_ref[...], nw_ref[...])

    @pl.when(i % tiles_per_seq == 0)
    def _():
        ext_ref[:, 0:CONV_HALO, :] = jnp.zeros((CONV_CH // LANES, CONV_HALO, LANES), F32)

    x_dt = _dot(h, wdt_ref[...]) + dtb_ref[...]
    dt = jnp.maximum(x_dt, 0.0) + jnp.log1p(jnp.exp(-jnp.abs(x_dt)))
    log2_dt = jnp.log2(dt)
    la = dt * (-LOG2_E * jnp.exp(alog_ref[...]))
    la_hi = la.astype(BF16)
    la_mid = (la - la_hi.astype(F32)).astype(BF16)
    la_lo = (la - la_hi.astype(F32) - la_mid.astype(F32)).astype(BF16)
    row_i = lax.broadcasted_iota(jnp.int32, (SSD_CHUNK, SSD_CHUNK), 0)
    col_i = lax.broadcasted_iota(jnp.int32, (SSD_CHUNK, SSD_CHUNK), 1)
    tril = (col_i <= row_i).astype(BF16)

    wide = ((z_ref, wzx_ref.at[:, 0:SSD_WIDTH], 1.0),
            (q_ref, wq_ref, DA_HEAD_DIM ** -0.5 * LOG2_E),
            (k_ref, wk_ref, 1.0), (v_ref, wv_ref, 1.0))
    n_chunks = tm // SSD_CHUNK
    assert n_chunks == len(wide)
    u = _dot(h, wzx_ref[:, SSD_WIDTH:SSD_WIDTH + CONV_CH])
    for t in range(CONV_CH // LANES):
        ext_ref[t, CONV_HALO:CONV_HALO + tm, :] = u[:, t * LANES:(t + 1) * LANES]
    for r in range(n_chunks):
        lo = r * SSD_CHUNK
        rows = slice(lo, lo + SSD_CHUNK)
        cum = (_dot(tril, la_hi[rows]) + _dot(tril, la_mid[rows])
               + _dot(tril, la_lo[rows]))
        cum_ref[rows, :] = cum
        src_ref[rows, :] = cum - log2_dt[rows]
        base = CONV_HALO + lo
        acts = []
        for t in range(CONV_CH // LANES):
            ln = slice(t * LANES, (t + 1) * LANES)
            acc = cb_ref[:, ln]
            for j in range(CONV_WIDTH):
                start = base - (CONV_WIDTH - 1) + j
                acc = acc + cw_ref[j:j + 1, ln] * ext_ref[t, start:start + SSD_CHUNK, :]
            acts.append(_silu(acc))
        xbc_ref[rows, :] = jnp.concatenate(acts, axis=-1).astype(BF16)
        o_ref, w_ref, scale = wide[r]
        out = _dot(h, w_ref[...])
        o_ref[...] = (out if scale == 1.0 else out * scale).astype(BF16)
    ext_ref[:, 0:CONV_HALO, :] = ext_ref[:, tm:tm + CONV_HALO, :]


def _in_proj(x1, nw, w_all, wdt, wq, wk, wv, cw, cb, dtb, alog, *, seq):
    t = x1.shape[0]

    def row(n):
        return pl.BlockSpec((ROW_TILE, n), lambda i: (i, 0))

    return pl.pallas_call(
        functools.partial(_in_proj_kernel, tiles_per_seq=seq // ROW_TILE),
        out_shape=(jax.ShapeDtypeStruct((t, SSD_WIDTH), BF16),
                   jax.ShapeDtypeStruct((t, CONV_CH), BF16),
                   jax.ShapeDtypeStruct((t, DT_PAD), F32),
                   jax.ShapeDtypeStruct((t, DT_PAD), F32),
                   jax.ShapeDtypeStruct((t, DA_WIDTH), BF16),
                   jax.ShapeDtypeStruct((t, DA_WIDTH), BF16),
                   jax.ShapeDtypeStruct((t, DA_WIDTH), BF16)),
        grid=(t // ROW_TILE,),
        in_specs=[row(D_MODEL), _resident((1, D_MODEL)),
                  _resident((D_MODEL, SSD_WIDTH + CONV_CH)),
                  _resident((D_MODEL, DT_PAD)), _resident((D_MODEL, DA_WIDTH)),
                  _resident((D_MODEL, DA_WIDTH)), _resident((D_MODEL, DA_WIDTH)),
                  _resident((CONV_WIDTH, CONV_CH)), _resident((1, CONV_CH)),
                  _resident((1, DT_PAD)), _resident((1, DT_PAD))],
        out_specs=(row(SSD_WIDTH), row(CONV_CH), row(DT_PAD), row(DT_PAD),
                   row(DA_WIDTH), row(DA_WIDTH), row(DA_WIDTH)),
        scratch_shapes=[pltpu.VMEM((CONV_CH // LANES, CONV_HALO + ROW_TILE + CONV_HALO,
                                    LANES), F32)],
        compiler_params=_params("arbitrary"),
        name="in_proj",
    )(x1, nw, w_all, wdt, wq, wk, wv, cw, cb, dtb, alog)


def _ssd_chunk(xbc, cum_col, src_col, state_ref):
    L = SSD_CHUNK
    hp = SSD_HEAD_DIM
    xs_b = xbc[:, :SSD_WIDTH]
    row_i = lax.broadcasted_iota(jnp.int32, (L, L), 0)
    col_i = lax.broadcasted_iota(jnp.int32, (L, L), 1)
    causal = col_i <= row_i
    first = col_i < hp
    keep = (first.astype(BF16), 1.0 - first.astype(BF16))
    src_row = src_col.T

    ys = []
    for g in range(SSD_GROUPS):
        b_g = xbc[:, SSD_WIDTH + g * SSD_STATE:SSD_WIDTH + (g + 1) * SSD_STATE]
        c_g = xbc[:, SSD_WIDTH + (SSD_GROUPS + g) * SSD_STATE:
                  SSD_WIDTH + (SSD_GROUPS + g + 1) * SSD_STATE]
        cb = _dot_nt(c_g, b_g)
        bt_g = b_g.astype(F32).T
        for pr in range(HEADS_PER_GROUP // 2):
            pair = g * (HEADS_PER_GROUP // 2) + pr
            x_pair = xs_b[:, pair * 2 * hp:(pair + 1) * 2 * hp]
            x_bd = jnp.concatenate([x_pair * keep[0], x_pair * keep[1]], axis=0)
            cols, w_parts, b_parts = [], [], []
            for t in range(2):
                h = 2 * pair + t
                col_b = jnp.broadcast_to(cum_col[:, h:h + 1], (L, L))
                row_b = jnp.broadcast_to(src_row[h:h + 1, :], (L, L))
                dec = jnp.exp2(jnp.where(causal, col_b - row_b, -jnp.inf))
                w_parts.append((cb * dec).astype(BF16))
                b_parts.append((bt_g * dec[L - 1:L, :]).astype(BF16))
                cols.append(col_b)
            from_start = jnp.exp2(jnp.where(first, cols[0], cols[1]))
            s_prev = state_ref[pair]
            ys.append(_dot(jnp.concatenate(w_parts, axis=1), x_bd)
                      + _dot(c_g, s_prev.astype(BF16)) * from_start)
            new = _dot(jnp.concatenate(b_parts, axis=1), x_bd)
            state_ref[pair] = s_prev * from_start[L - 1:L, :] + new
    return jnp.concatenate(ys, axis=-1).astype(BF16)


def _ssd_kernel(xbc_ref, cum_ref, src_ref, o_ref, state_ref):
    @pl.when(pl.program_id(1) == 0)
    def _():
        state_ref[...] = jnp.zeros_like(state_ref)

    for r in range(SSD_CHUNKS_PER_STEP):
        rows = slice(r * SSD_CHUNK, (r + 1) * SSD_CHUNK)
        o_ref[rows, :] = _ssd_chunk(xbc_ref[rows, :], cum_ref[rows, :], src_ref[rows, :],
                                    state_ref)


def _ssd_gate_norm(y, xs, z, dskip, nw):
    y = y.astype(F32) + dskip * xs.astype(F32)
    y = y * _silu(z.astype(F32))
    gw = SSD_WIDTH // SSD_GROUPS
    outs = []
    for g in range(SSD_GROUPS):
        yg = y[:, g * gw:(g + 1) * gw]
        ms = jnp.mean(yg * yg, axis=-1, keepdims=True)
        outs.append(yg * lax.rsqrt(ms + EPS))
    return jnp.concatenate(outs, axis=-1) * nw


def _ssd(xbc, cum, src, *, batch, seq):
    step_rows = SSD_CHUNKS_PER_STEP * SSD_CHUNK
    ns = seq // step_rows

    def row(n):
        return pl.BlockSpec((step_rows, n), lambda b, c: (b * ns + c, 0))

    return pl.pallas_call(
        _ssd_kernel,
        out_shape=jax.ShapeDtypeStruct((batch * seq, SSD_WIDTH), BF16),
        grid=(batch, ns),
        in_specs=[row(CONV_CH), row(DT_PAD), row(DT_PAD)],
        out_specs=row(SSD_WIDTH),
        scratch_shapes=[pltpu.VMEM((SSD_HEADS // 2, SSD_STATE, 2 * SSD_HEAD_DIM), F32)],
        compiler_params=_params("parallel", "arbitrary"),
        name="ssd",
    )(xbc, cum, src)


def _rel_bias_kernel(rb_ref, o_ref):
    k_i = lax.broadcasted_iota(jnp.int32, (BIAS_TILE, BIAS_TILE), 0)
    q_i = lax.broadcasted_iota(jnp.int32, (BIAS_TILE, BIAS_TILE), 1)
    max_exact = NUM_BUCKETS // 2
    for d in range(2):
        n = q_i - k_i + d * BIAS_TILE
        nf = jnp.maximum(n, 1).astype(F32)
        large = max_exact + (jnp.log(nf / max_exact) / math.log(MAX_DISTANCE / max_exact)
                             * (NUM_BUCKETS - max_exact)).astype(jnp.int32)
        large = jnp.minimum(large, NUM_BUCKETS - 1)
        bucket = jnp.where(n < max_exact, n, large)
        hits = [bucket == b for b in range(NUM_BUCKETS)]
        for h in range(DA_HEADS):
            tile = jnp.zeros((BIAS_TILE, BIAS_TILE), F32)
            for b in range(NUM_BUCKETS):
                tile = jnp.where(hits[b], rb_ref[b, h], tile)
            tile = (tile - rb_ref[NUM_BUCKETS - 1, h]) * LOG2_E
            o_ref[h, d] = jnp.where(n >= 0, tile, -jnp.inf)


def _rel_bias_tiles(rel_bias):
    return pl.pallas_call(
        _rel_bias_kernel,
        out_shape=jax.ShapeDtypeStruct((DA_HEADS, 2, BIAS_TILE, BIAS_TILE), F32),
        in_specs=[pl.BlockSpec(memory_space=pltpu.SMEM)],
        out_specs=pl.BlockSpec(memory_space=pltpu.VMEM),
        compiler_params=_params(),
        name="rel_bias",
    )(rel_bias)


def _attn_kernel(q_ref, k_ref, v_ref, bt_ref, lam_ref, sw_ref, o_ref,
                 vt_ref, qz_ref, s0_ref, s1_ref, m_ref, acc_ref, *, lambda_init, seq):
    tb = ATTN_BLOCK
    dh = DA_HEAD_DIM
    dv = DA_V_DIM
    nb = seq // tb

    for j in range(nb):
        vt = v_ref[j * tb:(j + 1) * tb, :].astype(F32).T
        vt_ref[j, 0:dv, :] = vt.astype(BF16)
        vt_ref[j, dv:dv + ATTN_ONES_ROWS, :] = jnp.ones((ATTN_ONES_ROWS, tb), BF16)

    for i in range(nb):
        qt = q_ref[i * tb:(i + 1) * tb, :].astype(F32).T
        feat = lax.broadcasted_iota(jnp.int32, qt.shape, 0)
        for c in range(2):
            in_map = (feat >= c * dh) & (feat < (c + 1) * dh)
            qz_ref[i, c] = jnp.where(in_map, qt, 0.0).astype(BF16)

    s_refs = (s0_ref, s1_ref)
    half = tb // 2
    bt = BIAS_TILE
    nsub = tb // bt
    full = ((0, tb, 0, tb),)
    diag_parts = ((0, half, 0, half), (0, tb, half, tb))

    zero = jnp.minimum(pl.program_id(0), 0)

    def rows(lo, hi):
        return pl.ds(pl.multiple_of(zero + lo, BIAS_TILE), hi - lo)

    def scores(c, step):
        i, j, kind = step
        k0 = j * tb
        if kind != "diag":
            s_refs[c][rows(0, tb), :] = _dot(k_ref[k0:k0 + tb, :], qz_ref[i, c])
        else:
            s_refs[c][rows(0, half), :] = _dot(k_ref[k0:k0 + half, :], qz_ref[i, c])
            s_refs[c][rows(half, tb), half:tb] = _dot(k_ref[k0 + half:k0 + tb, :],
                                                      qz_ref[i, c, :, half:tb])

    def add_bias(c, kind):
        s_ref = s_refs[c]
        if kind == "near":
            near = (rows(tb - bt, tb), slice(0, bt))
            s_ref[near] = s_ref[near] + bt_ref[1]
        elif kind == "diag":
            for a in range(nsub):
                on = (rows(a * bt, (a + 1) * bt), slice(a * bt, (a + 1) * bt))
                s_ref[on] = s_ref[on] + bt_ref[0]
                if a + 1 < nsub:
                    off = (rows(a * bt, (a + 1) * bt), slice((a + 1) * bt, (a + 2) * bt))
                    s_ref[off] = s_ref[off] + bt_ref[1]
            for k0, k1, q0, q1 in diag_parts:
                for a in range(k0 // bt, k1 // bt):
                    for b in range(q0 // bt, q1 // bt):
                        if a > b:
                            s_ref[rows(a * bt, (a + 1) * bt), b * bt:(b + 1) * bt] = (
                                jnp.full((bt, bt), -jnp.inf, F32))

    def softmax_pv(c, step):
        i, j, kind = step
        for k0, k1, q0, q1 in (diag_parts if kind == "diag" else full):
            s = s_refs[c][rows(k0, k1), q0:q1]
            m_blk = jnp.max(s, axis=0, keepdims=True)
            if j == 0:
                m_new = m_blk
            else:
                m_old = m_ref[i, c, :, q0:q1]
                m_new = jnp.maximum(m_old, m_blk)
            pv = _dot(vt_ref[j, :, k0:k1], jnp.exp2(s - m_new).astype(BF16))
            if j == 0:
                acc_ref[i, c, :, q0:q1] = pv
            else:
                acc_ref[i, c, :, q0:q1] = (acc_ref[i, c, :, q0:q1]
                                           * jnp.exp2(m_old - m_new) + pv)
            m_ref[i, c, :, q0:q1] = m_new

    steps = []
    for j in range(nb):
        for i in range(j, nb):
            steps.append((i, j, "diag" if i == j else "near" if i == j + 1 else "far"))

    scores(0, steps[0])
    for t, step in enumerate(steps):
        scores(1, step)
        add_bias(0, step[2])
        softmax_pv(0, step)
        if t + 1 < len(steps):
            scores(0, steps[t + 1])
        add_bias(1, step[2])
        softmax_pv(1, step)

    lp = lam_ref[...]
    lam = (jnp.exp(jnp.sum(lp[0:1] * lp[1:2], axis=-1, keepdims=True))
           - jnp.exp(jnp.sum(lp[2:3] * lp[3:4], axis=-1, keepdims=True))
           + lambda_init)
    for i in range(nb):
        a0 = acc_ref[i, 0]
        a1 = acc_ref[i, 1]
        o = a0[0:dv] / a0[dv:dv + 1] - lam * (a1[0:dv] / a1[dv:dv + 1])
        o = o * lax.rsqrt(jnp.mean(o * o, axis=0, keepdims=True) + EPS)
        o = o * sw_ref[...] * (1.0 - lambda_init)
        o_ref[i * tb:(i + 1) * tb, :] = o.T.astype(BF16)


def _attn(q, k, v, tiles, lam_p, sw, *, batch, seq, lambda_init):
    nb = seq // ATTN_BLOCK
    seq_head = pl.BlockSpec((seq, DA_V_DIM), lambda b, h: (b, h))
    acc_rows = DA_V_DIM + ATTN_ONES_ROWS
    return pl.pallas_call(
        functools.partial(_attn_kernel, lambda_init=lambda_init, seq=seq),
        out_shape=jax.ShapeDtypeStruct((batch * seq, DA_WIDTH), BF16),
        grid=(batch, DA_HEADS),
        in_specs=[seq_head, seq_head, seq_head,
                  pl.BlockSpec((None, 2, BIAS_TILE, BIAS_TILE),
                               lambda b, h: (h, 0, 0, 0)),
                  _resident((4, DA_HEAD_DIM)), _resident((DA_V_DIM, ATTN_BLOCK))],
        out_specs=seq_head,
        scratch_shapes=[pltpu.VMEM((nb, acc_rows, ATTN_BLOCK), BF16),
                        pltpu.VMEM((nb, 2, DA_V_DIM, ATTN_BLOCK), BF16),
                        pltpu.VMEM((ATTN_BLOCK, ATTN_BLOCK), F32),
                        pltpu.VMEM((ATTN_BLOCK, ATTN_BLOCK), F32),
                        pltpu.VMEM((nb, 2, 1, ATTN_BLOCK), F32),
                        pltpu.VMEM((nb, 2, acc_rows, ATTN_BLOCK), F32)],
        compiler_params=_params("parallel", "parallel"),
        name="diff_attn",
    )(q, k, v, tiles, lam_p, sw)


def _mem_kv_kernel(mem_ref, nw_ref, wk_ref, wv_ref, k_ref, v_ref):
    mn = _rms(mem_ref[...], nw_ref[...])
    k_ref[...] = _dot(mn, wk_ref[...]).astype(BF16)
    v_ref[...] = _dot(mn, wv_ref[...]).astype(BF16)


def _mem_kv(mem, nw, wk, wv):
    t = mem.shape[0]
    row = pl.BlockSpec((MEM_LEN, D_MODEL), lambda i: (i, 0))
    return pl.pallas_call(
        _mem_kv_kernel,
        out_shape=(jax.ShapeDtypeStruct((t, D_MODEL), BF16),
                   jax.ShapeDtypeStruct((t, D_MODEL), BF16)),
        grid=(t // MEM_LEN,),
        in_specs=[row, _resident((1, D_MODEL)), _resident((D_MODEL, D_MODEL)),
                  _resident((D_MODEL, D_MODEL))],
        out_specs=(row, row),
        compiler_params=_params("parallel"),
        name="mem_kv",
    )(mem, nw, wk, wv)


def _mix_out_cross_kernel(x_ref, ys_ref, xs_ref, z_ref, ya_ref, dskip_ref, snw_ref,
                          ws_ref, wa_ref, nw_ref, wq_ref, k_ref, v_ref, wo_ref, o_ref):
    y_ssd = _ssd_gate_norm(ys_ref[...], xs_ref[...], z_ref[...], dskip_ref[...],
                           snw_ref[...])
    x = (x_ref[...] + _dot(ya_ref[...].astype(F32), wa_ref[...])
         + _dot(y_ssd, ws_ref[...]))
    h = _rms(x, nw_ref[...])
    q = (_dot(h, wq_ref[...]) * (CROSS_HEAD_DIM ** -0.5)).astype(BF16)
    outs = []
    for hh in range(CROSS_HEADS):
        sl = slice(hh * CROSS_HEAD_DIM, (hh + 1) * CROSS_HEAD_DIM)
        s = _dot_nt(q[:, sl], k_ref[:, sl])
        p = jnp.exp(s - jnp.max(s, axis=-1, keepdims=True))
        o = _dot(p.astype(BF16), v_ref[:, sl])
        outs.append(o / jnp.sum(p, axis=-1, keepdims=True))
    o_ref[...] = x + _dot(jnp.concatenate(outs, axis=-1), wo_ref[...])


def _mix_out_cross(x1, y_scan, xbc, z, y_attn, dskip, snw, w_out, nw, wq, k, v, wo,
                   *, seq):
    t = x1.shape[0]
    tiles_per_batch = seq // ROW_TILE
    row = pl.BlockSpec((ROW_TILE, D_MODEL), lambda i: (i, 0))
    mem = pl.BlockSpec((MEM_LEN, D_MODEL), lambda i: (i // tiles_per_batch, 0))
    assert SSD_WIDTH == D_MODEL
    assert SSD_WIDTH == DA_WIDTH

    def w_half(j):
        return pl.BlockSpec((SSD_WIDTH, D_MODEL), lambda i: (j, 0),
                            pipeline_mode=pl.Buffered(1))

    return pl.pallas_call(
        _mix_out_cross_kernel,
        out_shape=jax.ShapeDtypeStruct((t, D_MODEL), F32),
        grid=(t // ROW_TILE,),
        in_specs=[row, row, row, row, row, _resident((1, SSD_WIDTH)),
                  _resident((1, SSD_WIDTH)), w_half(0), w_half(1),
                  _resident((1, D_MODEL)), _resident((D_MODEL, D_MODEL)), mem, mem,
                  _resident((D_MODEL, D_MODEL))],
        out_specs=row,
        compiler_params=_params("parallel"),
        name="mix_out_cross",
    )(x1, y_scan, xbc, z, y_attn, dskip, snw, w_out, w_out, nw, wq, k, v, wo)


def kernel(x, mem, norm_ffn1_w, ffn1_w_in, ffn1_w_out, norm_mix_w, w_in_mix, conv_w, conv_b, dt_bias, a_log, d_skip, ssd_norm_w, lambda_q1, lambda_k1, lambda_q2, lambda_k2, subln_w, rel_bias, w_out_mix, norm_cross_w, norm_mem_w, w_cq, w_ck, w_cv, w_co, norm_ffn2_w, ffn2_w_in, ffn2_w_out, norm_final_w):
    batch, seq, _ = x.shape
    depth = ffn1_w_in.shape[0]
    t = batch * seq
    xf = x.reshape(t, D_MODEL)
    memf = mem.reshape(batch * MEM_LEN, D_MODEL)
    fw = norm_final_w.reshape(1, D_MODEL)
    tiles = _rel_bias_tiles(rel_bias)

    def vec(p):
        return p.reshape(1, -1)

    def pad_lanes(p):
        return jnp.pad(p.reshape(1, -1), ((0, 0), (0, DT_PAD - p.shape[-1])))

    for l in range(depth):
        lambda_init = 0.8 - 0.6 * math.exp(-0.3 * l)
        xf = _ffn(xf, vec(norm_ffn1_w[l]), ffn1_w_in[l], ffn1_w_out[l], fw,
                  final=False, name="ffn1")

        wm = w_in_mix[l]
        o_dt = SSD_WIDTH + CONV_CH
        o_q = o_dt + SSD_HEADS
        o_k = o_q + DA_WIDTH
        o_v = o_k + DA_WIDTH
        wdt = jnp.pad(wm[:, o_dt:o_q], ((0, 0), (0, DT_PAD - SSD_HEADS)))
        z, xbc, cum, src, q, k, v = _in_proj(
            xf, vec(norm_mix_w[l]), wm, wdt,
            wm[:, o_q:o_k], wm[:, o_k:o_v], wm[:, o_v:o_v + DA_WIDTH],
            conv_w[l], vec(conv_b[l]), pad_lanes(dt_bias[l]), pad_lanes(a_log[l]),
            seq=seq)

        y_scan = _ssd(xbc, cum, src, batch=batch, seq=seq)

        lam_p = jnp.stack([lambda_q1[l], lambda_k1[l], lambda_q2[l], lambda_k2[l]])
        sw = jnp.broadcast_to(subln_w[l].reshape(DA_V_DIM, 1), (DA_V_DIM, ATTN_BLOCK))
        y_attn = _attn(q, k, v, tiles, lam_p, sw,
                       batch=batch, seq=seq, lambda_init=lambda_init)

        mk, mv = _mem_kv(memf, vec(norm_mem_w[l]), w_ck[l], w_cv[l])
        xf = _mix_out_cross(xf, y_scan, xbc, z, y_attn,
                            vec(jnp.repeat(d_skip[l], SSD_HEAD_DIM)),
                            vec(ssd_norm_w[l]), w_out_mix[l],
                            vec(norm_cross_w[l]), w_cq[l], mk, mv, w_co[l], seq=seq)

        xf = _ffn(xf, vec(norm_ffn2_w[l]), ffn2_w_in[l], ffn2_w_out[l], fw,
                  final=(l == depth - 1), name="ffn2")
    return xf.reshape(batch, seq, D_MODEL)
```

```python
import functools
import math

import jax
import jax.numpy as jnp
from jax import lax
from jax.experimental import pallas as pl
from jax.experimental.pallas import tpu as pltpu

F32 = jnp.float32
BF16 = jnp.bfloat16

D_MODEL = 1024
MEM_LEN = 256
EPS = 1e-6

SSD_HEADS = 16
SSD_HEAD_DIM = 64
SSD_WIDTH = SSD_HEADS * SSD_HEAD_DIM
SSD_GROUPS = 4
SSD_STATE = 128
CONV_WIDTH = 4
SSD_CHUNK = 128
CONV_CH = SSD_WIDTH + 2 * SSD_GROUPS * SSD_STATE
HEADS_PER_GROUP = SSD_HEADS // SSD_GROUPS

DA_HEADS = 8
DA_HEAD_DIM = 64
DA_V_DIM = 2 * DA_HEAD_DIM
DA_WIDTH = DA_HEADS * DA_V_DIM

NUM_BUCKETS = 32
MAX_DISTANCE = 128

CROSS_HEADS = 4
CROSS_HEAD_DIM = D_MODEL // CROSS_HEADS

D_FF = 2816
LOG2_E = math.log2(math.e)

LANES = 128
SUBLANES = 8
MXU_DIM = 256
VMEM_LIMIT_BYTES = 56 * 1024 * 1024

ROW_TILE = 512
FF_CHUNK = MXU_DIM
ATTN_BLOCK = 512
ATTN_ONES_ROWS = 16
BIAS_TILE = MAX_DISTANCE
DT_PAD = LANES
CONV_HALO = SUBLANES
SSD_CHUNKS_PER_STEP = 4


def _resident(shape):
    nd = len(shape)
    return pl.BlockSpec(shape, lambda *_: (0,) * nd, pipeline_mode=pl.Buffered(1))


def _params(*sem):
    return pltpu.CompilerParams(dimension_semantics=sem,
                                vmem_limit_bytes=VMEM_LIMIT_BYTES)


def _rms(x, w):
    ms = jnp.mean(x * x, axis=-1, keepdims=True)
    return x * lax.rsqrt(ms + EPS) * w


def _silu(x):
    h = 0.5 * x
    return h + h * jnp.tanh(h)


def _dot(a, b):
    return jnp.dot(a, b, preferred_element_type=F32)


def _dot_nt(a, b):
    return lax.dot_general(a, b, (((1,), (1,)), ((), ())),
                           preferred_element_type=F32)


def _ffn_kernel(x_ref, nw_ref, win_ref, wout_ref, fw_ref, o_ref, *, final):
    x = x_ref[...]
    h = _rms(x, nw_ref[...])
    acc = None
    for c in range(D_FF // FF_CHUNK):
        lo = c * FF_CHUNK
        g = _dot(h, win_ref[:, lo:lo + FF_CHUNK])
        u = _dot(h, win_ref[:, D_FF + lo:D_FF + lo + FF_CHUNK])
        a = _silu(g) * u
        d = _dot(a, wout_ref[lo:lo + FF_CHUNK, :])
        acc = d if acc is None else acc + d
    y = x + 0.5 * acc
    if final:
        y = _rms(y, fw_ref[...])
    o_ref[...] = y


def _ffn(x, nw, win, wout, fw, *, final, name):
    t = x.shape[0]
    row = pl.BlockSpec((ROW_TILE, D_MODEL), lambda i: (i, 0))
    return pl.pallas_call(
        functools.partial(_ffn_kernel, final=final),
        out_shape=jax.ShapeDtypeStruct((t, D_MODEL), F32),
        grid=(t // ROW_TILE,),
        in_specs=[row, _resident((1, D_MODEL)), _resident((D_MODEL, 2 * D_FF)),
                  _resident((D_FF, D_MODEL)), _resident((1, D_MODEL))],
        out_specs=row,
        compiler_params=_params("parallel"),
        name=name,
    )(x, nw, win, wout, fw)


def _in_proj_kernel(x_ref, nw_ref, wt_ref, cw_ref, cb_ref, dtb_ref, alog_ref,
                    z_ref, xbc_ref, cum_ref, src_ref, q_ref, k_ref, v_ref, ext_ref,
                    *, tiles_per_seq):
    i = pl.program_id(0)
    tm = ROW_TILE
    h = _rms(x_ref[...], nw_ref[...])
    o_xbc = SSD_WIDTH
    o_dt = o_xbc + CONV_CH
    o_q = o_dt + SSD_HEADS
    o_k = o_q + DA_WIDTH
    o_v = o_k + DA_WIDTH

    def proj(lo, width):
        return _dot_nt(h, wt_ref[lo:lo + width, :])

    @pl.when(i % tiles_per_seq == 0)
    def _():
        ext_ref[:, 0:CONV_HALO, :] = jnp.zeros((CONV_CH // LANES, CONV_HALO, LANES), F32)

    x_dt = proj(o_dt, DT_PAD) + dtb_ref[...]
    dt = jnp.maximum(x_dt, 0.0) + jnp.log1p(jnp.exp(-jnp.abs(x_dt)))
    log2_dt = jnp.log2(dt)
    la = dt * (-LOG2_E * jnp.exp(alog_ref[...]))
    la_hi = la.astype(BF16)
    la_mid = (la - la_hi.astype(F32)).astype(BF16)
    la_lo = (la - la_hi.astype(F32) - la_mid.astype(F32)).astype(BF16)
    row_i = lax.broadcasted_iota(jnp.int32, (SSD_CHUNK, SSD_CHUNK), 0)
    col_i = lax.broadcasted_iota(jnp.int32, (SSD_CHUNK, SSD_CHUNK), 1)
    tril = (col_i <= row_i).astype(BF16)

    wide = ((z_ref, 0, SSD_WIDTH, 1.0),
            (q_ref, o_q, DA_WIDTH, DA_HEAD_DIM ** -0.5 * LOG2_E),
            (k_ref, o_k, DA_WIDTH, 1.0), (v_ref, o_v, DA_WIDTH, 1.0))
    n_chunks = tm // SSD_CHUNK
    assert n_chunks == len(wide)
    u = proj(o_xbc, CONV_CH)
    for t in range(CONV_CH // LANES):
        ext_ref[t, CONV_HALO:CONV_HALO + tm, :] = u[:, t * LANES:(t + 1) * LANES]
    for r in range(n_chunks):
        lo = r * SSD_CHUNK
        rows = slice(lo, lo + SSD_CHUNK)
        cum = (_dot(tril, la_hi[rows]) + _dot(tril, la_mid[rows])
               + _dot(tril, la_lo[rows]))
        cum_ref[rows, :] = cum
        src_ref[rows, :] = cum - log2_dt[rows]
        base = CONV_HALO + lo
        acts = []
        for t in range(CONV_CH // LANES):
            ln = slice(t * LANES, (t + 1) * LANES)
            acc = cb_ref[:, ln]
            for j in range(CONV_WIDTH):
                start = base - (CONV_WIDTH - 1) + j
                acc = acc + cw_ref[j:j + 1, ln] * ext_ref[t, start:start + SSD_CHUNK, :]
            acts.append(_silu(acc))
        xbc_ref[rows, :] = jnp.concatenate(acts, axis=-1).astype(BF16)
        o_ref, w_lo, w_width, scale = wide[r]
        out = proj(w_lo, w_width)
        o_ref[...] = (out if scale == 1.0 else out * scale).astype(BF16)
    ext_ref[:, 0:CONV_HALO, :] = ext_ref[:, tm:tm + CONV_HALO, :]


def _in_proj(x1, nw, w_t, cw, cb, dtb, alog, *, seq):
    t = x1.shape[0]
    in_width = w_t.shape[0]
    assert (SSD_WIDTH + CONV_CH + SSD_HEADS) % SUBLANES == 0
    assert SSD_WIDTH + CONV_CH + DT_PAD <= in_width

    def row(n):
        return pl.BlockSpec((ROW_TILE, n), lambda i: (i, 0))

    return pl.pallas_call(
        functools.partial(_in_proj_kernel, tiles_per_seq=seq // ROW_TILE),
        out_shape=(jax.ShapeDtypeStruct((t, SSD_WIDTH), BF16),
                   jax.ShapeDtypeStruct((t, CONV_CH), BF16),
                   jax.ShapeDtypeStruct((t, DT_PAD), F32),
                   jax.ShapeDtypeStruct((t, DT_PAD), F32),
                   jax.ShapeDtypeStruct((t, DA_WIDTH), BF16),
                   jax.ShapeDtypeStruct((t, DA_WIDTH), BF16),
                   jax.ShapeDtypeStruct((t, DA_WIDTH), BF16)),
        grid=(t // ROW_TILE,),
        in_specs=[row(D_MODEL), _resident((1, D_MODEL)),
                  _resident((in_width, D_MODEL)),
                  _resident((CONV_WIDTH, CONV_CH)), _resident((1, CONV_CH)),
                  _resident((1, DT_PAD)), _resident((1, DT_PAD))],
        out_specs=(row(SSD_WIDTH), row(CONV_CH), row(DT_PAD), row(DT_PAD),
                   row(DA_WIDTH), row(DA_WIDTH), row(DA_WIDTH)),
        scratch_shapes=[pltpu.VMEM((CONV_CH // LANES, CONV_HALO + ROW_TILE + CONV_HALO,
                                    LANES), F32)],
        compiler_params=_params("arbitrary"),
        name="in_proj",
    )(x1, nw, w_t, cw, cb, dtb, alog)


def _ssd_chunk(xbc, cum_col, src_col, state_ref):
    L = SSD_CHUNK
    hp = SSD_HEAD_DIM
    xs_b = xbc[:, :SSD_WIDTH]
    row_i = lax.broadcasted_iota(jnp.int32, (L, L), 0)
    col_i = lax.broadcasted_iota(jnp.int32, (L, L), 1)
    causal = col_i <= row_i
    first = col_i < hp
    keep = (first.astype(BF16), 1.0 - first.astype(BF16))
    src_row = src_col.T

    ys = []
    for g in range(SSD_GROUPS):
        b_g = xbc[:, SSD_WIDTH + g * SSD_STATE:SSD_WIDTH + (g + 1) * SSD_STATE]
        c_g = xbc[:, SSD_WIDTH + (SSD_GROUPS + g) * SSD_STATE:
                  SSD_WIDTH + (SSD_GROUPS + g + 1) * SSD_STATE]
        cb = _dot_nt(c_g, b_g)
        bt_g = b_g.astype(F32).T
        for pr in range(HEADS_PER_GROUP // 2):
            pair = g * (HEADS_PER_GROUP // 2) + pr
            x_pair = xs_b[:, pair * 2 * hp:(pair + 1) * 2 * hp]
            x_bd = jnp.concatenate([x_pair * keep[0], x_pair * keep[1]], axis=0)
            cols, w_parts, b_parts = [], [], []
            for t in range(2):
                h = 2 * pair + t
                col_b = jnp.broadcast_to(cum_col[:, h:h + 1], (L, L))
                row_b = jnp.broadcast_to(src_row[h:h + 1, :], (L, L))
                dec = jnp.exp2(jnp.where(causal, col_b - row_b, -jnp.inf))
                w_parts.append((cb * dec).astype(BF16))
                b_parts.append((bt_g * dec[L - 1:L, :]).astype(BF16))
                cols.append(col_b)
            from_start = jnp.exp2(jnp.where(first, cols[0], cols[1]))
            s_prev = state_ref[pair]
            ys.append(_dot(jnp.concatenate(w_parts, axis=1), x_bd)
                      + _dot(c_g, s_prev.astype(BF16)) * from_start)
            new = _dot(jnp.concatenate(b_parts, axis=1), x_bd)
            state_ref[pair] = s_prev * from_start[L - 1:L, :] + new
    return jnp.concatenate(ys, axis=-1).astype(BF16)


def _ssd_kernel(xbc_ref, cum_ref, src_ref, o_ref, state_ref):
    @pl.when(pl.program_id(1) == 0)
    def _():
        state_ref[...] = jnp.zeros_like(state_ref)

    for r in range(SSD_CHUNKS_PER_STEP):
        rows = slice(r * SSD_CHUNK, (r + 1) * SSD_CHUNK)
        o_ref[rows, :] = _ssd_chunk(xbc_ref[rows, :], cum_ref[rows, :], src_ref[rows, :],
                                    state_ref)


def _ssd_gate_norm(y, xs, z, dskip, nw):
    y = y.astype(F32) + dskip * xs.astype(F32)
    y = y * _silu(z.astype(F32))
    gw = SSD_WIDTH // SSD_GROUPS
    outs = []
    for g in range(SSD_GROUPS):
        yg = y[:, g * gw:(g + 1) * gw]
        ms = jnp.mean(yg * yg, axis=-1, keepdims=True)
        outs.append(yg * lax.rsqrt(ms + EPS))
    return jnp.concatenate(outs, axis=-1) * nw


def _ssd(xbc, cum, src, *, batch, seq):
    step_rows = SSD_CHUNKS_PER_STEP * SSD_CHUNK
    ns = seq // step_rows

    def row(n):
        return pl.BlockSpec((step_rows, n), lambda b, c: (b * ns + c, 0))

    return pl.pallas_call(
        _ssd_kernel,
        out_shape=jax.ShapeDtypeStruct((batch * seq, SSD_WIDTH), BF16),
        grid=(batch, ns),
        in_specs=[row(CONV_CH), row(DT_PAD), row(DT_PAD)],
        out_specs=row(SSD_WIDTH),
        scratch_shapes=[pltpu.VMEM((SSD_HEADS // 2, SSD_STATE, 2 * SSD_HEAD_DIM), F32)],
        compiler_params=_params("parallel", "arbitrary"),
        name="ssd",
    )(xbc, cum, src)


def _rel_bias_kernel(rb_ref, o_ref):
    k_i = lax.broadcasted_iota(jnp.int32, (BIAS_TILE, BIAS_TILE), 0)
    q_i = lax.broadcasted_iota(jnp.int32, (BIAS_TILE, BIAS_TILE), 1)
    max_exact = NUM_BUCKETS // 2
    for d in range(2):
        n = q_i - k_i + d * BIAS_TILE
        nf = jnp.maximum(n, 1).astype(F32)
        large = max_exact + (jnp.log(nf / max_exact) / math.log(MAX_DISTANCE / max_exact)
                             * (NUM_BUCKETS - max_exact)).astype(jnp.int32)
        large = jnp.minimum(large, NUM_BUCKETS - 1)
        bucket = jnp.where(n < max_exact, n, large)
        hits = [bucket == b for b in range(NUM_BUCKETS)]
        for h in range(DA_HEADS):
            tile = jnp.zeros((BIAS_TILE, BIAS_TILE), F32)
            for b in range(NUM_BUCKETS):
                tile = jnp.where(hits[b], rb_ref[b, h], tile)
            tile = (tile - rb_ref[NUM_BUCKETS - 1, h]) * LOG2_E
            o_ref[h, d] = jnp.where(n >= 0, tile, -jnp.inf)


def _rel_bias_tiles(rel_bias):
    return pl.pallas_call(
        _rel_bias_kernel,
        out_shape=jax.ShapeDtypeStruct((DA_HEADS, 2, BIAS_TILE, BIAS_TILE), F32),
        in_specs=[pl.BlockSpec(memory_space=pltpu.SMEM)],
        out_specs=pl.BlockSpec(memory_space=pltpu.VMEM),
        compiler_params=_params(),
        name="rel_bias",
    )(rel_bias)


def _attn_kernel(q_ref, k_ref, v_ref, bt_ref, lam_ref, sw_ref, o_ref,
                 vt_ref, qz_ref, s0_ref, s1_ref, m_ref, acc_ref, *, lambda_init, seq):
    tb = ATTN_BLOCK
    dh = DA_HEAD_DIM
    dv = DA_V_DIM
    nb = seq // tb

    for j in range(nb):
        vt = v_ref[j * tb:(j + 1) * tb, :].astype(F32).T
        vt_ref[j, 0:dv, :] = vt.astype(BF16)
        vt_ref[j, dv:dv + ATTN_ONES_ROWS, :] = jnp.ones((ATTN_ONES_ROWS, tb), BF16)

    for i in range(nb):
        qt = q_ref[i * tb:(i + 1) * tb, :].astype(F32).T
        feat = lax.broadcasted_iota(jnp.int32, qt.shape, 0)
        for c in range(2):
            in_map = (feat >= c * dh) & (feat < (c + 1) * dh)
            qz_ref[i, c] = jnp.where(in_map, qt, 0.0).astype(BF16)

    s_refs = (s0_ref, s1_ref)
    half = tb // 2
    bt = BIAS_TILE
    nsub = tb // bt
    full = ((0, tb, 0, tb),)
    diag_parts = ((0, half, 0, half), (0, tb, half, tb))

    zero = jnp.minimum(pl.program_id(0), 0)

    def rows(lo, hi):
        return pl.ds(pl.multiple_of(zero + lo, BIAS_TILE), hi - lo)

    def scores(c, step):
        i, j, kind = step
        k0 = j * tb
        if kind != "diag":
            s_refs[c][rows(0, tb), :] = _dot(k_ref[k0:k0 + tb, :], qz_ref[i, c])
        else:
            s_refs[c][rows(0, half), :] = _dot(k_ref[k0:k0 + half, :], qz_ref[i, c])
            s_refs[c][rows(half, tb), half:tb] = _dot(k_ref[k0 + half:k0 + tb, :],
                                                      qz_ref[i, c, :, half:tb])

    def add_bias(c, kind):
        s_ref = s_refs[c]
        if kind == "near":
            near = (rows(tb - bt, tb), slice(0, bt))
            s_ref[near] = s_ref[near] + bt_ref[1]
        elif kind == "diag":
            for a in range(nsub):
                on = (rows(a * bt, (a + 1) * bt), slice(a * bt, (a + 1) * bt))
                s_ref[on] = s_ref[on] + bt_ref[0]
                if a + 1 < nsub:
                    off = (rows(a * bt, (a + 1) * bt), slice((a + 1) * bt, (a + 2) * bt))
                    s_ref[off] = s_ref[off] + bt_ref[1]
            for k0, k1, q0, q1 in diag_parts:
                for a in range(k0 // bt, k1 // bt):
                    for b in range(q0 // bt, q1 // bt):
                        if a > b:
                            s_ref[rows(a * bt, (a + 1) * bt), b * bt:(b + 1) * bt] = (
                                jnp.full((bt, bt), -jnp.inf, F32))

    def softmax_pv(c, step):
        i, j, kind = step
        for k0, k1, q0, q1 in (diag_parts if kind == "diag" else full):
            s = s_refs[c][rows(k0, k1), q0:q1]
            m_blk = jnp.max(s, axis=0, keepdims=True)
            if j == 0:
                m_new = m_blk
            else:
                m_old = m_ref[i, c, :, q0:q1]
                m_new = jnp.maximum(m_old, m_blk)
            pv = _dot(vt_ref[j, :, k0:k1], jnp.exp2(s - m_new).astype(BF16))
            if j == 0:
                acc_ref[i, c, :, q0:q1] = pv
            else:
                acc_ref[i, c, :, q0:q1] = (acc_ref[i, c, :, q0:q1]
                                           * jnp.exp2(m_old - m_new) + pv)
            m_ref[i, c, :, q0:q1] = m_new

    steps = []
    for j in range(nb):
        for i in range(j, nb):
            steps.append((i, j, "diag" if i == j else "near" if i == j + 1 else "far"))

    scores(0, steps[0])
    for t, step in enumerate(steps):
        scores(1, step)
        add_bias(0, step[2])
        softmax_pv(0, step)
        if t + 1 < len(steps):
            scores(0, steps[t + 1])
        add_bias(1, step[2])
        softmax_pv(1, step)

    lp = lam_ref[...]
    lam = (jnp.exp(jnp.sum(lp[0:1] * lp[1:2], axis=-1, keepdims=True))
           - jnp.exp(jnp.sum(lp[2:3] * lp[3:4], axis=-1, keepdims=True))
           + lambda_init)
    for i in range(nb):
        a0 = acc_ref[i, 0]
        a1 = acc_ref[i, 1]
        o = a0[0:dv] / a0[dv:dv + 1] - lam * (a1[0:dv] / a1[dv:dv + 1])
        o = o * lax.rsqrt(jnp.mean(o * o, axis=0, keepdims=True) + EPS)
        o = o * sw_ref[...] * (1.0 - lambda_init)
        o_ref[i * tb:(i + 1) * tb, :] = o.T.astype(BF16)


def _attn(q, k, v, tiles, lam_p, sw, *, batch, seq, lambda_init):
    nb = seq // ATTN_BLOCK
    seq_head = pl.BlockSpec((seq, DA_V_DIM), lambda b, h: (b, h))
    acc_rows = DA_V_DIM + ATTN_ONES_ROWS
    return pl.pallas_call(
        functools.partial(_attn_kernel, lambda_init=lambda_init, seq=seq),
        out_shape=jax.ShapeDtypeStruct((batch * seq, DA_WIDTH), BF16),
        grid=(batch, DA_HEADS),
        in_specs=[seq_head, seq_head, seq_head,
                  pl.BlockSpec((None, 2, BIAS_TILE, BIAS_TILE),
                               lambda b, h: (h, 0, 0, 0)),
                  _resident((4, DA_HEAD_DIM)), _resident((DA_V_DIM, ATTN_BLOCK))],
        out_specs=seq_head,
        scratch_shapes=[pltpu.VMEM((nb, acc_rows, ATTN_BLOCK), BF16),
                        pltpu.VMEM((nb, 2, DA_V_DIM, ATTN_BLOCK), BF16),
                        pltpu.VMEM((ATTN_BLOCK, ATTN_BLOCK), F32),
                        pltpu.VMEM((ATTN_BLOCK, ATTN_BLOCK), F32),
                        pltpu.VMEM((nb, 2, 1, ATTN_BLOCK), F32),
                        pltpu.VMEM((nb, 2, acc_rows, ATTN_BLOCK), F32)],
        compiler_params=_params("parallel", "parallel"),
        name="diff_attn",
    )(q, k, v, tiles, lam_p, sw)


def _mem_kv_kernel(mem_ref, nw_ref, wk_ref, wv_ref, k_ref, v_ref):
    mn = _rms(mem_ref[...], nw_ref[...])
    k_ref[...] = _dot(mn, wk_ref[...]).astype(BF16)
    v_ref[...] = _dot(mn, wv_ref[...]).astype(BF16)


def _mem_kv(mem, nw, wk, wv):
    t = mem.shape[0]
    row = pl.BlockSpec((MEM_LEN, D_MODEL), lambda i: (i, 0))
    return pl.pallas_call(
        _mem_kv_kernel,
        out_shape=(jax.ShapeDtypeStruct((t, D_MODEL), BF16),
                   jax.ShapeDtypeStruct((t, D_MODEL), BF16)),
        grid=(t // MEM_LEN,),
        in_specs=[row, _resident((1, D_MODEL)), _resident((D_MODEL, D_MODEL)),
                  _resident((D_MODEL, D_MODEL))],
        out_specs=(row, row),
        compiler_params=_params("parallel"),
        name="mem_kv",
    )(mem, nw, wk, wv)


def _mix_out_cross_kernel(x_ref, ys_ref, xs_ref, z_ref, ya_ref, dskip_ref, snw_ref,
                          ws_ref, wa_ref, nw_ref, wq_ref, k_ref, v_ref, wo_ref, o_ref):
    y_ssd = _ssd_gate_norm(ys_ref[...], xs_ref[...], z_ref[...], dskip_ref[...],
                           snw_ref[...])
    x = (x_ref[...] + _dot(ya_ref[...].astype(F32), wa_ref[...])
         + _dot(y_ssd, ws_ref[...]))
    h = _rms(x, nw_ref[...])
    q = (_dot(h, wq_ref[...]) * (CROSS_HEAD_DIM ** -0.5)).astype(BF16)
    outs = []
    for hh in range(CROSS_HEADS):
        sl = slice(hh * CROSS_HEAD_DIM, (hh + 1) * CROSS_HEAD_DIM)
        s = _dot_nt(q[:, sl], k_ref[:, sl])
        p = jnp.exp(s - jnp.max(s, axis=-1, keepdims=True))
        o = _dot(p.astype(BF16), v_ref[:, sl])
        outs.append(o / jnp.sum(p, axis=-1, keepdims=True))
    o_ref[...] = x + _dot(jnp.concatenate(outs, axis=-1), wo_ref[...])


def _mix_out_cross(x1, y_scan, xbc, z, y_attn, dskip, snw, w_out, nw, wq, k, v, wo,
                   *, seq):
    t = x1.shape[0]
    tiles_per_batch = seq // ROW_TILE
    row = pl.BlockSpec((ROW_TILE, D_MODEL), lambda i: (i, 0))
    mem = pl.BlockSpec((MEM_LEN, D_MODEL), lambda i: (i // tiles_per_batch, 0))
    assert SSD_WIDTH == D_MODEL
    assert SSD_WIDTH == DA_WIDTH

    def w_half(j):
        return pl.BlockSpec((SSD_WIDTH, D_MODEL), lambda i: (j, 0),
                            pipeline_mode=pl.Buffered(1))

    return pl.pallas_call(
        _mix_out_cross_kernel,
        out_shape=jax.ShapeDtypeStruct((t, D_MODEL), F32),
        grid=(t // ROW_TILE,),
        in_specs=[row, row, row, row, row, _resident((1, SSD_WIDTH)),
                  _resident((1, SSD_WIDTH)), w_half(0), w_half(1),
                  _resident((1, D_MODEL)), _resident((D_MODEL, D_MODEL)), mem, mem,
                  _resident((D_MODEL, D_MODEL))],
        out_specs=row,
        compiler_params=_params("parallel"),
        name="mix_out_cross",
    )(x1, y_scan, xbc, z, y_attn, dskip, snw, w_out, w_out, nw, wq, k, v, wo)


def kernel(x, mem, norm_ffn1_w, ffn1_w_in, ffn1_w_out, norm_mix_w, w_in_mix, conv_w, conv_b, dt_bias, a_log, d_skip, ssd_norm_w, lambda_q1, lambda_k1, lambda_q2, lambda_k2, subln_w, rel_bias, w_out_mix, norm_cross_w, norm_mem_w, w_cq, w_ck, w_cv, w_co, norm_ffn2_w, ffn2_w_in, ffn2_w_out, norm_final_w):
    batch, seq, _ = x.shape
    depth = ffn1_w_in.shape[0]
    t = batch * seq
    xf = x.reshape(t, D_MODEL)
    memf = mem.reshape(batch * MEM_LEN, D_MODEL)
    fw = norm_final_w.reshape(1, D_MODEL)
    tiles = _rel_bias_tiles(rel_bias)

    def vec(p):
        return p.reshape(1, -1)

    def pad_lanes(p):
        return jnp.pad(p.reshape(1, -1), ((0, 0), (0, DT_PAD - p.shape[-1])))

    for l in range(depth):
        lambda_init = 0.8 - 0.6 * math.exp(-0.3 * l)
        xf = _ffn(xf, vec(norm_ffn1_w[l]), ffn1_w_in[l], ffn1_w_out[l], fw,
                  final=False, name="ffn1")

        z, xbc, cum, src, q, k, v = _in_proj(
            xf, vec(norm_mix_w[l]), w_in_mix[l].T,
            conv_w[l], vec(conv_b[l]), pad_lanes(dt_bias[l]), pad_lanes(a_log[l]),
            seq=seq)

        y_scan = _ssd(xbc, cum, src, batch=batch, seq=seq)

        lam_p = jnp.stack([lambda_q1[l], lambda_k1[l], lambda_q2[l], lambda_k2[l]])
        sw = jnp.broadcast_to(subln_w[l].reshape(DA_V_DIM, 1), (DA_V_DIM, ATTN_BLOCK))
        y_attn = _attn(q, k, v, tiles, lam_p, sw,
                       batch=batch, seq=seq, lambda_init=lambda_init)

        mk, mv = _mem_kv(memf, vec(norm_mem_w[l]), w_ck[l], w_cv[l])
        xf = _mix_out_cross(xf, y_scan, xbc, z, y_attn,
                            vec(jnp.repeat(d_skip[l], SSD_HEAD_DIM)),
                            vec(ssd_norm_w[l]), w_out_mix[l],
                            vec(norm_cross_w[l]), w_cq[l], mk, mv, w_co[l], seq=seq)

        xf = _ffn(xf, vec(norm_ffn2_w[l]), ffn2_w_in[l], ffn2_w_out[l], fw,
                  final=(l == depth - 1), name="ffn2")
    return xf.reshape(batch, seq, D_MODEL)
```

```python
import functools
import math

import jax
import jax.numpy as jnp
from jax import lax
from jax.experimental import pallas as pl
from jax.experimental.pallas import tpu as pltpu

F32 = jnp.float32
BF16 = jnp.bfloat16

D_MODEL = 1024
MEM_LEN = 256
EPS = 1e-6

SSD_HEADS = 16
SSD_HEAD_DIM = 64
SSD_WIDTH = SSD_HEADS * SSD_HEAD_DIM
SSD_GROUPS = 4
SSD_STATE = 128
CONV_WIDTH = 4
SSD_CHUNK = 128
CONV_CH = SSD_WIDTH + 2 * SSD_GROUPS * SSD_STATE
HEADS_PER_GROUP = SSD_HEADS // SSD_GROUPS

DA_HEADS = 8
DA_HEAD_DIM = 64
DA_V_DIM = 2 * DA_HEAD_DIM
DA_WIDTH = DA_HEADS * DA_V_DIM

NUM_BUCKETS = 32
MAX_DISTANCE = 128

CROSS_HEADS = 4
CROSS_HEAD_DIM = D_MODEL // CROSS_HEADS

D_FF = 2816
LOG2_E = math.log2(math.e)

LANES = 128
SUBLANES = 8
MXU_DIM = 256
VMEM_LIMIT_BYTES = 56 * 1024 * 1024

ROW_TILE = 512
FF_CHUNK = MXU_DIM
ATTN_BLOCK = 512
ATTN_ONES_ROWS = 16
BIAS_TILE = MAX_DISTANCE
DT_PAD = LANES
CONV_HALO = SUBLANES
SSD_CHUNKS_PER_STEP = 4


def _resident(shape):
    nd = len(shape)
    return pl.BlockSpec(shape, lambda *_: (0,) * nd, pipeline_mode=pl.Buffered(1))


def _params(*sem):
    return pltpu.CompilerParams(dimension_semantics=sem,
                                vmem_limit_bytes=VMEM_LIMIT_BYTES)


def _rms(x, w):
    ms = jnp.mean(x * x, axis=-1, keepdims=True)
    return x * lax.rsqrt(ms + EPS) * w


def _silu(x):
    h = 0.5 * x
    return h + h * jnp.tanh(h)


def _dot(a, b):
    return jnp.dot(a, b, preferred_element_type=F32)


def _dot_nt(a, b):
    return lax.dot_general(a, b, (((1,), (1,)), ((), ())),
                           preferred_element_type=F32)


def _ffn_kernel(x_ref, nw_ref, win_ref, wout_ref, fw_ref, o_ref, *, final):
    x = x_ref[...]
    h = _rms(x, nw_ref[...])
    acc = None
    for c in range(D_FF // FF_CHUNK):
        lo = c * FF_CHUNK
        g = _dot(h, win_ref[:, lo:lo + FF_CHUNK])
        u = _dot(h, win_ref[:, D_FF + lo:D_FF + lo + FF_CHUNK])
        a = _silu(g) * u
        d = _dot(a, wout_ref[lo:lo + FF_CHUNK, :])
        acc = d if acc is None else acc + d
    y = x + 0.5 * acc
    if final:
        y = _rms(y, fw_ref[...])
    o_ref[...] = y


def _ffn(x, nw, win, wout, fw, *, final, name):
    t = x.shape[0]
    row = pl.BlockSpec((ROW_TILE, D_MODEL), lambda i: (i, 0))
    return pl.pallas_call(
        functools.partial(_ffn_kernel, final=final),
        out_shape=jax.ShapeDtypeStruct((t, D_MODEL), F32),
        grid=(t // ROW_TILE,),
        in_specs=[row, _resident((1, D_MODEL)), _resident((D_MODEL, 2 * D_FF)),
                  _resident((D_FF, D_MODEL)), _resident((1, D_MODEL))],
        out_specs=row,
        compiler_params=_params("parallel"),
        name=name,
    )(x, nw, win, wout, fw)


def _in_proj_kernel(x_ref, nw_ref, wt_ref, cw_ref, cb_ref, dtb_ref, alog_ref,
                    z_ref, xbc_ref, cum_ref, src_ref, qt_ref, k_ref, vt_ref, ext_ref,
                    *, tiles_per_seq):
    i = pl.program_id(0)
    tm = ROW_TILE
    h = _rms(x_ref[...], nw_ref[...])
    o_xbc = SSD_WIDTH
    o_dt = o_xbc + CONV_CH
    o_q = o_dt + SSD_HEADS
    o_k = o_q + DA_WIDTH
    o_v = o_k + DA_WIDTH

    def proj(lo, width):
        return _dot_nt(h, wt_ref[lo:lo + width, :])

    @pl.when(i % tiles_per_seq == 0)
    def _():
        ext_ref[:, 0:CONV_HALO, :] = jnp.zeros((CONV_CH // LANES, CONV_HALO, LANES), F32)

    x_dt = proj(o_dt, DT_PAD) + dtb_ref[...]
    dt = jnp.maximum(x_dt, 0.0) + jnp.log1p(jnp.exp(-jnp.abs(x_dt)))
    log2_dt = jnp.log2(dt)
    la = dt * (-LOG2_E * jnp.exp(alog_ref[...]))
    la_hi = la.astype(BF16)
    la_mid = (la - la_hi.astype(F32)).astype(BF16)
    la_lo = (la - la_hi.astype(F32) - la_mid.astype(F32)).astype(BF16)
    row_i = lax.broadcasted_iota(jnp.int32, (SSD_CHUNK, SSD_CHUNK), 0)
    col_i = lax.broadcasted_iota(jnp.int32, (SSD_CHUNK, SSD_CHUNK), 1)
    tril = (col_i <= row_i).astype(BF16)

    wide = ((z_ref, 0, SSD_WIDTH, 1.0, False),
            (qt_ref, o_q, DA_WIDTH, DA_HEAD_DIM ** -0.5 * LOG2_E, True),
            (k_ref, o_k, DA_WIDTH, 1.0, False), (vt_ref, o_v, DA_WIDTH, 1.0, True))
    n_chunks = tm // SSD_CHUNK
    assert n_chunks == len(wide)
    u = proj(o_xbc, CONV_CH)
    for t in range(CONV_CH // LANES):
        ext_ref[t, CONV_HALO:CONV_HALO + tm, :] = u[:, t * LANES:(t + 1) * LANES]
    for r in range(n_chunks):
        lo = r * SSD_CHUNK
        rows = slice(lo, lo + SSD_CHUNK)
        cum = (_dot(tril, la_hi[rows]) + _dot(tril, la_mid[rows])
               + _dot(tril, la_lo[rows]))
        cum_ref[rows, :] = cum
        src_ref[rows, :] = cum - log2_dt[rows]
        base = CONV_HALO + lo
        acts = []
        for t in range(CONV_CH // LANES):
            ln = slice(t * LANES, (t + 1) * LANES)
            acc = cb_ref[:, ln]
            for j in range(CONV_WIDTH):
                start = base - (CONV_WIDTH - 1) + j
                acc = acc + cw_ref[j:j + 1, ln] * ext_ref[t, start:start + SSD_CHUNK, :]
            acts.append(_silu(acc))
        xbc_ref[rows, :] = jnp.concatenate(acts, axis=-1).astype(BF16)
        o_ref, w_lo, w_width, scale, feature_major = wide[r]
        if feature_major:
            out = _dot_nt(wt_ref[w_lo:w_lo + w_width, :], h)
        else:
            out = proj(w_lo, w_width)
        o_ref[...] = (out if scale == 1.0 else out * scale).astype(BF16)
    ext_ref[:, 0:CONV_HALO, :] = ext_ref[:, tm:tm + CONV_HALO, :]


def _in_proj(x1, nw, w_t, cw, cb, dtb, alog, *, seq):
    t = x1.shape[0]
    in_width = w_t.shape[0]
    assert (SSD_WIDTH + CONV_CH + SSD_HEADS) % SUBLANES == 0
    assert SSD_WIDTH + CONV_CH + DT_PAD <= in_width

    def row(n):
        return pl.BlockSpec((ROW_TILE, n), lambda i: (i, 0))

    def col(n):
        return pl.BlockSpec((n, ROW_TILE), lambda i: (0, i))

    return pl.pallas_call(
        functools.partial(_in_proj_kernel, tiles_per_seq=seq // ROW_TILE),
        out_shape=(jax.ShapeDtypeStruct((t, SSD_WIDTH), BF16),
                   jax.ShapeDtypeStruct((t, CONV_CH), BF16),
                   jax.ShapeDtypeStruct((t, DT_PAD), F32),
                   jax.ShapeDtypeStruct((t, DT_PAD), F32),
                   jax.ShapeDtypeStruct((DA_WIDTH, t), BF16),
                   jax.ShapeDtypeStruct((t, DA_WIDTH), BF16),
                   jax.ShapeDtypeStruct((DA_WIDTH, t), BF16)),
        grid=(t // ROW_TILE,),
        in_specs=[row(D_MODEL), _resident((1, D_MODEL)),
                  _resident((in_width, D_MODEL)),
                  _resident((CONV_WIDTH, CONV_CH)), _resident((1, CONV_CH)),
                  _resident((1, DT_PAD)), _resident((1, DT_PAD))],
        out_specs=(row(SSD_WIDTH), row(CONV_CH), row(DT_PAD), row(DT_PAD),
                   col(DA_WIDTH), row(DA_WIDTH), col(DA_WIDTH)),
        scratch_shapes=[pltpu.VMEM((CONV_CH // LANES, CONV_HALO + ROW_TILE + CONV_HALO,
                                    LANES), F32)],
        compiler_params=_params("arbitrary"),
        name="in_proj",
    )(x1, nw, w_t, cw, cb, dtb, alog)


def _ssd_chunk(xbc, cum_col, src_col, state_ref):
    L = SSD_CHUNK
    hp = SSD_HEAD_DIM
    xs_b = xbc[:, :SSD_WIDTH]
    row_i = lax.broadcasted_iota(jnp.int32, (L, L), 0)
    col_i = lax.broadcasted_iota(jnp.int32, (L, L), 1)
    causal = col_i <= row_i
    first = col_i < hp
    keep = (first.astype(BF16), 1.0 - first.astype(BF16))
    src_row = src_col.T

    ys = []
    for g in range(SSD_GROUPS):
        b_g = xbc[:, SSD_WIDTH + g * SSD_STATE:SSD_WIDTH + (g + 1) * SSD_STATE]
        c_g = xbc[:, SSD_WIDTH + (SSD_GROUPS + g) * SSD_STATE:
                  SSD_WIDTH + (SSD_GROUPS + g + 1) * SSD_STATE]
        cb = _dot_nt(c_g, b_g)
        bt_g = b_g.astype(F32).T
        for pr in range(HEADS_PER_GROUP // 2):
            pair = g * (HEADS_PER_GROUP // 2) + pr
            x_pair = xs_b[:, pair * 2 * hp:(pair + 1) * 2 * hp]
            x_bd = jnp.concatenate([x_pair * keep[0], x_pair * keep[1]], axis=0)
            cols, w_parts, b_parts = [], [], []
            for t in range(2):
                h = 2 * pair + t
                col_b = jnp.broadcast_to(cum_col[:, h:h + 1], (L, L))
                row_b = jnp.broadcast_to(src_row[h:h + 1, :], (L, L))
                dec = jnp.exp2(jnp.where(causal, col_b - row_b, -jnp.inf))
                w_parts.append((cb * dec).astype(BF16))
                b_parts.append((bt_g * dec[L - 1:L, :]).astype(BF16))
                cols.append(col_b)
            from_start = jnp.exp2(jnp.where(first, cols[0], cols[1]))
            s_prev = state_ref[pair]
            ys.append(_dot(jnp.concatenate(w_parts, axis=1), x_bd)
                      + _dot(c_g, s_prev.astype(BF16)) * from_start)
            new = _dot(jnp.concatenate(b_parts, axis=1), x_bd)
            state_ref[pair] = s_prev * from_start[L - 1:L, :] + new
    return jnp.concatenate(ys, axis=-1).astype(BF16)


def _ssd_kernel(xbc_ref, cum_ref, src_ref, o_ref, state_ref):
    @pl.when(pl.program_id(1) == 0)
    def _():
        state_ref[...] = jnp.zeros_like(state_ref)

    for r in range(SSD_CHUNKS_PER_STEP):
        rows = slice(r * SSD_CHUNK, (r + 1) * SSD_CHUNK)
        o_ref[rows, :] = _ssd_chunk(xbc_ref[rows, :], cum_ref[rows, :], src_ref[rows, :],
                                    state_ref)


def _ssd_gate_norm(y, xs, z, dskip, nw):
    y = y.astype(F32) + dskip * xs.astype(F32)
    y = y * _silu(z.astype(F32))
    gw = SSD_WIDTH // SSD_GROUPS
    outs = []
    for g in range(SSD_GROUPS):
        yg = y[:, g * gw:(g + 1) * gw]
        ms = jnp.mean(yg * yg, axis=-1, keepdims=True)
        outs.append(yg * lax.rsqrt(ms + EPS))
    return jnp.concatenate(outs, axis=-1) * nw


def _ssd(xbc, cum, src, *, batch, seq):
    step_rows = SSD_CHUNKS_PER_STEP * SSD_CHUNK
    ns = seq // step_rows

    def row(n):
        return pl.BlockSpec((step_rows, n), lambda b, c: (b * ns + c, 0))

    return pl.pallas_call(
        _ssd_kernel,
        out_shape=jax.ShapeDtypeStruct((batch * seq, SSD_WIDTH), BF16),
        grid=(batch, ns),
        in_specs=[row(CONV_CH), row(DT_PAD), row(DT_PAD)],
        out_specs=row(SSD_WIDTH),
        scratch_shapes=[pltpu.VMEM((SSD_HEADS // 2, SSD_STATE, 2 * SSD_HEAD_DIM), F32)],
        compiler_params=_params("parallel", "arbitrary"),
        name="ssd",
    )(xbc, cum, src)


def _rel_bias_kernel(rb_ref, o_ref):
    k_i = lax.broadcasted_iota(jnp.int32, (BIAS_TILE, BIAS_TILE), 0)
    q_i = lax.broadcasted_iota(jnp.int32, (BIAS_TILE, BIAS_TILE), 1)
    max_exact = NUM_BUCKETS // 2
    for d in range(2):
        n = q_i - k_i + d * BIAS_TILE
        nf = jnp.maximum(n, 1).astype(F32)
        large = max_exact + (jnp.log(nf / max_exact) / math.log(MAX_DISTANCE / max_exact)
                             * (NUM_BUCKETS - max_exact)).astype(jnp.int32)
        large = jnp.minimum(large, NUM_BUCKETS - 1)
        bucket = jnp.where(n < max_exact, n, large)
        hits = [bucket == b for b in range(NUM_BUCKETS)]
        for h in range(DA_HEADS):
            tile = jnp.zeros((BIAS_TILE, BIAS_TILE), F32)
            for b in range(NUM_BUCKETS):
                tile = jnp.where(hits[b], rb_ref[b, h], tile)
            tile = (tile - rb_ref[NUM_BUCKETS - 1, h]) * LOG2_E
            o_ref[h, d] = jnp.where(n >= 0, tile, -jnp.inf)


def _rel_bias_tiles(rel_bias):
    return pl.pallas_call(
        _rel_bias_kernel,
        out_shape=jax.ShapeDtypeStruct((DA_HEADS, 2, BIAS_TILE, BIAS_TILE), F32),
        in_specs=[pl.BlockSpec(memory_space=pltpu.SMEM)],
        out_specs=pl.BlockSpec(memory_space=pltpu.VMEM),
        compiler_params=_params(),
        name="rel_bias",
    )(rel_bias)


def _attn_kernel(qt_ref, k_ref, vt_ref, bt_ref, lam_ref, sw_ref, o_ref,
                 s0_ref, s1_ref, m_ref, acc_ref, *, lambda_init, seq):
    tb = ATTN_BLOCK
    dh = DA_HEAD_DIM
    dv = DA_V_DIM
    nb = seq // tb

    def q_operand(i, c, q0, q1):
        blk = qt_ref[c * dh:(c + 1) * dh, i * tb + q0:i * tb + q1]
        pad = jnp.zeros_like(blk)
        return jnp.concatenate([blk, pad] if c == 0 else [pad, blk], axis=0)

    def v_operand(j, k0, k1):
        ones = jnp.ones((ATTN_ONES_ROWS, k1 - k0), BF16)
        return jnp.concatenate([vt_ref[:, j * tb + k0:j * tb + k1], ones], axis=0)

    s_refs = (s0_ref, s1_ref)
    half = tb // 2
    bt = BIAS_TILE
    nsub = tb // bt
    full = ((0, tb, 0, tb),)
    diag_parts = ((0, half, 0, half), (0, tb, half, tb))

    zero = jnp.minimum(pl.program_id(0), 0)

    def rows(lo, hi):
        return pl.ds(pl.multiple_of(zero + lo, BIAS_TILE), hi - lo)

    def scores(c, step):
        i, j, kind = step
        k0 = j * tb
        if kind != "diag":
            s_refs[c][rows(0, tb), :] = _dot(k_ref[k0:k0 + tb, :],
                                             q_operand(i, c, 0, tb))
        else:
            s_refs[c][rows(0, half), :] = _dot(k_ref[k0:k0 + half, :],
                                               q_operand(i, c, 0, tb))
            s_refs[c][rows(half, tb), half:tb] = _dot(k_ref[k0 + half:k0 + tb, :],
                                                      q_operand(i, c, half, tb))

    def add_bias(c, kind):
        s_ref = s_refs[c]
        if kind == "near":
            near = (rows(tb - bt, tb), slice(0, bt))
            s_ref[near] = s_ref[near] + bt_ref[1]
        elif kind == "diag":
            for a in range(nsub):
                on = (rows(a * bt, (a + 1) * bt), slice(a * bt, (a + 1) * bt))
                s_ref[on] = s_ref[on] + bt_ref[0]
                if a + 1 < nsub:
                    off = (rows(a * bt, (a + 1) * bt), slice((a + 1) * bt, (a + 2) * bt))
                    s_ref[off] = s_ref[off] + bt_ref[1]
            for k0, k1, q0, q1 in diag_parts:
                for a in range(k0 // bt, k1 // bt):
                    for b in range(q0 // bt, q1 // bt):
                        if a > b:
                            s_ref[rows(a * bt, (a + 1) * bt), b * bt:(b + 1) * bt] = (
                                jnp.full((bt, bt), -jnp.inf, F32))

    def softmax_pv(c, step):
        i, j, kind = step
        for k0, k1, q0, q1 in (diag_parts if kind == "diag" else full):
            s = s_refs[c][rows(k0, k1), q0:q1]
            m_blk = jnp.max(s, axis=0, keepdims=True)
            if j == 0:
                m_new = m_blk
            else:
                m_old = m_ref[i, c, :, q0:q1]
                m_new = jnp.maximum(m_old, m_blk)
            pv = _dot(v_operand(j, k0, k1), jnp.exp2(s - m_new).astype(BF16))
            if j == 0:
                acc_ref[i, c, :, q0:q1] = pv
            else:
                acc_ref[i, c, :, q0:q1] = (acc_ref[i, c, :, q0:q1]
                                           * jnp.exp2(m_old - m_new) + pv)
            m_ref[i, c, :, q0:q1] = m_new

    steps = []
    for j in range(nb):
        for i in range(j, nb):
            steps.append((i, j, "diag" if i == j else "near" if i == j + 1 else "far"))

    scores(0, steps[0])
    for t, step in enumerate(steps):
        scores(1, step)
        add_bias(0, step[2])
        softmax_pv(0, step)
        if t + 1 < len(steps):
            scores(0, steps[t + 1])
        add_bias(1, step[2])
        softmax_pv(1, step)

    lp = lam_ref[...]
    lam = (jnp.exp(jnp.sum(lp[0:1] * lp[1:2], axis=-1, keepdims=True))
           - jnp.exp(jnp.sum(lp[2:3] * lp[3:4], axis=-1, keepdims=True))
           + lambda_init)
    for i in range(nb):
        a0 = acc_ref[i, 0]
        a1 = acc_ref[i, 1]
        o = a0[0:dv] / a0[dv:dv + 1] - lam * (a1[0:dv] / a1[dv:dv + 1])
        o = o * lax.rsqrt(jnp.mean(o * o, axis=0, keepdims=True) + EPS)
        o = o * sw_ref[...] * (1.0 - lambda_init)
        o_ref[i * tb:(i + 1) * tb, :] = o.T.astype(BF16)


def _attn(q_t, k, v_t, tiles, lam_p, sw, *, batch, seq, lambda_init):
    nb = seq // ATTN_BLOCK
    seq_head = pl.BlockSpec((seq, DA_V_DIM), lambda b, h: (b, h))
    head_seq = pl.BlockSpec((DA_V_DIM, seq), lambda b, h: (h, b))
    acc_rows = DA_V_DIM + ATTN_ONES_ROWS
    return pl.pallas_call(
        functools.partial(_attn_kernel, lambda_init=lambda_init, seq=seq),
        out_shape=jax.ShapeDtypeStruct((batch * seq, DA_WIDTH), BF16),
        grid=(batch, DA_HEADS),
        in_specs=[head_seq, seq_head, head_seq,
                  pl.BlockSpec((None, 2, BIAS_TILE, BIAS_TILE),
                               lambda b, h: (h, 0, 0, 0)),
                  _resident((4, DA_HEAD_DIM)), _resident((DA_V_DIM, ATTN_BLOCK))],
        out_specs=seq_head,
        scratch_shapes=[pltpu.VMEM((ATTN_BLOCK, ATTN_BLOCK), F32),
                        pltpu.VMEM((ATTN_BLOCK, ATTN_BLOCK), F32),
                        pltpu.VMEM((nb, 2, 1, ATTN_BLOCK), F32),
                        pltpu.VMEM((nb, 2, acc_rows, ATTN_BLOCK), F32)],
        compiler_params=_params("parallel", "parallel"),
        name="diff_attn",
    )(q_t, k, v_t, tiles, lam_p, sw)


def _mem_kv_kernel(mem_ref, nw_ref, wk_ref, wv_ref, k_ref, v_ref):
    mn = _rms(mem_ref[...], nw_ref[...])
    k_ref[...] = _dot(mn, wk_ref[...]).astype(BF16)
    v_ref[...] = _dot(mn, wv_ref[...]).astype(BF16)


def _mem_kv(mem, nw, wk, wv):
    t = mem.shape[0]
    row = pl.BlockSpec((MEM_LEN, D_MODEL), lambda i: (i, 0))
    return pl.pallas_call(
        _mem_kv_kernel,
        out_shape=(jax.ShapeDtypeStruct((t, D_MODEL), BF16),
                   jax.ShapeDtypeStruct((t, D_MODEL), BF16)),
        grid=(t // MEM_LEN,),
        in_specs=[row, _resident((1, D_MODEL)), _resident((D_MODEL, D_MODEL)),
                  _resident((D_MODEL, D_MODEL))],
        out_specs=(row, row),
        compiler_params=_params("parallel"),
        name="mem_kv",
    )(mem, nw, wk, wv)


def _mix_out_cross_kernel(x_ref, ys_ref, xs_ref, z_ref, ya_ref, dskip_ref, snw_ref,
                          ws_ref, wa_ref, nw_ref, wq_ref, k_ref, v_ref, wo_ref, o_ref):
    y_ssd = _ssd_gate_norm(ys_ref[...], xs_ref[...], z_ref[...], dskip_ref[...],
                           snw_ref[...])
    x = (x_ref[...] + _dot(ya_ref[...].astype(F32), wa_ref[...])
         + _dot(y_ssd, ws_ref[...]))
    h = _rms(x, nw_ref[...])
    q = (_dot(h, wq_ref[...]) * (CROSS_HEAD_DIM ** -0.5)).astype(BF16)
    outs = []
    for hh in range(CROSS_HEADS):
        sl = slice(hh * CROSS_HEAD_DIM, (hh + 1) * CROSS_HEAD_DIM)
        s = _dot_nt(q[:, sl], k_ref[:, sl])
        p = jnp.exp(s - jnp.max(s, axis=-1, keepdims=True))
        o = _dot(p.astype(BF16), v_ref[:, sl])
        outs.append(o / jnp.sum(p, axis=-1, keepdims=True))
    o_ref[...] = x + _dot(jnp.concatenate(outs, axis=-1), wo_ref[...])


def _mix_out_cross(x1, y_scan, xbc, z, y_attn, dskip, snw, w_out, nw, wq, k, v, wo,
                   *, seq):
    t = x1.shape[0]
    tiles_per_batch = seq // ROW_TILE
    row = pl.BlockSpec((ROW_TILE, D_MODEL), lambda i: (i, 0))
    mem = pl.BlockSpec((MEM_LEN, D_MODEL), lambda i: (i // tiles_per_batch, 0))
    assert SSD_WIDTH == D_MODEL
    assert SSD_WIDTH == DA_WIDTH

    def w_half(j):
        return pl.BlockSpec((SSD_WIDTH, D_MODEL), lambda i: (j, 0),
                            pipeline_mode=pl.Buffered(1))

    return pl.pallas_call(
        _mix_out_cross_kernel,
        out_shape=jax.ShapeDtypeStruct((t, D_MODEL), F32),
        grid=(t // ROW_TILE,),
        in_specs=[row, row, row, row, row, _resident((1, SSD_WIDTH)),
                  _resident((1, SSD_WIDTH)), w_half(0), w_half(1),
                  _resident((1, D_MODEL)), _resident((D_MODEL, D_MODEL)), mem, mem,
                  _resident((D_MODEL, D_MODEL))],
        out_specs=row,
        compiler_params=_params("parallel"),
        name="mix_out_cross",
    )(x1, y_scan, xbc, z, y_attn, dskip, snw, w_out, w_out, nw, wq, k, v, wo)


def kernel(x, mem, norm_ffn1_w, ffn1_w_in, ffn1_w_out, norm_mix_w, w_in_mix, conv_w, conv_b, dt_bias, a_log, d_skip, ssd_norm_w, lambda_q1, lambda_k1, lambda_q2, lambda_k2, subln_w, rel_bias, w_out_mix, norm_cross_w, norm_mem_w, w_cq, w_ck, w_cv, w_co, norm_ffn2_w, ffn2_w_in, ffn2_w_out, norm_final_w):
    batch, seq, _ = x.shape
    depth = ffn1_w_in.shape[0]
    t = batch * seq
    xf = x.reshape(t, D_MODEL)
    memf = mem.reshape(batch * MEM_LEN, D_MODEL)
    fw = norm_final_w.reshape(1, D_MODEL)
    tiles = _rel_bias_tiles(rel_bias)

    def vec(p):
        return p.reshape(1, -1)

    def pad_lanes(p):
        return jnp.pad(p.reshape(1, -1), ((0, 0), (0, DT_PAD - p.shape[-1])))

    for l in range(depth):
        lambda_init = 0.8 - 0.6 * math.exp(-0.3 * l)
        xf = _ffn(xf, vec(norm_ffn1_w[l]), ffn1_w_in[l], ffn1_w_out[l], fw,
                  final=False, name="ffn1")

        z, xbc, cum, src, q_t, k, v_t = _in_proj(
            xf, vec(norm_mix_w[l]), w_in_mix[l].T,
            conv_w[l], vec(conv_b[l]), pad_lanes(dt_bias[l]), pad_lanes(a_log[l]),
            seq=seq)

        y_scan = _ssd(xbc, cum, src, batch=batch, seq=seq)

        lam_p = jnp.stack([lambda_q1[l], lambda_k1[l], lambda_q2[l], lambda_k2[l]])
        sw = jnp.broadcast_to(subln_w[l].reshape(DA_V_DIM, 1), (DA_V_DIM, ATTN_BLOCK))
        y_attn = _attn(q_t, k, v_t, tiles, lam_p, sw,
                       batch=batch, seq=seq, lambda_init=lambda_init)

        mk, mv = _mem_kv(memf, vec(norm_mem_w[l]), w_ck[l], w_cv[l])
        xf = _mix_out_cross(xf, y_scan, xbc, z, y_attn,
                            vec(jnp.repeat(d_skip[l], SSD_HEAD_DIM)),
                            vec(ssd_norm_w[l]), w_out_mix[l],
                            vec(norm_cross_w[l]), w_cq[l], mk, mv, w_co[l], seq=seq)

        xf = _ffn(xf, vec(norm_ffn2_w[l]), ffn2_w_in[l], ffn2_w_out[l], fw,
                  final=(l == depth - 1), name="ffn2")
    return xf.reshape(batch, seq, D_MODEL)
```

```python
import functools
import math

import jax
import jax.numpy as jnp
from jax import lax
from jax.experimental import pallas as pl
from jax.experimental.pallas import tpu as pltpu

F32 = jnp.float32
BF16 = jnp.bfloat16

D_MODEL = 1024
MEM_LEN = 256
EPS = 1e-6

SSD_HEADS = 16
SSD_HEAD_DIM = 64
SSD_WIDTH = SSD_HEADS * SSD_HEAD_DIM
SSD_GROUPS = 4
SSD_STATE = 128
CONV_WIDTH = 4
SSD_CHUNK = 128
CONV_CH = SSD_WIDTH + 2 * SSD_GROUPS * SSD_STATE
HEADS_PER_GROUP = SSD_HEADS // SSD_GROUPS

DA_HEADS = 8
DA_HEAD_DIM = 64
DA_V_DIM = 2 * DA_HEAD_DIM
DA_WIDTH = DA_HEADS * DA_V_DIM

NUM_BUCKETS = 32
MAX_DISTANCE = 128

CROSS_HEADS = 4
CROSS_HEAD_DIM = D_MODEL // CROSS_HEADS

D_FF = 2816
LOG2_E = math.log2(math.e)

LANES = 128
SUBLANES = 8
MXU_DIM = 256
VMEM_LIMIT_BYTES = 56 * 1024 * 1024

ROW_TILE = 512
FF_CHUNK = MXU_DIM
ATTN_BLOCK = 512
ATTN_ONES_ROWS = 16
BIAS_TILE = MAX_DISTANCE
DT_PAD = LANES
CONV_HALO = SUBLANES
SSD_CHUNKS_PER_STEP = 4


def _resident(shape):
    nd = len(shape)
    return pl.BlockSpec(shape, lambda *_: (0,) * nd, pipeline_mode=pl.Buffered(1))


def _params(*sem):
    return pltpu.CompilerParams(dimension_semantics=sem,
                                vmem_limit_bytes=VMEM_LIMIT_BYTES)


def _rms(x, w):
    ms = jnp.mean(x * x, axis=-1, keepdims=True)
    return x * lax.rsqrt(ms + EPS) * w


def _silu(x):
    h = 0.5 * x
    return h + h * jnp.tanh(h)


def _dot(a, b):
    return jnp.dot(a, b, preferred_element_type=F32)


def _dot_nt(a, b):
    return lax.dot_general(a, b, (((1,), (1,)), ((), ())),
                           preferred_element_type=F32)


def _ffn_kernel(x_ref, nw_ref, win_ref, wout_ref, fw_ref, o_ref, *, final):
    x = x_ref[...]
    h = _rms(x, nw_ref[...])
    acc = None
    for c in range(D_FF // FF_CHUNK):
        lo = c * FF_CHUNK
        g = _dot(h, win_ref[:, lo:lo + FF_CHUNK])
        u = _dot(h, win_ref[:, D_FF + lo:D_FF + lo + FF_CHUNK])
        a = _silu(g) * u
        d = _dot(a, wout_ref[lo:lo + FF_CHUNK, :])
        acc = d if acc is None else acc + d
    y = x + 0.5 * acc
    if final:
        y = _rms(y, fw_ref[...])
    o_ref[...] = y


def _ffn(x, nw, win, wout, fw, *, final, name):
    t = x.shape[0]
    row = pl.BlockSpec((ROW_TILE, D_MODEL), lambda i: (i, 0))
    return pl.pallas_call(
        functools.partial(_ffn_kernel, final=final),
        out_shape=jax.ShapeDtypeStruct((t, D_MODEL), F32),
        grid=(t // ROW_TILE,),
        in_specs=[row, _resident((1, D_MODEL)), _resident((D_MODEL, 2 * D_FF)),
                  _resident((D_FF, D_MODEL)), _resident((1, D_MODEL))],
        out_specs=row,
        compiler_params=_params("parallel"),
        name=name,
    )(x, nw, win, wout, fw)


def _in_proj_kernel(x_ref, nw_ref, wt_ref, cw_ref, cb_ref, dtb_ref, alog_ref,
                    z_ref, xbc_ref, cum_ref, src_ref, qt_ref, k_ref, vt_ref, ext_ref,
                    *, tiles_per_seq):
    i = pl.program_id(0)
    tm = ROW_TILE
    h = _rms(x_ref[...], nw_ref[...])
    o_xbc = SSD_WIDTH
    o_dt = o_xbc + CONV_CH
    o_q = o_dt + SSD_HEADS
    o_k = o_q + DA_WIDTH
    o_v = o_k + DA_WIDTH

    def proj(lo, width):
        return _dot_nt(h, wt_ref[lo:lo + width, :])

    @pl.when(i % tiles_per_seq == 0)
    def _():
        ext_ref[:, 0:CONV_HALO, :] = jnp.zeros((CONV_CH // LANES, CONV_HALO, LANES), F32)

    x_dt = proj(o_dt, DT_PAD) + dtb_ref[...]
    dt = jnp.maximum(x_dt, 0.0) + jnp.log1p(jnp.exp(-jnp.abs(x_dt)))
    log2_dt = jnp.log2(dt)
    la = dt * (-LOG2_E * jnp.exp(alog_ref[...]))
    la_hi = la.astype(BF16)
    la_mid = (la - la_hi.astype(F32)).astype(BF16)
    la_lo = (la - la_hi.astype(F32) - la_mid.astype(F32)).astype(BF16)
    row_i = lax.broadcasted_iota(jnp.int32, (SSD_CHUNK, SSD_CHUNK), 0)
    col_i = lax.broadcasted_iota(jnp.int32, (SSD_CHUNK, SSD_CHUNK), 1)
    tril = (col_i <= row_i).astype(BF16)

    wide = ((z_ref, 0, SSD_WIDTH, 1.0, False),
            (qt_ref, o_q, DA_WIDTH, DA_HEAD_DIM ** -0.5 * LOG2_E, True),
            (k_ref, o_k, DA_WIDTH, 1.0, False), (vt_ref, o_v, DA_WIDTH, 1.0, True))
    n_chunks = tm // SSD_CHUNK
    assert n_chunks == len(wide)
    u = proj(o_xbc, CONV_CH)
    for t in range(CONV_CH // LANES):
        ext_ref[t, CONV_HALO:CONV_HALO + tm, :] = u[:, t * LANES:(t + 1) * LANES]
    for r in range(n_chunks):
        lo = r * SSD_CHUNK
        rows = slice(lo, lo + SSD_CHUNK)
        cum = (_dot(tril, la_hi[rows]) + _dot(tril, la_mid[rows])
               + _dot(tril, la_lo[rows]))
        cum_ref[rows, :] = cum
        src_ref[rows, :] = cum - log2_dt[rows]
        base = CONV_HALO + lo
        acts = []
        for t in range(CONV_CH // LANES):
            ln = slice(t * LANES, (t + 1) * LANES)
            acc = cb_ref[:, ln]
            for j in range(CONV_WIDTH):
                start = base - (CONV_WIDTH - 1) + j
                acc = acc + cw_ref[j:j + 1, ln] * ext_ref[t, start:start + SSD_CHUNK, :]
            acts.append(_silu(acc))
        xbc_ref[rows, :] = jnp.concatenate(acts, axis=-1).astype(BF16)
        o_ref, w_lo, w_width, scale, feature_major = wide[r]
        if feature_major:
            out = _dot_nt(wt_ref[w_lo:w_lo + w_width, :], h)
        else:
            out = proj(w_lo, w_width)
        o_ref[...] = (out if scale == 1.0 else out * scale).astype(BF16)
    ext_ref[:, 0:CONV_HALO, :] = ext_ref[:, tm:tm + CONV_HALO, :]


def _in_proj(x1, nw, w_t, cw, cb, dtb, alog, *, seq):
    t = x1.shape[0]
    in_width = w_t.shape[0]
    assert (SSD_WIDTH + CONV_CH + SSD_HEADS) % SUBLANES == 0
    assert SSD_WIDTH + CONV_CH + DT_PAD <= in_width

    def row(n):
        return pl.BlockSpec((ROW_TILE, n), lambda i: (i, 0))

    def col(n):
        return pl.BlockSpec((n, ROW_TILE), lambda i: (0, i))

    return pl.pallas_call(
        functools.partial(_in_proj_kernel, tiles_per_seq=seq // ROW_TILE),
        out_shape=(jax.ShapeDtypeStruct((t, SSD_WIDTH), BF16),
                   jax.ShapeDtypeStruct((t, CONV_CH), BF16),
                   jax.ShapeDtypeStruct((t, DT_PAD), F32),
                   jax.ShapeDtypeStruct((t, DT_PAD), F32),
                   jax.ShapeDtypeStruct((DA_WIDTH, t), BF16),
                   jax.ShapeDtypeStruct((t, DA_WIDTH), BF16),
                   jax.ShapeDtypeStruct((DA_WIDTH, t), BF16)),
        grid=(t // ROW_TILE,),
        in_specs=[row(D_MODEL), _resident((1, D_MODEL)),
                  _resident((in_width, D_MODEL)),
                  _resident((CONV_WIDTH, CONV_CH)), _resident((1, CONV_CH)),
                  _resident((1, DT_PAD)), _resident((1, DT_PAD))],
        out_specs=(row(SSD_WIDTH), row(CONV_CH), row(DT_PAD), row(DT_PAD),
                   col(DA_WIDTH), row(DA_WIDTH), col(DA_WIDTH)),
        scratch_shapes=[pltpu.VMEM((CONV_CH // LANES, CONV_HALO + ROW_TILE + CONV_HALO,
                                    LANES), F32)],
        compiler_params=_params("arbitrary"),
        name="in_proj",
    )(x1, nw, w_t, cw, cb, dtb, alog)


def _ssd_step(xbc_ref, cum_ref, src_ref, o_ref, state_ref):
    L = SSD_CHUNK
    hp = SSD_HEAD_DIM
    row_i = lax.broadcasted_iota(jnp.int32, (L, L), 0)
    col_i = lax.broadcasted_iota(jnp.int32, (L, L), 1)
    causal = col_i <= row_i
    first = col_i < hp
    keep = (first.astype(BF16), 1.0 - first.astype(BF16))
    for r in range(SSD_CHUNKS_PER_STEP):
        rows = slice(r * SSD_CHUNK, (r + 1) * SSD_CHUNK)
        cum_col = cum_ref[rows, :]
        src_row = src_ref[rows, :].T
        for g in range(SSD_GROUPS):
            b_lo = SSD_WIDTH + g * SSD_STATE
            c_lo = SSD_WIDTH + (SSD_GROUPS + g) * SSD_STATE
            b_g = xbc_ref[rows, b_lo:b_lo + SSD_STATE]
            c_g = xbc_ref[rows, c_lo:c_lo + SSD_STATE]
            cb = _dot_nt(c_g, b_g)
            bt_g = b_g.astype(F32).T
            for pr in range(HEADS_PER_GROUP // 2):
                pair = g * (HEADS_PER_GROUP // 2) + pr
                lanes = slice(pair * 2 * hp, (pair + 1) * 2 * hp)
                x_pair = xbc_ref[rows, lanes]
                x_bd = jnp.concatenate([x_pair * keep[0], x_pair * keep[1]], axis=0)
                cols, w_parts, b_parts = [], [], []
                for t in range(2):
                    h = 2 * pair + t
                    col_b = jnp.broadcast_to(cum_col[:, h:h + 1], (L, L))
                    row_b = jnp.broadcast_to(src_row[h:h + 1, :], (L, L))
                    dec = jnp.exp2(jnp.where(causal, col_b - row_b, -jnp.inf))
                    w_parts.append((cb * dec).astype(BF16))
                    b_parts.append((bt_g * dec[L - 1:L, :]).astype(BF16))
                    cols.append(col_b)
                from_start = jnp.exp2(jnp.where(first, cols[0], cols[1]))
                s_prev = state_ref[pair]
                y = (_dot(jnp.concatenate(w_parts, axis=1), x_bd)
                     + _dot(c_g, s_prev.astype(BF16)) * from_start)
                o_ref[rows, lanes] = y.astype(BF16)
                new = _dot(jnp.concatenate(b_parts, axis=1), x_bd)
                state_ref[pair] = s_prev * from_start[L - 1:L, :] + new


def _ssd_gate_norm(y, xs, z, dskip, nw):
    y = y.astype(F32) + dskip * xs.astype(F32)
    y = y * _silu(z.astype(F32))
    gw = SSD_WIDTH // SSD_GROUPS
    outs = []
    for g in range(SSD_GROUPS):
        yg = y[:, g * gw:(g + 1) * gw]
        ms = jnp.mean(yg * yg, axis=-1, keepdims=True)
        outs.append(yg * lax.rsqrt(ms + EPS))
    return jnp.concatenate(outs, axis=-1) * nw


def _rel_bias_kernel(rb_ref, o_ref):
    k_i = lax.broadcasted_iota(jnp.int32, (BIAS_TILE, BIAS_TILE), 0)
    q_i = lax.broadcasted_iota(jnp.int32, (BIAS_TILE, BIAS_TILE), 1)
    max_exact = NUM_BUCKETS // 2
    for d in range(2):
        n = q_i - k_i + d * BIAS_TILE
        nf = jnp.maximum(n, 1).astype(F32)
        large = max_exact + (jnp.log(nf / max_exact) / math.log(MAX_DISTANCE / max_exact)
                             * (NUM_BUCKETS - max_exact)).astype(jnp.int32)
        large = jnp.minimum(large, NUM_BUCKETS - 1)
        bucket = jnp.where(n < max_exact, n, large)
        hits = [bucket == b for b in range(NUM_BUCKETS)]
        for h in range(DA_HEADS):
            tile = jnp.zeros((BIAS_TILE, BIAS_TILE), F32)
            for b in range(NUM_BUCKETS):
                tile = jnp.where(hits[b], rb_ref[b, h], tile)
            tile = (tile - rb_ref[NUM_BUCKETS - 1, h]) * LOG2_E
            o_ref[h, d] = jnp.where(n >= 0, tile, -jnp.inf)


def _rel_bias_tiles(rel_bias):
    return pl.pallas_call(
        _rel_bias_kernel,
        out_shape=jax.ShapeDtypeStruct((DA_HEADS, 2, BIAS_TILE, BIAS_TILE), F32),
        in_specs=[pl.BlockSpec(memory_space=pltpu.SMEM)],
        out_specs=pl.BlockSpec(memory_space=pltpu.VMEM),
        compiler_params=_params(),
        name="rel_bias",
    )(rel_bias)


def _attn_kernel(qt_ref, k_ref, vt_ref, bt_ref, lam_ref, sw_ref, o_ref,
                 s0_ref, s1_ref, m_ref, acc_ref, *, lambda_init, seq):
    tb = ATTN_BLOCK
    dh = DA_HEAD_DIM
    dv = DA_V_DIM
    nb = seq // tb

    def q_operand(i, c, q0, q1):
        blk = qt_ref[c * dh:(c + 1) * dh, i * tb + q0:i * tb + q1]
        pad = jnp.zeros_like(blk)
        return jnp.concatenate([blk, pad] if c == 0 else [pad, blk], axis=0)

    def v_operand(j, k0, k1):
        ones = jnp.ones((ATTN_ONES_ROWS, k1 - k0), BF16)
        return jnp.concatenate([vt_ref[:, j * tb + k0:j * tb + k1], ones], axis=0)

    s_refs = (s0_ref, s1_ref)
    half = tb // 2
    bt = BIAS_TILE
    nsub = tb // bt
    full = ((0, tb, 0, tb),)
    diag_parts = ((0, half, 0, half), (0, tb, half, tb))

    zero = jnp.minimum(pl.program_id(0), 0)

    def rows(lo, hi):
        return pl.ds(pl.multiple_of(zero + lo, BIAS_TILE), hi - lo)

    def scores(c, step):
        i, j, kind = step
        k0 = j * tb
        if kind != "diag":
            s_refs[c][rows(0, tb), :] = _dot(k_ref[k0:k0 + tb, :],
                                             q_operand(i, c, 0, tb))
        else:
            s_refs[c][rows(0, half), :] = _dot(k_ref[k0:k0 + half, :],
                                               q_operand(i, c, 0, tb))
            s_refs[c][rows(half, tb), half:tb] = _dot(k_ref[k0 + half:k0 + tb, :],
                                                      q_operand(i, c, half, tb))

    def add_bias(c, kind):
        s_ref = s_refs[c]
        if kind == "near":
            near = (rows(tb - bt, tb), slice(0, bt))
            s_ref[near] = s_ref[near] + bt_ref[1]
        elif kind == "diag":
            for a in range(nsub):
                on = (rows(a * bt, (a + 1) * bt), slice(a * bt, (a + 1) * bt))
                s_ref[on] = s_ref[on] + bt_ref[0]
                if a + 1 < nsub:
                    off = (rows(a * bt, (a + 1) * bt), slice((a + 1) * bt, (a + 2) * bt))
                    s_ref[off] = s_ref[off] + bt_ref[1]
            for k0, k1, q0, q1 in diag_parts:
                for a in range(k0 // bt, k1 // bt):
                    for b in range(q0 // bt, q1 // bt):
                        if a > b:
                            s_ref[rows(a * bt, (a + 1) * bt), b * bt:(b + 1) * bt] = (
                                jnp.full((bt, bt), -jnp.inf, F32))

    def softmax_pv(c, step):
        i, j, kind = step
        for k0, k1, q0, q1 in (diag_parts if kind == "diag" else full):
            s = s_refs[c][rows(k0, k1), q0:q1]
            m_blk = jnp.max(s, axis=0, keepdims=True)
            if j == 0:
                m_new = m_blk
            else:
                m_old = m_ref[i, c, :, q0:q1]
                m_new = jnp.maximum(m_old, m_blk)
            pv = _dot(v_operand(j, k0, k1), jnp.exp2(s - m_new).astype(BF16))
            if j == 0:
                acc_ref[i, c, :, q0:q1] = pv
            else:
                acc_ref[i, c, :, q0:q1] = (acc_ref[i, c, :, q0:q1]
                                           * jnp.exp2(m_old - m_new) + pv)
            m_ref[i, c, :, q0:q1] = m_new

    steps = []
    for j in range(nb):
        for i in range(j, nb):
            steps.append((i, j, "diag" if i == j else "near" if i == j + 1 else "far"))

    scores(0, steps[0])
    for t, step in enumerate(steps):
        scores(1, step)
        add_bias(0, step[2])
        softmax_pv(0, step)
        if t + 1 < len(steps):
            scores(0, steps[t + 1])
        add_bias(1, step[2])
        softmax_pv(1, step)

    lp = lam_ref[...]
    lam = (jnp.exp(jnp.sum(lp[0:1] * lp[1:2], axis=-1, keepdims=True))
           - jnp.exp(jnp.sum(lp[2:3] * lp[3:4], axis=-1, keepdims=True))
           + lambda_init)
    for i in range(nb):
        a0 = acc_ref[i, 0]
        a1 = acc_ref[i, 1]
        o = a0[0:dv] / a0[dv:dv + 1] - lam * (a1[0:dv] / a1[dv:dv + 1])
        o = o * lax.rsqrt(jnp.mean(o * o, axis=0, keepdims=True) + EPS)
        o = o * sw_ref[...] * (1.0 - lambda_init)
        o_ref[i * tb:(i + 1) * tb, :] = o.T.astype(BF16)


def _mixers_kernel(qt_ref, k_ref, vt_ref, bt_ref, lam_ref, sw_ref, xbc_ref, cum_ref,
                   src_ref, o_attn_ref, o_scan_ref, s0_ref, s1_ref, m_ref, acc_ref,
                   state_ref, *, lambda_init, seq):
    @pl.when(pl.program_id(1) == 0)
    def _():
        state_ref[...] = jnp.zeros_like(state_ref)

    _ssd_step(xbc_ref, cum_ref, src_ref, o_scan_ref, state_ref)
    _attn_kernel(qt_ref, k_ref, vt_ref, bt_ref, lam_ref, sw_ref, o_attn_ref,
                 s0_ref, s1_ref, m_ref, acc_ref, lambda_init=lambda_init, seq=seq)


def _mixers(q_t, k, v_t, tiles, lam_p, sw, xbc, cum, src, *, batch, seq, lambda_init):
    nb = seq // ATTN_BLOCK
    step_rows = SSD_CHUNKS_PER_STEP * SSD_CHUNK
    assert seq // step_rows == DA_HEADS
    seq_head = pl.BlockSpec((seq, DA_V_DIM), lambda b, h: (b, h))
    head_seq = pl.BlockSpec((DA_V_DIM, seq), lambda b, h: (h, b))
    acc_rows = DA_V_DIM + ATTN_ONES_ROWS

    def scan_rows(n):
        return pl.BlockSpec((step_rows, n), lambda b, h: (b * DA_HEADS + h, 0))

    return pl.pallas_call(
        functools.partial(_mixers_kernel, lambda_init=lambda_init, seq=seq),
        out_shape=(jax.ShapeDtypeStruct((batch * seq, DA_WIDTH), BF16),
                   jax.ShapeDtypeStruct((batch * seq, SSD_WIDTH), BF16)),
        grid=(batch, DA_HEADS),
        in_specs=[head_seq, seq_head, head_seq,
                  pl.BlockSpec((None, 2, BIAS_TILE, BIAS_TILE),
                               lambda b, h: (h, 0, 0, 0)),
                  _resident((4, DA_HEAD_DIM)), _resident((DA_V_DIM, ATTN_BLOCK)),
                  scan_rows(CONV_CH), scan_rows(DT_PAD), scan_rows(DT_PAD)],
        out_specs=(seq_head, scan_rows(SSD_WIDTH)),
        scratch_shapes=[pltpu.VMEM((ATTN_BLOCK, ATTN_BLOCK), F32),
                        pltpu.VMEM((ATTN_BLOCK, ATTN_BLOCK), F32),
                        pltpu.VMEM((nb, 2, 1, ATTN_BLOCK), F32),
                        pltpu.VMEM((nb, 2, acc_rows, ATTN_BLOCK), F32),
                        pltpu.VMEM((SSD_HEADS // 2, SSD_STATE, 2 * SSD_HEAD_DIM), F32)],
        compiler_params=_params("parallel", "arbitrary"),
        name="mixers",
    )(q_t, k, v_t, tiles, lam_p, sw, xbc, cum, src)


def _mem_kv_kernel(mem_ref, nw_ref, wk_ref, wv_ref, k_ref, v_ref):
    mn = _rms(mem_ref[...], nw_ref[...])
    k_ref[...] = _dot(mn, wk_ref[...]).astype(BF16)
    v_ref[...] = _dot(mn, wv_ref[...]).astype(BF16)


def _mem_kv(mem, nw, wk, wv):
    t = mem.shape[0]
    row = pl.BlockSpec((MEM_LEN, D_MODEL), lambda i: (i, 0))
    return pl.pallas_call(
        _mem_kv_kernel,
        out_shape=(jax.ShapeDtypeStruct((t, D_MODEL), BF16),
                   jax.ShapeDtypeStruct((t, D_MODEL), BF16)),
        grid=(t // MEM_LEN,),
        in_specs=[row, _resident((1, D_MODEL)), _resident((D_MODEL, D_MODEL)),
                  _resident((D_MODEL, D_MODEL))],
        out_specs=(row, row),
        compiler_params=_params("parallel"),
        name="mem_kv",
    )(mem, nw, wk, wv)


def _mix_out_cross_kernel(x_ref, ys_ref, xs_ref, z_ref, ya_ref, dskip_ref, snw_ref,
                          ws_ref, wa_ref, nw_ref, wq_ref, k_ref, v_ref, wo_ref, o_ref):
    y_ssd = _ssd_gate_norm(ys_ref[...], xs_ref[...], z_ref[...], dskip_ref[...],
                           snw_ref[...])
    x = (x_ref[...] + _dot(ya_ref[...].astype(F32), wa_ref[...])
         + _dot(y_ssd, ws_ref[...]))
    h = _rms(x, nw_ref[...])
    q = (_dot(h, wq_ref[...]) * (CROSS_HEAD_DIM ** -0.5)).astype(BF16)
    outs = []
    for hh in range(CROSS_HEADS):
        sl = slice(hh * CROSS_HEAD_DIM, (hh + 1) * CROSS_HEAD_DIM)
        s = _dot_nt(q[:, sl], k_ref[:, sl])
        p = jnp.exp(s - jnp.max(s, axis=-1, keepdims=True))
        o = _dot(p.astype(BF16), v_ref[:, sl])
        outs.append(o / jnp.sum(p, axis=-1, keepdims=True))
    o_ref[...] = x + _dot(jnp.concatenate(outs, axis=-1), wo_ref[...])


def _mix_out_cross(x1, y_scan, xbc, z, y_attn, dskip, snw, w_out, nw, wq, k, v, wo,
                   *, seq):
    t = x1.shape[0]
    tiles_per_batch = seq // ROW_TILE
    row = pl.BlockSpec((ROW_TILE, D_MODEL), lambda i: (i, 0))
    mem = pl.BlockSpec((MEM_LEN, D_MODEL), lambda i: (i // tiles_per_batch, 0))
    assert SSD_WIDTH == D_MODEL
    assert SSD_WIDTH == DA_WIDTH

    def w_half(j):
        return pl.BlockSpec((SSD_WIDTH, D_MODEL), lambda i: (j, 0),
                            pipeline_mode=pl.Buffered(1))

    return pl.pallas_call(
        _mix_out_cross_kernel,
        out_shape=jax.ShapeDtypeStruct((t, D_MODEL), F32),
        grid=(t // ROW_TILE,),
        in_specs=[row, row, row, row, row, _resident((1, SSD_WIDTH)),
                  _resident((1, SSD_WIDTH)), w_half(0), w_half(1),
                  _resident((1, D_MODEL)), _resident((D_MODEL, D_MODEL)), mem, mem,
                  _resident((D_MODEL, D_MODEL))],
        out_specs=row,
        compiler_params=_params("parallel"),
        name="mix_out_cross",
    )(x1, y_scan, xbc, z, y_attn, dskip, snw, w_out, w_out, nw, wq, k, v, wo)


def kernel(x, mem, norm_ffn1_w, ffn1_w_in, ffn1_w_out, norm_mix_w, w_in_mix, conv_w, conv_b, dt_bias, a_log, d_skip, ssd_norm_w, lambda_q1, lambda_k1, lambda_q2, lambda_k2, subln_w, rel_bias, w_out_mix, norm_cross_w, norm_mem_w, w_cq, w_ck, w_cv, w_co, norm_ffn2_w, ffn2_w_in, ffn2_w_out, norm_final_w):
    batch, seq, _ = x.shape
    depth = ffn1_w_in.shape[0]
    t = batch * seq
    xf = x.reshape(t, D_MODEL)
    memf = mem.reshape(batch * MEM_LEN, D_MODEL)
    fw = norm_final_w.reshape(1, D_MODEL)
    tiles = _rel_bias_tiles(rel_bias)

    def vec(p):
        return p.reshape(1, -1)

    def pad_lanes(p):
        return jnp.pad(p.reshape(1, -1), ((0, 0), (0, DT_PAD - p.shape[-1])))

    for l in range(depth):
        lambda_init = 0.8 - 0.6 * math.exp(-0.3 * l)
        xf = _ffn(xf, vec(norm_ffn1_w[l]), ffn1_w_in[l], ffn1_w_out[l], fw,
                  final=False, name="ffn1")

        z, xbc, cum, src, q_t, k, v_t = _in_proj(
            xf, vec(norm_mix_w[l]), w_in_mix[l].T,
            conv_w[l], vec(conv_b[l]), pad_lanes(dt_bias[l]), pad_lanes(a_log[l]),
            seq=seq)

        lam_p = jnp.stack([lambda_q1[l], lambda_k1[l], lambda_q2[l], lambda_k2[l]])
        sw = jnp.broadcast_to(subln_w[l].reshape(DA_V_DIM, 1), (DA_V_DIM, ATTN_BLOCK))
        y_attn, y_scan = _mixers(q_t, k, v_t, tiles, lam_p, sw, xbc, cum, src,
                                 batch=batch, seq=seq, lambda_init=lambda_init)

        mk, mv = _mem_kv(memf, vec(norm_mem_w[l]), w_ck[l], w_cv[l])
        xf = _mix_out_cross(xf, y_scan, xbc, z, y_attn,
                            vec(jnp.repeat(d_skip[l], SSD_HEAD_DIM)),
                            vec(ssd_norm_w[l]), w_out_mix[l],
                            vec(norm_cross_w[l]), w_cq[l], mk, mv, w_co[l], seq=seq)

        xf = _ffn(xf, vec(norm_ffn2_w[l]), ffn2_w_in[l], ffn2_w_out[l], fw,
                  final=(l == depth - 1), name="ffn2")
    return xf.reshape(batch, seq, D_MODEL)
```

```python
import functools
import math

import jax
import jax.numpy as jnp
from jax import lax
from jax.experimental import pallas as pl
from jax.experimental.pallas import tpu as pltpu

F32 = jnp.float32
BF16 = jnp.bfloat16

D_MODEL = 1024
MEM_LEN = 256
EPS = 1e-6

SSD_HEADS = 16
SSD_HEAD_DIM = 64
SSD_WIDTH = SSD_HEADS * SSD_HEAD_DIM
SSD_GROUPS = 4
SSD_STATE = 128
CONV_WIDTH = 4
SSD_CHUNK = 128
CONV_CH = SSD_WIDTH + 2 * SSD_GROUPS * SSD_STATE
HEADS_PER_GROUP = SSD_HEADS // SSD_GROUPS

DA_HEADS = 8
DA_HEAD_DIM = 64
DA_V_DIM = 2 * DA_HEAD_DIM
DA_WIDTH = DA_HEADS * DA_V_DIM

NUM_BUCKETS = 32
MAX_DISTANCE = 128

CROSS_HEADS = 4
CROSS_HEAD_DIM = D_MODEL // CROSS_HEADS

D_FF = 2816
LOG2_E = math.log2(math.e)

LANES = 128
SUBLANES = 8
MXU_DIM = 256
VMEM_LIMIT_BYTES = 56 * 1024 * 1024

ROW_TILE = 512
FF_CHUNK = MXU_DIM
ATTN_BLOCK = 512
ATTN_ONES_ROWS = 16
BIAS_TILE = MAX_DISTANCE
DT_PAD = LANES
CONV_HALO = SUBLANES
SSD_CHUNKS_PER_STEP = 4


def _resident(shape):
    nd = len(shape)
    return pl.BlockSpec(shape, lambda *_: (0,) * nd, pipeline_mode=pl.Buffered(1))


def _params(*sem):
    return pltpu.CompilerParams(dimension_semantics=sem,
                                vmem_limit_bytes=VMEM_LIMIT_BYTES)


def _rms(x, w):
    ms = jnp.mean(x * x, axis=-1, keepdims=True)
    return x * lax.rsqrt(ms + EPS) * w


def _silu(x):
    h = 0.5 * x
    return h + h * jnp.tanh(h)


def _dot(a, b):
    return jnp.dot(a, b, preferred_element_type=F32)


def _dot_nt(a, b):
    return lax.dot_general(a, b, (((1,), (1,)), ((), ())),
                           preferred_element_type=F32)


def _ffn_kernel(x_ref, nw_ref, win_ref, wout_ref, fw_ref, o_ref, *, final):
    x = x_ref[...]
    h = _rms(x, nw_ref[...])
    acc = None
    for c in range(D_FF // FF_CHUNK):
        lo = c * FF_CHUNK
        g = _dot(h, win_ref[:, lo:lo + FF_CHUNK])
        u = _dot(h, win_ref[:, D_FF + lo:D_FF + lo + FF_CHUNK])
        a = _silu(g) * u
        d = _dot(a, wout_ref[lo:lo + FF_CHUNK, :])
        acc = d if acc is None else acc + d
    y = x + 0.5 * acc
    if final:
        y = _rms(y, fw_ref[...])
    o_ref[...] = y


def _ffn(x, nw, win, wout, fw, *, final, name):
    t = x.shape[0]
    row = pl.BlockSpec((ROW_TILE, D_MODEL), lambda i: (i, 0))
    return pl.pallas_call(
        functools.partial(_ffn_kernel, final=final),
        out_shape=jax.ShapeDtypeStruct((t, D_MODEL), F32),
        grid=(t // ROW_TILE,),
        in_specs=[row, _resident((1, D_MODEL)), _resident((D_MODEL, 2 * D_FF)),
                  _resident((D_FF, D_MODEL)), _resident((1, D_MODEL))],
        out_specs=row,
        compiler_params=_params("parallel"),
        name=name,
    )(x, nw, win, wout, fw)


def _in_proj_kernel(x_ref, nw_ref, wt_ref, cw_ref, cb_ref, dtb_ref, alog_ref,
                    z_ref, xbc_ref, cum_ref, src_ref, qt_ref, k_ref, vt_ref, ext_ref,
                    *, tiles_per_seq):
    i = pl.program_id(0)
    tm = ROW_TILE
    h = _rms(x_ref[...], nw_ref[...])
    o_xbc = SSD_WIDTH
    o_dt = o_xbc + CONV_CH
    o_q = o_dt + SSD_HEADS
    o_k = o_q + DA_WIDTH
    o_v = o_k + DA_WIDTH

    def proj(lo, width):
        return _dot_nt(h, wt_ref[lo:lo + width, :])

    @pl.when(i % tiles_per_seq == 0)
    def _():
        ext_ref[:, 0:CONV_HALO, :] = jnp.zeros((CONV_CH // LANES, CONV_HALO, LANES), F32)

    x_dt = proj(o_dt, DT_PAD) + dtb_ref[...]
    dt = jnp.maximum(x_dt, 0.0) + jnp.log1p(jnp.exp(-jnp.abs(x_dt)))
    log2_dt = jnp.log2(dt)
    la = dt * (-LOG2_E * jnp.exp(alog_ref[...]))
    la_hi = la.astype(BF16)
    la_mid = (la - la_hi.astype(F32)).astype(BF16)
    la_lo = (la - la_hi.astype(F32) - la_mid.astype(F32)).astype(BF16)
    row_i = lax.broadcasted_iota(jnp.int32, (SSD_CHUNK, SSD_CHUNK), 0)
    col_i = lax.broadcasted_iota(jnp.int32, (SSD_CHUNK, SSD_CHUNK), 1)
    tril = (col_i <= row_i).astype(BF16)

    wide = ((z_ref, 0, SSD_WIDTH, 1.0, False),
            (qt_ref, o_q, DA_WIDTH, DA_HEAD_DIM ** -0.5 * LOG2_E, True),
            (k_ref, o_k, DA_WIDTH, 1.0, False), (vt_ref, o_v, DA_WIDTH, 1.0, True))
    n_chunks = tm // SSD_CHUNK
    assert n_chunks == len(wide)
    u = proj(o_xbc, CONV_CH)
    for t in range(CONV_CH // LANES):
        ext_ref[t, CONV_HALO:CONV_HALO + tm, :] = u[:, t * LANES:(t + 1) * LANES]
    for r in range(n_chunks):
        lo = r * SSD_CHUNK
        rows = slice(lo, lo + SSD_CHUNK)
        cum = (_dot(tril, la_hi[rows]) + _dot(tril, la_mid[rows])
               + _dot(tril, la_lo[rows]))
        cum_ref[rows, :] = cum
        src_ref[rows, :] = cum - log2_dt[rows]
        base = CONV_HALO + lo
        acts = []
        for t in range(CONV_CH // LANES):
            ln = slice(t * LANES, (t + 1) * LANES)
            acc = cb_ref[:, ln]
            for j in range(CONV_WIDTH):
                start = base - (CONV_WIDTH - 1) + j
                acc = acc + cw_ref[j:j + 1, ln] * ext_ref[t, start:start + SSD_CHUNK, :]
            acts.append(_silu(acc))
        xbc_ref[rows, :] = jnp.concatenate(acts, axis=-1).astype(BF16)
        o_ref, w_lo, w_width, scale, feature_major = wide[r]
        if feature_major:
            out = _dot_nt(wt_ref[w_lo:w_lo + w_width, :], h)
        else:
            out = proj(w_lo, w_width)
        o_ref[...] = (out if scale == 1.0 else out * scale).astype(BF16)
    ext_ref[:, 0:CONV_HALO, :] = ext_ref[:, tm:tm + CONV_HALO, :]


def _in_proj(x1, nw, w_t, cw, cb, dtb, alog, *, seq):
    t = x1.shape[0]
    in_width = w_t.shape[0]
    assert (SSD_WIDTH + CONV_CH + SSD_HEADS) % SUBLANES == 0
    assert SSD_WIDTH + CONV_CH + DT_PAD <= in_width

    def row(n):
        return pl.BlockSpec((ROW_TILE, n), lambda i: (i, 0))

    def col(n):
        return pl.BlockSpec((n, ROW_TILE), lambda i: (0, i))

    return pl.pallas_call(
        functools.partial(_in_proj_kernel, tiles_per_seq=seq // ROW_TILE),
        out_shape=(jax.ShapeDtypeStruct((t, SSD_WIDTH), BF16),
                   jax.ShapeDtypeStruct((t, CONV_CH), BF16),
                   jax.ShapeDtypeStruct((t, DT_PAD), F32),
                   jax.ShapeDtypeStruct((t, DT_PAD), F32),
                   jax.ShapeDtypeStruct((DA_WIDTH, t), BF16),
                   jax.ShapeDtypeStruct((t, DA_WIDTH), BF16),
                   jax.ShapeDtypeStruct((DA_WIDTH, t), BF16)),
        grid=(t // ROW_TILE,),
        in_specs=[row(D_MODEL), _resident((1, D_MODEL)),
                  _resident((in_width, D_MODEL)),
                  _resident((CONV_WIDTH, CONV_CH)), _resident((1, CONV_CH)),
                  _resident((1, DT_PAD)), _resident((1, DT_PAD))],
        out_specs=(row(SSD_WIDTH), row(CONV_CH), row(DT_PAD), row(DT_PAD),
                   col(DA_WIDTH), row(DA_WIDTH), col(DA_WIDTH)),
        scratch_shapes=[pltpu.VMEM((CONV_CH // LANES, CONV_HALO + ROW_TILE + CONV_HALO,
                                    LANES), F32)],
        compiler_params=_params("arbitrary"),
        name="in_proj",
    )(x1, nw, w_t, cw, cb, dtb, alog)


def _ssd_step(xbc_ref, cum_ref, src_ref, o_ref, state_ref):
    L = SSD_CHUNK
    hp = SSD_HEAD_DIM
    row_i = lax.broadcasted_iota(jnp.int32, (L, L), 0)
    col_i = lax.broadcasted_iota(jnp.int32, (L, L), 1)
    causal = col_i <= row_i
    first = col_i < hp
    keep = (first.astype(BF16), 1.0 - first.astype(BF16))
    for r in range(SSD_CHUNKS_PER_STEP):
        rows = slice(r * SSD_CHUNK, (r + 1) * SSD_CHUNK)
        cum_col = cum_ref[rows, :]
        src_row = src_ref[rows, :].T
        for g in range(SSD_GROUPS):
            b_lo = SSD_WIDTH + g * SSD_STATE
            c_lo = SSD_WIDTH + (SSD_GROUPS + g) * SSD_STATE
            b_g = xbc_ref[rows, b_lo:b_lo + SSD_STATE]
            c_g = xbc_ref[rows, c_lo:c_lo + SSD_STATE]
            cb = _dot_nt(c_g, b_g)
            bt_g = b_g.astype(F32).T
            for pr in range(HEADS_PER_GROUP // 2):
                pair = g * (HEADS_PER_GROUP // 2) + pr
                lanes = slice(pair * 2 * hp, (pair + 1) * 2 * hp)
                x_pair = xbc_ref[rows, lanes]
                x_bd = jnp.concatenate([x_pair * keep[0], x_pair * keep[1]], axis=0)
                cols, w_parts, b_parts = [], [], []
                for t in range(2):
                    h = 2 * pair + t
                    col_b = jnp.broadcast_to(cum_col[:, h:h + 1], (L, L))
                    row_b = jnp.broadcast_to(src_row[h:h + 1, :], (L, L))
                    dec = jnp.exp2(jnp.where(causal, col_b - row_b, -jnp.inf))
                    w_parts.append((cb * dec).astype(BF16))
                    b_parts.append((bt_g * dec[L - 1:L, :]).astype(BF16))
                    cols.append(col_b)
                from_start = jnp.exp2(jnp.where(first, cols[0], cols[1]))
                s_prev = state_ref[pair]
                y = (_dot(jnp.concatenate(w_parts, axis=1), x_bd)
                     + _dot(c_g, s_prev.astype(BF16)) * from_start)
                o_ref[rows, lanes] = y.astype(BF16)
                new = _dot(jnp.concatenate(b_parts, axis=1), x_bd)
                state_ref[pair] = s_prev * from_start[L - 1:L, :] + new


def _ssd_gate_norm(y, xs, z, dskip, nw):
    y = y.astype(F32) + dskip * xs.astype(F32)
    y = y * _silu(z.astype(F32))
    gw = SSD_WIDTH // SSD_GROUPS
    outs = []
    for g in range(SSD_GROUPS):
        yg = y[:, g * gw:(g + 1) * gw]
        ms = jnp.mean(yg * yg, axis=-1, keepdims=True)
        outs.append(yg * lax.rsqrt(ms + EPS))
    return jnp.concatenate(outs, axis=-1) * nw


def _rel_bias_kernel(rb_ref, o_ref):
    k_i = lax.broadcasted_iota(jnp.int32, (BIAS_TILE, BIAS_TILE), 0)
    q_i = lax.broadcasted_iota(jnp.int32, (BIAS_TILE, BIAS_TILE), 1)
    max_exact = NUM_BUCKETS // 2
    for d in range(2):
        n = q_i - k_i + d * BIAS_TILE
        nf = jnp.maximum(n, 1).astype(F32)
        large = max_exact + (jnp.log(nf / max_exact) / math.log(MAX_DISTANCE / max_exact)
                             * (NUM_BUCKETS - max_exact)).astype(jnp.int32)
        large = jnp.minimum(large, NUM_BUCKETS - 1)
        bucket = jnp.where(n < max_exact, n, large)
        hits = [bucket == b for b in range(NUM_BUCKETS)]
        for h in range(DA_HEADS):
            tile = jnp.zeros((BIAS_TILE, BIAS_TILE), F32)
            for b in range(NUM_BUCKETS):
                tile = jnp.where(hits[b], rb_ref[b, h], tile)
            tile = (tile - rb_ref[NUM_BUCKETS - 1, h]) * LOG2_E
            o_ref[h, d] = jnp.where(n >= 0, tile, -jnp.inf)


def _rel_bias_tiles(rel_bias):
    return pl.pallas_call(
        _rel_bias_kernel,
        out_shape=jax.ShapeDtypeStruct((DA_HEADS, 2, BIAS_TILE, BIAS_TILE), F32),
        in_specs=[pl.BlockSpec(memory_space=pltpu.SMEM)],
        out_specs=pl.BlockSpec(memory_space=pltpu.VMEM),
        compiler_params=_params(),
        name="rel_bias",
    )(rel_bias)


def _attn_kernel(qt_ref, k_ref, vt_ref, bt_ref, lam_ref, sw_ref, o_ref,
                 s0_ref, s1_ref, m_ref, acc_ref, *, lambda_init, seq):
    tb = ATTN_BLOCK
    dh = DA_HEAD_DIM
    dv = DA_V_DIM
    nb = seq // tb

    def q_operand(i, c, q0, q1):
        blk = qt_ref[c * dh:(c + 1) * dh, i * tb + q0:i * tb + q1]
        pad = jnp.zeros_like(blk)
        return jnp.concatenate([blk, pad] if c == 0 else [pad, blk], axis=0)

    def v_operand(j, k0, k1):
        ones = jnp.ones((ATTN_ONES_ROWS, k1 - k0), BF16)
        return jnp.concatenate([vt_ref[:, j * tb + k0:j * tb + k1], ones], axis=0)

    s_refs = (s0_ref, s1_ref)
    half = tb // 2
    bt = BIAS_TILE
    nsub = tb // bt
    full = ((0, tb, 0, tb),)
    diag_parts = ((0, half, 0, half), (0, tb, half, tb))

    zero = jnp.minimum(pl.program_id(0), 0)

    def rows(lo, hi):
        return pl.ds(pl.multiple_of(zero + lo, BIAS_TILE), hi - lo)

    def scores(c, step):
        i, j, kind = step
        k0 = j * tb
        if kind != "diag":
            s_refs[c][rows(0, tb), 0:tb] = _dot(k_ref[k0:k0 + tb, :],
                                                q_operand(i, c, 0, tb))
        else:
            s_refs[c][rows(0, half), 0:tb] = _dot(k_ref[k0:k0 + half, :],
                                                  q_operand(i, c, 0, tb))
            s_refs[c][rows(half, tb), half:tb] = _dot(k_ref[k0 + half:k0 + tb, :],
                                                      q_operand(i, c, half, tb))

    def add_bias(c, kind):
        s_ref = s_refs[c]
        if kind == "near":
            near = (rows(tb - bt, tb), slice(0, bt))
            s_ref[near] = s_ref[near] + bt_ref[1]
        elif kind == "diag":
            for a in range(nsub):
                on = (rows(a * bt, (a + 1) * bt), slice(a * bt, (a + 1) * bt))
                s_ref[on] = s_ref[on] + bt_ref[0]
                if a + 1 < nsub:
                    off = (rows(a * bt, (a + 1) * bt), slice((a + 1) * bt, (a + 2) * bt))
                    s_ref[off] = s_ref[off] + bt_ref[1]
            for k0, k1, q0, q1 in diag_parts:
                for a in range(k0 // bt, k1 // bt):
                    for b in range(q0 // bt, q1 // bt):
                        if a > b:
                            s_ref[rows(a * bt, (a + 1) * bt), b * bt:(b + 1) * bt] = (
                                jnp.full((bt, bt), -jnp.inf, F32))

    def softmax_pv(c, step):
        i, j, kind = step
        for k0, k1, q0, q1 in (diag_parts if kind == "diag" else full):
            s = s_refs[c][rows(k0, k1), q0:q1]
            m_blk = jnp.max(s, axis=0, keepdims=True)
            if j == 0:
                m_new = m_blk
            else:
                m_old = m_ref[i, c, :, q0:q1]
                m_new = jnp.maximum(m_old, m_blk)
            pv = _dot(v_operand(j, k0, k1), jnp.exp2(s - m_new).astype(BF16))
            if j == 0:
                acc_ref[i, c, :, q0:q1] = pv
            else:
                acc_ref[i, c, :, q0:q1] = (acc_ref[i, c, :, q0:q1]
                                           * jnp.exp2(m_old - m_new) + pv)
            m_ref[i, c, :, q0:q1] = m_new

    steps = []
    for j in range(nb):
        for i in range(j, nb):
            steps.append((i, j, "diag" if i == j else "near" if i == j + 1 else "far"))

    scores(0, steps[0])
    for t, step in enumerate(steps):
        scores(1, step)
        add_bias(0, step[2])
        softmax_pv(0, step)
        if t + 1 < len(steps):
            scores(0, steps[t + 1])
        add_bias(1, step[2])
        softmax_pv(1, step)

    lp = lam_ref[...]
    lam = (jnp.exp(jnp.sum(lp[0:1] * lp[1:2], axis=-1, keepdims=True))
           - jnp.exp(jnp.sum(lp[2:3] * lp[3:4], axis=-1, keepdims=True))
           + lambda_init)
    for i in range(nb):
        a0 = acc_ref[i, 0]
        a1 = acc_ref[i, 1]
        o = a0[0:dv] / a0[dv:dv + 1] - lam * (a1[0:dv] / a1[dv:dv + 1])
        o = o * lax.rsqrt(jnp.mean(o * o, axis=0, keepdims=True) + EPS)
        o = o * sw_ref[...] * (1.0 - lambda_init)
        o_ref[i * tb:(i + 1) * tb, :] = o.T.astype(BF16)


def _mixers_kernel(qt_ref, k_ref, vt_ref, bt_ref, lam_ref, sw_ref, xbc_ref, cum_ref,
                   src_ref, o_attn_ref, o_scan_ref, s0_ref, s1_ref, m_ref, acc_ref,
                   state_ref, *, lambda_init, seq):
    @pl.when(pl.program_id(1) == 0)
    def _():
        state_ref[...] = jnp.zeros_like(state_ref)

    _ssd_step(xbc_ref, cum_ref, src_ref, o_scan_ref, state_ref)
    _attn_kernel(qt_ref, k_ref, vt_ref, bt_ref, lam_ref, sw_ref, o_attn_ref,
                 s0_ref, s1_ref, m_ref, acc_ref, lambda_init=lambda_init, seq=seq)


def _mixers(q_t, k, v_t, tiles, lam_p, sw, xbc, cum, src, *, batch, seq, lambda_init):
    nb = seq // ATTN_BLOCK
    step_rows = SSD_CHUNKS_PER_STEP * SSD_CHUNK
    assert seq // step_rows == DA_HEADS
    seq_head = pl.BlockSpec((seq, DA_V_DIM), lambda b, h: (b, h))
    head_seq = pl.BlockSpec((DA_V_DIM, seq), lambda b, h: (h, b))
    acc_rows = DA_V_DIM + ATTN_ONES_ROWS

    def scan_rows(n):
        return pl.BlockSpec((step_rows, n), lambda b, h: (b * DA_HEADS + h, 0))

    return pl.pallas_call(
        functools.partial(_mixers_kernel, lambda_init=lambda_init, seq=seq),
        out_shape=(jax.ShapeDtypeStruct((batch * seq, DA_WIDTH), BF16),
                   jax.ShapeDtypeStruct((batch * seq, SSD_WIDTH), BF16)),
        grid=(batch, DA_HEADS),
        in_specs=[head_seq, seq_head, head_seq,
                  pl.BlockSpec((None, 2, BIAS_TILE, BIAS_TILE),
                               lambda b, h: (h, 0, 0, 0)),
                  _resident((4, DA_HEAD_DIM)), _resident((DA_V_DIM, ATTN_BLOCK)),
                  scan_rows(CONV_CH), scan_rows(DT_PAD), scan_rows(DT_PAD)],
        out_specs=(seq_head, scan_rows(SSD_WIDTH)),
        scratch_shapes=[pltpu.VMEM((ATTN_BLOCK, ATTN_BLOCK + LANES), F32),
                        pltpu.VMEM((ATTN_BLOCK, ATTN_BLOCK + LANES), F32),
                        pltpu.VMEM((nb, 2, 1, ATTN_BLOCK), F32),
                        pltpu.VMEM((nb, 2, acc_rows, ATTN_BLOCK), F32),
                        pltpu.VMEM((SSD_HEADS // 2, SSD_STATE, 2 * SSD_HEAD_DIM), F32)],
        compiler_params=_params("parallel", "arbitrary"),
        name="mixers",
    )(q_t, k, v_t, tiles, lam_p, sw, xbc, cum, src)


def _mem_kv_kernel(mem_ref, nw_ref, wk_ref, wv_ref, k_ref, v_ref):
    mn = _rms(mem_ref[...], nw_ref[...])
    k_ref[...] = _dot(mn, wk_ref[...]).astype(BF16)
    v_ref[...] = _dot(mn, wv_ref[...]).astype(BF16)


def _mem_kv(mem, nw, wk, wv):
    t = mem.shape[0]
    row = pl.BlockSpec((MEM_LEN, D_MODEL), lambda i: (i, 0))
    return pl.pallas_call(
        _mem_kv_kernel,
        out_shape=(jax.ShapeDtypeStruct((t, D_MODEL), BF16),
                   jax.ShapeDtypeStruct((t, D_MODEL), BF16)),
        grid=(t // MEM_LEN,),
        in_specs=[row, _resident((1, D_MODEL)), _resident((D_MODEL, D_MODEL)),
                  _resident((D_MODEL, D_MODEL))],
        out_specs=(row, row),
        compiler_params=_params("parallel"),
        name="mem_kv",
    )(mem, nw, wk, wv)


def _mix_out_cross_kernel(x_ref, ys_ref, xs_ref, z_ref, ya_ref, dskip_ref, snw_ref,
                          ws_ref, wa_ref, nw_ref, wq_ref, k_ref, v_ref, wo_ref, o_ref):
    y_ssd = _ssd_gate_norm(ys_ref[...], xs_ref[...], z_ref[...], dskip_ref[...],
                           snw_ref[...])
    x = (x_ref[...] + _dot(ya_ref[...].astype(F32), wa_ref[...])
         + _dot(y_ssd, ws_ref[...]))
    h = _rms(x, nw_ref[...])
    q = (_dot(h, wq_ref[...]) * (CROSS_HEAD_DIM ** -0.5)).astype(BF16)
    outs = []
    for hh in range(CROSS_HEADS):
        sl = slice(hh * CROSS_HEAD_DIM, (hh + 1) * CROSS_HEAD_DIM)
        s = _dot_nt(q[:, sl], k_ref[:, sl])
        p = jnp.exp(s - jnp.max(s, axis=-1, keepdims=True))
        o = _dot(p.astype(BF16), v_ref[:, sl])
        outs.append(o / jnp.sum(p, axis=-1, keepdims=True))
    o_ref[...] = x + _dot(jnp.concatenate(outs, axis=-1), wo_ref[...])


def _mix_out_cross(x1, y_scan, xbc, z, y_attn, dskip, snw, w_out, nw, wq, k, v, wo,
                   *, seq):
    t = x1.shape[0]
    tiles_per_batch = seq // ROW_TILE
    row = pl.BlockSpec((ROW_TILE, D_MODEL), lambda i: (i, 0))
    mem = pl.BlockSpec((MEM_LEN, D_MODEL), lambda i: (i // tiles_per_batch, 0))
    assert SSD_WIDTH == D_MODEL
    assert SSD_WIDTH == DA_WIDTH

    def w_half(j):
        return pl.BlockSpec((SSD_WIDTH, D_MODEL), lambda i: (j, 0),
                            pipeline_mode=pl.Buffered(1))

    return pl.pallas_call(
        _mix_out_cross_kernel,
        out_shape=jax.ShapeDtypeStruct((t, D_MODEL), F32),
        grid=(t // ROW_TILE,),
        in_specs=[row, row, row, row, row, _resident((1, SSD_WIDTH)),
                  _resident((1, SSD_WIDTH)), w_half(0), w_half(1),
                  _resident((1, D_MODEL)), _resident((D_MODEL, D_MODEL)), mem, mem,
                  _resident((D_MODEL, D_MODEL))],
        out_specs=row,
        compiler_params=_params("parallel"),
        name="mix_out_cross",
    )(x1, y_scan, xbc, z, y_attn, dskip, snw, w_out, w_out, nw, wq, k, v, wo)


def kernel(x, mem, norm_ffn1_w, ffn1_w_in, ffn1_w_out, norm_mix_w, w_in_mix, conv_w, conv_b, dt_bias, a_log, d_skip, ssd_norm_w, lambda_q1, lambda_k1, lambda_q2, lambda_k2, subln_w, rel_bias, w_out_mix, norm_cross_w, norm_mem_w, w_cq, w_ck, w_cv, w_co, norm_ffn2_w, ffn2_w_in, ffn2_w_out, norm_final_w):
    batch, seq, _ = x.shape
    depth = ffn1_w_in.shape[0]
    t = batch * seq
    xf = x.reshape(t, D_MODEL)
    memf = mem.reshape(batch * MEM_LEN, D_MODEL)
    fw = norm_final_w.reshape(1, D_MODEL)
    tiles = _rel_bias_tiles(rel_bias)

    def vec(p):
        return p.reshape(1, -1)

    def pad_lanes(p):
        return jnp.pad(p.reshape(1, -1), ((0, 0), (0, DT_PAD - p.shape[-1])))

    for l in range(depth):
        lambda_init = 0.8 - 0.6 * math.exp(-0.3 * l)
        xf = _ffn(xf, vec(norm_ffn1_w[l]), ffn1_w_in[l], ffn1_w_out[l], fw,
                  final=False, name="ffn1")

        z, xbc, cum, src, q_t, k, v_t = _in_proj(
            xf, vec(norm_mix_w[l]), w_in_mix[l].T,
            conv_w[l], vec(conv_b[l]), pad_lanes(dt_bias[l]), pad_lanes(a_log[l]),
            seq=seq)

        lam_p = jnp.stack([lambda_q1[l], lambda_k1[l], lambda_q2[l], lambda_k2[l]])
        sw = jnp.broadcast_to(subln_w[l].reshape(DA_V_DIM, 1), (DA_V_DIM, ATTN_BLOCK))
        y_attn, y_scan = _mixers(q_t, k, v_t, tiles, lam_p, sw, xbc, cum, src,
                                 batch=batch, seq=seq, lambda_init=lambda_init)

        mk, mv = _mem_kv(memf, vec(norm_mem_w[l]), w_ck[l], w_cv[l])
        xf = _mix_out_cross(xf, y_scan, xbc, z, y_attn,
                            vec(jnp.repeat(d_skip[l], SSD_HEAD_DIM)),
                            vec(ssd_norm_w[l]), w_out_mix[l],
                            vec(norm_cross_w[l]), w_cq[l], mk, mv, w_co[l], seq=seq)

        xf = _ffn(xf, vec(norm_ffn2_w[l]), ffn2_w_in[l], ffn2_w_out[l], fw,
                  final=(l == depth - 1), name="ffn2")
    return xf.reshape(batch, seq, D_MODEL)
```

```python
import functools
import math

import jax
import jax.numpy as jnp
from jax import lax
from jax.experimental import pallas as pl
from jax.experimental.pallas import tpu as pltpu

F32 = jnp.float32
BF16 = jnp.bfloat16

D_MODEL = 1024
MEM_LEN = 256
EPS = 1e-6

SSD_HEADS = 16
SSD_HEAD_DIM = 64
SSD_WIDTH = SSD_HEADS * SSD_HEAD_DIM
SSD_GROUPS = 4
SSD_STATE = 128
CONV_WIDTH = 4
SSD_CHUNK = 128
CONV_CH = SSD_WIDTH + 2 * SSD_GROUPS * SSD_STATE
HEADS_PER_GROUP = SSD_HEADS // SSD_GROUPS

DA_HEADS = 8
DA_HEAD_DIM = 64
DA_V_DIM = 2 * DA_HEAD_DIM
DA_WIDTH = DA_HEADS * DA_V_DIM

NUM_BUCKETS = 32
MAX_DISTANCE = 128

CROSS_HEADS = 4
CROSS_HEAD_DIM = D_MODEL // CROSS_HEADS

D_FF = 2816
LOG2_E = math.log2(math.e)

LANES = 128
SUBLANES = 8
MXU_DIM = 256
VMEM_LIMIT_BYTES = 56 * 1024 * 1024

ROW_TILE = 512
FF_CHUNK = MXU_DIM
ATTN_BLOCK = 512
ATTN_ONES_ROWS = 16
BIAS_TILE = MAX_DISTANCE
DT_PAD = LANES
CONV_HALO = SUBLANES
SSD_CHUNKS_PER_STEP = 4


def _resident(shape):
    nd = len(shape)
    return pl.BlockSpec(shape, lambda *_: (0,) * nd, pipeline_mode=pl.Buffered(1))


def _params(*sem):
    return pltpu.CompilerParams(dimension_semantics=sem,
                                vmem_limit_bytes=VMEM_LIMIT_BYTES)


def _rms(x, w):
    ms = jnp.mean(x * x, axis=-1, keepdims=True)
    return x * lax.rsqrt(ms + EPS) * w


def _silu(x):
    h = 0.5 * x
    return h + h * jnp.tanh(h)


def _dot(a, b):
    return jnp.dot(a, b, preferred_element_type=F32)


def _dot_nt(a, b):
    return lax.dot_general(a, b, (((1,), (1,)), ((), ())),
                           preferred_element_type=F32)


def _ffn_kernel(x_ref, nw_ref, win_ref, wout_ref, fw_ref, o_ref, *, final):
    x = x_ref[...]
    h = _rms(x, nw_ref[...])
    acc = None
    for c in range(D_FF // FF_CHUNK):
        lo = c * FF_CHUNK
        g = _dot(h, win_ref[:, lo:lo + FF_CHUNK])
        u = _dot(h, win_ref[:, D_FF + lo:D_FF + lo + FF_CHUNK])
        a = _silu(g) * u
        d = _dot(a, wout_ref[lo:lo + FF_CHUNK, :])
        acc = d if acc is None else acc + d
    y = x + 0.5 * acc
    if final:
        y = _rms(y, fw_ref[...])
    o_ref[...] = y


def _ffn(x, nw, win, wout, fw, *, final, name):
    t = x.shape[0]
    row = pl.BlockSpec((ROW_TILE, D_MODEL), lambda i: (i, 0))
    return pl.pallas_call(
        functools.partial(_ffn_kernel, final=final),
        out_shape=jax.ShapeDtypeStruct((t, D_MODEL), F32),
        grid=(t // ROW_TILE,),
        in_specs=[row, _resident((1, D_MODEL)), _resident((D_MODEL, 2 * D_FF)),
                  _resident((D_FF, D_MODEL)), _resident((1, D_MODEL))],
        out_specs=row,
        compiler_params=_params("parallel"),
        name=name,
    )(x, nw, win, wout, fw)


def _in_proj_kernel(x_ref, nw_ref, wt_ref, cw_ref, cb_ref, dtb_ref, alog_ref,
                    z_ref, xbc_ref, cum_ref, src_ref, qt_ref, k_ref, vt_ref, ext_ref,
                    *, tiles_per_seq):
    i = pl.program_id(0)
    tm = ROW_TILE
    h = _rms(x_ref[...], nw_ref[...])
    o_xbc = SSD_WIDTH
    o_dt = o_xbc + CONV_CH
    o_q = o_dt + SSD_HEADS
    o_k = o_q + DA_WIDTH
    o_v = o_k + DA_WIDTH

    def proj(lo, width):
        return _dot_nt(h, wt_ref[lo:lo + width, :])

    @pl.when(i % tiles_per_seq == 0)
    def _():
        ext_ref[:, 0:CONV_HALO, :] = jnp.zeros((CONV_CH // LANES, CONV_HALO, LANES), F32)

    x_dt = proj(o_dt, DT_PAD) + dtb_ref[...]
    dt = jnp.maximum(x_dt, 0.0) + jnp.log1p(jnp.exp(-jnp.abs(x_dt)))
    log2_dt = jnp.log2(dt)
    la = dt * (-LOG2_E * jnp.exp(alog_ref[...]))
    la_hi = la.astype(BF16)
    la_mid = (la - la_hi.astype(F32)).astype(BF16)
    la_lo = (la - la_hi.astype(F32) - la_mid.astype(F32)).astype(BF16)
    row_i = lax.broadcasted_iota(jnp.int32, (SSD_CHUNK, SSD_CHUNK), 0)
    col_i = lax.broadcasted_iota(jnp.int32, (SSD_CHUNK, SSD_CHUNK), 1)
    tril = (col_i <= row_i).astype(BF16)

    wide = ((z_ref, 0, SSD_WIDTH, 1.0, False),
            (qt_ref, o_q, DA_WIDTH, DA_HEAD_DIM ** -0.5 * LOG2_E, True),
            (k_ref, o_k, DA_WIDTH, 1.0, False), (vt_ref, o_v, DA_WIDTH, 1.0, True))
    n_chunks = tm // SSD_CHUNK
    assert n_chunks == len(wide)
    u = proj(o_xbc, CONV_CH)
    for t in range(CONV_CH // LANES):
        ext_ref[t, CONV_HALO:CONV_HALO + tm, :] = u[:, t * LANES:(t + 1) * LANES]
    for r in range(n_chunks):
        lo = r * SSD_CHUNK
        rows = slice(lo, lo + SSD_CHUNK)
        cum = (_dot(tril, la_hi[rows]) + _dot(tril, la_mid[rows])
               + _dot(tril, la_lo[rows]))
        cum_ref[rows, :] = cum
        src_ref[rows, :] = cum - log2_dt[rows]
        base = CONV_HALO + lo
        acts = []
        for t in range(CONV_CH // LANES):
            ln = slice(t * LANES, (t + 1) * LANES)
            acc = cb_ref[:, ln]
            for j in range(CONV_WIDTH):
                start = base - (CONV_WIDTH - 1) + j
                acc = acc + cw_ref[j:j + 1, ln] * ext_ref[t, start:start + SSD_CHUNK, :]
            acts.append(_silu(acc))
        xbc_ref[rows, :] = jnp.concatenate(acts, axis=-1).astype(BF16)
        o_ref, w_lo, w_width, scale, feature_major = wide[r]
        if feature_major:
            out = _dot_nt(wt_ref[w_lo:w_lo + w_width, :], h)
        else:
            out = proj(w_lo, w_width)
        o_ref[...] = (out if scale == 1.0 else out * scale).astype(BF16)
    ext_ref[:, 0:CONV_HALO, :] = ext_ref[:, tm:tm + CONV_HALO, :]


def _in_proj(x1, nw, w_t, cw, cb, dtb, alog, *, seq):
    t = x1.shape[0]
    in_width = w_t.shape[0]
    assert (SSD_WIDTH + CONV_CH + SSD_HEADS) % SUBLANES == 0
    assert SSD_WIDTH + CONV_CH + DT_PAD <= in_width

    def row(n):
        return pl.BlockSpec((ROW_TILE, n), lambda i: (i, 0))

    def col(n):
        return pl.BlockSpec((n, ROW_TILE), lambda i: (0, i))

    return pl.pallas_call(
        functools.partial(_in_proj_kernel, tiles_per_seq=seq // ROW_TILE),
        out_shape=(jax.ShapeDtypeStruct((t, SSD_WIDTH), BF16),
                   jax.ShapeDtypeStruct((t, CONV_CH), BF16),
                   jax.ShapeDtypeStruct((t, DT_PAD), F32),
                   jax.ShapeDtypeStruct((t, DT_PAD), F32),
                   jax.ShapeDtypeStruct((DA_WIDTH, t), BF16),
                   jax.ShapeDtypeStruct((t, DA_WIDTH), BF16),
                   jax.ShapeDtypeStruct((DA_WIDTH, t), BF16)),
        grid=(t // ROW_TILE,),
        in_specs=[row(D_MODEL), _resident((1, D_MODEL)),
                  _resident((in_width, D_MODEL)),
                  _resident((CONV_WIDTH, CONV_CH)), _resident((1, CONV_CH)),
                  _resident((1, DT_PAD)), _resident((1, DT_PAD))],
        out_specs=(row(SSD_WIDTH), row(CONV_CH), row(DT_PAD), row(DT_PAD),
                   col(DA_WIDTH), row(DA_WIDTH), col(DA_WIDTH)),
        scratch_shapes=[pltpu.VMEM((CONV_CH // LANES, CONV_HALO + ROW_TILE + CONV_HALO,
                                    LANES), F32)],
        compiler_params=_params("arbitrary"),
        name="in_proj",
    )(x1, nw, w_t, cw, cb, dtb, alog)


def _ssd_step(xbc_ref, cum_ref, src_ref, o_ref, state_ref):
    L = SSD_CHUNK
    hp = SSD_HEAD_DIM
    row_i = lax.broadcasted_iota(jnp.int32, (L, L), 0)
    col_i = lax.broadcasted_iota(jnp.int32, (L, L), 1)
    causal = col_i <= row_i
    first = col_i < hp
    keep = (first.astype(BF16), 1.0 - first.astype(BF16))
    for r in range(SSD_CHUNKS_PER_STEP):
        rows = slice(r * SSD_CHUNK, (r + 1) * SSD_CHUNK)
        cum_col = cum_ref[rows, :]
        src_row = src_ref[rows, :].T
        for g in range(SSD_GROUPS):
            b_lo = SSD_WIDTH + g * SSD_STATE
            c_lo = SSD_WIDTH + (SSD_GROUPS + g) * SSD_STATE
            b_g = xbc_ref[rows, b_lo:b_lo + SSD_STATE]
            c_g = xbc_ref[rows, c_lo:c_lo + SSD_STATE]
            cb = _dot_nt(c_g, b_g)
            bt_g = b_g.astype(F32).T
            for pr in range(HEADS_PER_GROUP // 2):
                pair = g * (HEADS_PER_GROUP // 2) + pr
                lanes = slice(pair * 2 * hp, (pair + 1) * 2 * hp)
                x_pair = xbc_ref[rows, lanes]
                x_bd = jnp.concatenate([x_pair * keep[0], x_pair * keep[1]], axis=0)
                cols, w_parts, b_parts = [], [], []
                for t in range(2):
                    h = 2 * pair + t
                    col_b = jnp.broadcast_to(cum_col[:, h:h + 1], (L, L))
                    row_b = jnp.broadcast_to(src_row[h:h + 1, :], (L, L))
                    dec = jnp.exp2(jnp.where(causal, col_b - row_b, -jnp.inf))
                    w_parts.append((cb * dec).astype(BF16))
                    b_parts.append((bt_g * dec[L - 1:L, :]).astype(BF16))
                    cols.append(col_b)
                from_start = jnp.exp2(jnp.where(first, cols[0], cols[1]))
                s_prev = state_ref[pair]
                y = (_dot(jnp.concatenate(w_parts, axis=1), x_bd)
                     + _dot(c_g, s_prev.astype(BF16)) * from_start)
                o_ref[rows, lanes] = y.astype(BF16)
                new = _dot(jnp.concatenate(b_parts, axis=1), x_bd)
                state_ref[pair] = s_prev * from_start[L - 1:L, :] + new


def _ssd_gate_norm(y, xs, z, dskip, nw):
    y = y.astype(F32) + dskip * xs.astype(F32)
    y = y * _silu(z.astype(F32))
    gw = SSD_WIDTH // SSD_GROUPS
    outs = []
    for g in range(SSD_GROUPS):
        yg = y[:, g * gw:(g + 1) * gw]
        ms = jnp.mean(yg * yg, axis=-1, keepdims=True)
        outs.append(yg * lax.rsqrt(ms + EPS))
    return jnp.concatenate(outs, axis=-1) * nw


def _rel_bias_kernel(rb_ref, o_ref):
    k_i = lax.broadcasted_iota(jnp.int32, (BIAS_TILE, BIAS_TILE), 0)
    q_i = lax.broadcasted_iota(jnp.int32, (BIAS_TILE, BIAS_TILE), 1)
    max_exact = NUM_BUCKETS // 2
    for d in range(2):
        n = q_i - k_i + d * BIAS_TILE
        nf = jnp.maximum(n, 1).astype(F32)
        large = max_exact + (jnp.log(nf / max_exact) / math.log(MAX_DISTANCE / max_exact)
                             * (NUM_BUCKETS - max_exact)).astype(jnp.int32)
        large = jnp.minimum(large, NUM_BUCKETS - 1)
        bucket = jnp.where(n < max_exact, n, large)
        hits = [bucket == b for b in range(NUM_BUCKETS)]
        for h in range(DA_HEADS):
            tile = jnp.zeros((BIAS_TILE, BIAS_TILE), F32)
            for b in range(NUM_BUCKETS):
                tile = jnp.where(hits[b], rb_ref[b, h], tile)
            tile = (tile - rb_ref[NUM_BUCKETS - 1, h]) * LOG2_E
            o_ref[h, d] = jnp.where(n >= 0, tile, -jnp.inf)


def _rel_bias_tiles(rel_bias):
    return pl.pallas_call(
        _rel_bias_kernel,
        out_shape=jax.ShapeDtypeStruct((DA_HEADS, 2, BIAS_TILE, BIAS_TILE), F32),
        in_specs=[pl.BlockSpec(memory_space=pltpu.SMEM)],
        out_specs=pl.BlockSpec(memory_space=pltpu.VMEM),
        compiler_params=_params(),
        name="rel_bias",
    )(rel_bias)


def _attn_kernel(qt_ref, k_ref, vt_ref, bt_ref, lam_ref, sw_ref, o_ref,
                 s0_ref, s1_ref, m_ref, acc_ref, *, lambda_init, seq):
    tb = ATTN_BLOCK
    dh = DA_HEAD_DIM
    dv = DA_V_DIM
    nb = seq // tb

    def q_operand(i, c, q0, q1):
        blk = qt_ref[c * dh:(c + 1) * dh, i * tb + q0:i * tb + q1]
        pad = jnp.zeros_like(blk)
        return jnp.concatenate([blk, pad] if c == 0 else [pad, blk], axis=0)

    def v_operand(j, k0, k1):
        ones = jnp.ones((ATTN_ONES_ROWS, k1 - k0), BF16)
        return jnp.concatenate([vt_ref[:, j * tb + k0:j * tb + k1], ones], axis=0)

    s_refs = (s0_ref, s1_ref)
    half = tb // 2
    bt = BIAS_TILE
    nsub = tb // bt
    full = ((0, tb, 0, tb),)
    diag_parts = ((0, half, 0, half), (0, tb, half, tb))

    zero = jnp.minimum(pl.program_id(0), 0)

    def rows(lo, hi):
        return pl.ds(pl.multiple_of(zero + lo, BIAS_TILE), hi - lo)

    def scores(c, step):
        i, j, kind = step
        k0 = j * tb
        if kind != "diag":
            s_refs[c][rows(0, tb), 0:tb] = _dot(k_ref[k0:k0 + tb, :],
                                                q_operand(i, c, 0, tb))
        else:
            s_refs[c][rows(0, half), 0:tb] = _dot(k_ref[k0:k0 + half, :],
                                                  q_operand(i, c, 0, tb))
            s_refs[c][rows(half, tb), half:tb] = _dot(k_ref[k0 + half:k0 + tb, :],
                                                      q_operand(i, c, half, tb))

    def add_bias(c, kind):
        s_ref = s_refs[c]
        if kind == "near":
            near = (rows(tb - bt, tb), slice(0, bt))
            s_ref[near] = s_ref[near] + bt_ref[1]
        elif kind == "diag":
            for a in range(nsub):
                on = (rows(a * bt, (a + 1) * bt), slice(a * bt, (a + 1) * bt))
                s_ref[on] = s_ref[on] + bt_ref[0]
                if a + 1 < nsub:
                    off = (rows(a * bt, (a + 1) * bt), slice((a + 1) * bt, (a + 2) * bt))
                    s_ref[off] = s_ref[off] + bt_ref[1]
            for k0, k1, q0, q1 in diag_parts:
                for a in range(k0 // bt, k1 // bt):
                    for b in range(q0 // bt, q1 // bt):
                        if a > b:
                            s_ref[rows(a * bt, (a + 1) * bt), b * bt:(b + 1) * bt] = (
                                jnp.full((bt, bt), -jnp.inf, F32))

    def softmax_pv(c, step):
        i, j, kind = step
        for k0, k1, q0, q1 in (diag_parts if kind == "diag" else full):
            s = s_refs[c][rows(k0, k1), q0:q1]
            m_blk = jnp.max(s, axis=0, keepdims=True)
            if j == 0:
                m_new = m_blk
            else:
                m_old = m_ref[i, c, :, q0:q1]
                m_new = jnp.maximum(m_old, m_blk)
            pv = _dot(v_operand(j, k0, k1), jnp.exp2(s - m_new).astype(BF16))
            if j == 0:
                acc_ref[i, c, :, q0:q1] = pv
            else:
                acc_ref[i, c, :, q0:q1] = (acc_ref[i, c, :, q0:q1]
                                           * jnp.exp2(m_old - m_new) + pv)
            m_ref[i, c, :, q0:q1] = m_new

    steps = []
    for j in range(nb):
        for i in range(j, nb):
            steps.append((i, j, "diag" if i == j else "near" if i == j + 1 else "far"))

    scores(0, steps[0])
    for t, step in enumerate(steps):
        scores(1, step)
        add_bias(0, step[2])
        softmax_pv(0, step)
        if t + 1 < len(steps):
            scores(0, steps[t + 1])
        add_bias(1, step[2])
        softmax_pv(1, step)

    lp = lam_ref[...]
    lam = (jnp.exp(jnp.sum(lp[0:1] * lp[1:2], axis=-1, keepdims=True))
           - jnp.exp(jnp.sum(lp[2:3] * lp[3:4], axis=-1, keepdims=True))
           + lambda_init)
    for i in range(nb):
        a0 = acc_ref[i, 0, :, 0:tb]
        a1 = acc_ref[i, 1, :, 0:tb]
        o = a0[0:dv] / a0[dv:dv + 1] - lam * (a1[0:dv] / a1[dv:dv + 1])
        o = o * lax.rsqrt(jnp.mean(o * o, axis=0, keepdims=True) + EPS)
        o = o * sw_ref[...] * (1.0 - lambda_init)
        o_ref[i * tb:(i + 1) * tb, :] = o.T.astype(BF16)


def _mixers_kernel(qt_ref, k_ref, vt_ref, bt_ref, lam_ref, sw_ref, xbc_ref, cum_ref,
                   src_ref, o_attn_ref, o_scan_ref, s0_ref, s1_ref, m_ref, acc_ref,
                   state_ref, *, lambda_init, seq):
    @pl.when(pl.program_id(1) == 0)
    def _():
        state_ref[...] = jnp.zeros_like(state_ref)

    _ssd_step(xbc_ref, cum_ref, src_ref, o_scan_ref, state_ref)
    _attn_kernel(qt_ref, k_ref, vt_ref, bt_ref, lam_ref, sw_ref, o_attn_ref,
                 s0_ref, s1_ref, m_ref, acc_ref, lambda_init=lambda_init, seq=seq)


def _mixers(q_t, k, v_t, tiles, lam_p, sw, xbc, cum, src, *, batch, seq, lambda_init):
    nb = seq // ATTN_BLOCK
    step_rows = SSD_CHUNKS_PER_STEP * SSD_CHUNK
    assert seq // step_rows == DA_HEADS
    seq_head = pl.BlockSpec((seq, DA_V_DIM), lambda b, h: (b, h))
    head_seq = pl.BlockSpec((DA_V_DIM, seq), lambda b, h: (h, b))
    acc_rows = DA_V_DIM + ATTN_ONES_ROWS

    def scan_rows(n):
        return pl.BlockSpec((step_rows, n), lambda b, h: (b * DA_HEADS + h, 0))

    return pl.pallas_call(
        functools.partial(_mixers_kernel, lambda_init=lambda_init, seq=seq),
        out_shape=(jax.ShapeDtypeStruct((batch * seq, DA_WIDTH), BF16),
                   jax.ShapeDtypeStruct((batch * seq, SSD_WIDTH), BF16)),
        grid=(batch, DA_HEADS),
        in_specs=[head_seq, seq_head, head_seq,
                  pl.BlockSpec((None, 2, BIAS_TILE, BIAS_TILE),
                               lambda b, h: (h, 0, 0, 0)),
                  _resident((4, DA_HEAD_DIM)), _resident((DA_V_DIM, ATTN_BLOCK)),
                  scan_rows(CONV_CH), scan_rows(DT_PAD), scan_rows(DT_PAD)],
        out_specs=(seq_head, scan_rows(SSD_WIDTH)),
        scratch_shapes=[pltpu.VMEM((ATTN_BLOCK, ATTN_BLOCK + LANES), F32),
                        pltpu.VMEM((ATTN_BLOCK, ATTN_BLOCK + LANES), F32),
                        pltpu.VMEM((nb, 2, 1, ATTN_BLOCK), F32),
                        pltpu.VMEM((nb, 2, acc_rows, ATTN_BLOCK + LANES), F32),
                        pltpu.VMEM((SSD_HEADS // 2, SSD_STATE, 2 * SSD_HEAD_DIM), F32)],
        compiler_params=_params("parallel", "arbitrary"),
        name="mixers",
    )(q_t, k, v_t, tiles, lam_p, sw, xbc, cum, src)


def _mem_kv_kernel(mem_ref, nw_ref, wk_ref, wv_ref, k_ref, v_ref):
    mn = _rms(mem_ref[...], nw_ref[...])
    k_ref[...] = _dot(mn, wk_ref[...]).astype(BF16)
    v_ref[...] = _dot(mn, wv_ref[...]).astype(BF16)


def _mem_kv(mem, nw, wk, wv):
    t = mem.shape[0]
    row = pl.BlockSpec((MEM_LEN, D_MODEL), lambda i: (i, 0))
    return pl.pallas_call(
        _mem_kv_kernel,
        out_shape=(jax.ShapeDtypeStruct((t, D_MODEL), BF16),
                   jax.ShapeDtypeStruct((t, D_MODEL), BF16)),
        grid=(t // MEM_LEN,),
        in_specs=[row, _resident((1, D_MODEL)), _resident((D_MODEL, D_MODEL)),
                  _resident((D_MODEL, D_MODEL))],
        out_specs=(row, row),
        compiler_params=_params("parallel"),
        name="mem_kv",
    )(mem, nw, wk, wv)


def _mix_out_cross_kernel(x_ref, ys_ref, xs_ref, z_ref, ya_ref, dskip_ref, snw_ref,
                          ws_ref, wa_ref, nw_ref, wq_ref, k_ref, v_ref, wo_ref, o_ref):
    y_ssd = _ssd_gate_norm(ys_ref[...], xs_ref[...], z_ref[...], dskip_ref[...],
                           snw_ref[...])
    x = (x_ref[...] + _dot(ya_ref[...].astype(F32), wa_ref[...])
         + _dot(y_ssd, ws_ref[...]))
    h = _rms(x, nw_ref[...])
    q = (_dot(h, wq_ref[...]) * (CROSS_HEAD_DIM ** -0.5)).astype(BF16)
    outs = []
    for hh in range(CROSS_HEADS):
        sl = slice(hh * CROSS_HEAD_DIM, (hh + 1) * CROSS_HEAD_DIM)
        s = _dot_nt(q[:, sl], k_ref[:, sl])
        p = jnp.exp(s - jnp.max(s, axis=-1, keepdims=True))
        o = _dot(p.astype(BF16), v_ref[:, sl])
        outs.append(o / jnp.sum(p, axis=-1, keepdims=True))
    o_ref[...] = x + _dot(jnp.concatenate(outs, axis=-1), wo_ref[...])


def _mix_out_cross(x1, y_scan, xbc, z, y_attn, dskip, snw, w_out, nw, wq, k, v, wo,
                   *, seq):
    t = x1.shape[0]
    tiles_per_batch = seq // ROW_TILE
    row = pl.BlockSpec((ROW_TILE, D_MODEL), lambda i: (i, 0))
    mem = pl.BlockSpec((MEM_LEN, D_MODEL), lambda i: (i // tiles_per_batch, 0))
    assert SSD_WIDTH == D_MODEL
    assert SSD_WIDTH == DA_WIDTH

    def w_half(j):
        return pl.BlockSpec((SSD_WIDTH, D_MODEL), lambda i: (j, 0),
                            pipeline_mode=pl.Buffered(1))

    return pl.pallas_call(
        _mix_out_cross_kernel,
        out_shape=jax.ShapeDtypeStruct((t, D_MODEL), F32),
        grid=(t // ROW_TILE,),
        in_specs=[row, row, row, row, row, _resident((1, SSD_WIDTH)),
                  _resident((1, SSD_WIDTH)), w_half(0), w_half(1),
                  _resident((1, D_MODEL)), _resident((D_MODEL, D_MODEL)), mem, mem,
                  _resident((D_MODEL, D_MODEL))],
        out_specs=row,
        compiler_params=_params("parallel"),
        name="mix_out_cross",
    )(x1, y_scan, xbc, z, y_attn, dskip, snw, w_out, w_out, nw, wq, k, v, wo)


def kernel(x, mem, norm_ffn1_w, ffn1_w_in, ffn1_w_out, norm_mix_w, w_in_mix, conv_w, conv_b, dt_bias, a_log, d_skip, ssd_norm_w, lambda_q1, lambda_k1, lambda_q2, lambda_k2, subln_w, rel_bias, w_out_mix, norm_cross_w, norm_mem_w, w_cq, w_ck, w_cv, w_co, norm_ffn2_w, ffn2_w_in, ffn2_w_out, norm_final_w):
    batch, seq, _ = x.shape
    depth = ffn1_w_in.shape[0]
    t = batch * seq
    xf = x.reshape(t, D_MODEL)
    memf = mem.reshape(batch * MEM_LEN, D_MODEL)
    fw = norm_final_w.reshape(1, D_MODEL)
    tiles = _rel_bias_tiles(rel_bias)

    def vec(p):
        return p.reshape(1, -1)

    def pad_lanes(p):
        return jnp.pad(p.reshape(1, -1), ((0, 0), (0, DT_PAD - p.shape[-1])))

    for l in range(depth):
        lambda_init = 0.8 - 0.6 * math.exp(-0.3 * l)
        xf = _ffn(xf, vec(norm_ffn1_w[l]), ffn1_w_in[l], ffn1_w_out[l], fw,
                  final=False, name="ffn1")

        z, xbc, cum, src, q_t, k, v_t = _in_proj(
            xf, vec(norm_mix_w[l]), w_in_mix[l].T,
            conv_w[l], vec(conv_b[l]), pad_lanes(dt_bias[l]), pad_lanes(a_log[l]),
            seq=seq)

        lam_p = jnp.stack([lambda_q1[l], lambda_k1[l], lambda_q2[l], lambda_k2[l]])
        sw = jnp.broadcast_to(subln_w[l].reshape(DA_V_DIM, 1), (DA_V_DIM, ATTN_BLOCK))
        y_attn, y_scan = _mixers(q_t, k, v_t, tiles, lam_p, sw, xbc, cum, src,
                                 batch=batch, seq=seq, lambda_init=lambda_init)

        mk, mv = _mem_kv(memf, vec(norm_mem_w[l]), w_ck[l], w_cv[l])
        xf = _mix_out_cross(xf, y_scan, xbc, z, y_attn,
                            vec(jnp.repeat(d_skip[l], SSD_HEAD_DIM)),
                            vec(ssd_norm_w[l]), w_out_mix[l],
                            vec(norm_cross_w[l]), w_cq[l], mk, mv, w_co[l], seq=seq)

        xf = _ffn(xf, vec(norm_ffn2_w[l]), ffn2_w_in[l], ffn2_w_out[l], fw,
                  final=(l == depth - 1), name="ffn2")
    return xf.reshape(batch, seq, D_MODEL)
```

```python
import functools
import math

import jax
import jax.numpy as jnp
from jax import lax
from jax.experimental import pallas as pl
from jax.experimental.pallas import tpu as pltpu

F32 = jnp.float32
BF16 = jnp.bfloat16

D_MODEL = 1024
MEM_LEN = 256
EPS = 1e-6

SSD_HEADS = 16
SSD_HEAD_DIM = 64
SSD_WIDTH = SSD_HEADS * SSD_HEAD_DIM
SSD_GROUPS = 4
SSD_STATE = 128
CONV_WIDTH = 4
SSD_CHUNK = 128
CONV_CH = SSD_WIDTH + 2 * SSD_GROUPS * SSD_STATE
HEADS_PER_GROUP = SSD_HEADS // SSD_GROUPS

DA_HEADS = 8
DA_HEAD_DIM = 64
DA_V_DIM = 2 * DA_HEAD_DIM
DA_WIDTH = DA_HEADS * DA_V_DIM

NUM_BUCKETS = 32
MAX_DISTANCE = 128

CROSS_HEADS = 4
CROSS_HEAD_DIM = D_MODEL // CROSS_HEADS

D_FF = 2816
LOG2_E = math.log2(math.e)

LANES = 128
SUBLANES = 8
MXU_DIM = 256
VMEM_LIMIT_BYTES = 56 * 1024 * 1024

ROW_TILE = 512
FF_CHUNK = MXU_DIM
ATTN_BLOCK = 512
ATTN_ONES_ROWS = 16
BIAS_TILE = MAX_DISTANCE
DT_PAD = LANES
CONV_HALO = SUBLANES
SSD_CHUNKS_PER_STEP = 4


def _resident(shape):
    nd = len(shape)
    return pl.BlockSpec(shape, lambda *_: (0,) * nd, pipeline_mode=pl.Buffered(1))


def _params(*sem):
    return pltpu.CompilerParams(dimension_semantics=sem,
                                vmem_limit_bytes=VMEM_LIMIT_BYTES)


def _rms(x, w):
    ms = jnp.mean(x * x, axis=-1, keepdims=True)
    return x * lax.rsqrt(ms + EPS) * w


def _silu(x):
    h = 0.5 * x
    return h + h * jnp.tanh(h)


def _dot(a, b):
    return jnp.dot(a, b, preferred_element_type=F32)


def _dot_nt(a, b):
    return lax.dot_general(a, b, (((1,), (1,)), ((), ())),
                           preferred_element_type=F32)


def _ffn_kernel(x_ref, nw_ref, win_ref, wout_ref, fw_ref, o_ref, *, final):
    x = x_ref[...]
    h = _rms(x, nw_ref[...])
    acc = None
    for c in range(D_FF // FF_CHUNK):
        lo = c * FF_CHUNK
        g = _dot(h, win_ref[:, lo:lo + FF_CHUNK])
        u = _dot(h, win_ref[:, D_FF + lo:D_FF + lo + FF_CHUNK])
        a = _silu(g) * u
        d = _dot(a, wout_ref[lo:lo + FF_CHUNK, :])
        acc = d if acc is None else acc + d
    y = x + 0.5 * acc
    if final:
        y = _rms(y, fw_ref[...])
    o_ref[...] = y


def _ffn(x, nw, win, wout, fw, *, final, name):
    t = x.shape[0]
    row = pl.BlockSpec((ROW_TILE, D_MODEL), lambda i: (i, 0))
    return pl.pallas_call(
        functools.partial(_ffn_kernel, final=final),
        out_shape=jax.ShapeDtypeStruct((t, D_MODEL), F32),
        grid=(t // ROW_TILE,),
        in_specs=[row, _resident((1, D_MODEL)), _resident((D_MODEL, 2 * D_FF)),
                  _resident((D_FF, D_MODEL)), _resident((1, D_MODEL))],
        out_specs=row,
        compiler_params=_params("parallel"),
        name=name,
    )(x, nw, win, wout, fw)


def _in_proj_kernel(x_ref, nw_ref, wt_ref, cw_ref, cb_ref, dtb_ref, alog_ref,
                    z_ref, xbc_ref, cum_ref, src_ref, qt_ref, k_ref, vt_ref, ext_ref,
                    *, tiles_per_seq):
    i = pl.program_id(0)
    tm = ROW_TILE
    h = _rms(x_ref[...], nw_ref[...])
    o_xbc = SSD_WIDTH
    o_dt = o_xbc + CONV_CH
    o_q = o_dt + SSD_HEADS
    o_k = o_q + DA_WIDTH
    o_v = o_k + DA_WIDTH

    def proj(lo, width):
        return _dot_nt(h, wt_ref[lo:lo + width, :])

    @pl.when(i % tiles_per_seq == 0)
    def _():
        ext_ref[:, 0:CONV_HALO, :] = jnp.zeros((CONV_CH // LANES, CONV_HALO, LANES), F32)

    x_dt = proj(o_dt, DT_PAD) + dtb_ref[...]
    dt = jnp.maximum(x_dt, 0.0) + jnp.log1p(jnp.exp(-jnp.abs(x_dt)))
    log2_dt = jnp.log2(dt)
    la = dt * (-LOG2_E * jnp.exp(alog_ref[...]))
    la_hi = la.astype(BF16)
    la_mid = (la - la_hi.astype(F32)).astype(BF16)
    la_lo = (la - la_hi.astype(F32) - la_mid.astype(F32)).astype(BF16)
    row_i = lax.broadcasted_iota(jnp.int32, (SSD_CHUNK, SSD_CHUNK), 0)
    col_i = lax.broadcasted_iota(jnp.int32, (SSD_CHUNK, SSD_CHUNK), 1)
    tril = (col_i <= row_i).astype(BF16)

    wide = ((z_ref, 0, SSD_WIDTH, 1.0, False),
            (qt_ref, o_q, DA_WIDTH, DA_HEAD_DIM ** -0.5 * LOG2_E, True),
            (k_ref, o_k, DA_WIDTH, 1.0, False), (vt_ref, o_v, DA_WIDTH, 1.0, True))
    n_chunks = tm // SSD_CHUNK
    assert n_chunks == len(wide)
    u = proj(o_xbc, CONV_CH)
    for t in range(CONV_CH // LANES):
        ext_ref[t, CONV_HALO:CONV_HALO + tm, :] = u[:, t * LANES:(t + 1) * LANES]
    for r in range(n_chunks):
        lo = r * SSD_CHUNK
        rows = slice(lo, lo + SSD_CHUNK)
        cum = (_dot(tril, la_hi[rows]) + _dot(tril, la_mid[rows])
               + _dot(tril, la_lo[rows]))
        cum_ref[rows, :] = cum
        src_ref[rows, :] = cum - log2_dt[rows]
        base = CONV_HALO + lo
        acts = []
        for t in range(CONV_CH // LANES):
            ln = slice(t * LANES, (t + 1) * LANES)
            acc = cb_ref[:, ln]
            for j in range(CONV_WIDTH):
                start = base - (CONV_WIDTH - 1) + j
                acc = acc + cw_ref[j:j + 1, ln] * ext_ref[t, start:start + SSD_CHUNK, :]
            acts.append(_silu(acc))
        xbc_ref[rows, :] = jnp.concatenate(acts, axis=-1).astype(BF16)
        o_ref, w_lo, w_width, scale, feature_major = wide[r]
        if feature_major:
            out = _dot_nt(wt_ref[w_lo:w_lo + w_width, :], h)
        else:
            out = proj(w_lo, w_width)
        o_ref[...] = (out if scale == 1.0 else out * scale).astype(BF16)
    ext_ref[:, 0:CONV_HALO, :] = ext_ref[:, tm:tm + CONV_HALO, :]


def _in_proj(x1, nw, w_t, cw, cb, dtb, alog, *, seq):
    t = x1.shape[0]
    in_width = w_t.shape[0]
    assert (SSD_WIDTH + CONV_CH + SSD_HEADS) % SUBLANES == 0
    assert SSD_WIDTH + CONV_CH + DT_PAD <= in_width

    def row(n):
        return pl.BlockSpec((ROW_TILE, n), lambda i: (i, 0))

    def col(n):
        return pl.BlockSpec((n, ROW_TILE), lambda i: (0, i))

    return pl.pallas_call(
        functools.partial(_in_proj_kernel, tiles_per_seq=seq // ROW_TILE),
        out_shape=(jax.ShapeDtypeStruct((t, SSD_WIDTH), BF16),
                   jax.ShapeDtypeStruct((t, CONV_CH), BF16),
                   jax.ShapeDtypeStruct((t, DT_PAD), F32),
                   jax.ShapeDtypeStruct((t, DT_PAD), F32),
                   jax.ShapeDtypeStruct((DA_WIDTH, t), BF16),
                   jax.ShapeDtypeStruct((t, DA_WIDTH), BF16),
                   jax.ShapeDtypeStruct((DA_WIDTH, t), BF16)),
        grid=(t // ROW_TILE,),
        in_specs=[row(D_MODEL), _resident((1, D_MODEL)),
                  _resident((in_width, D_MODEL)),
                  _resident((CONV_WIDTH, CONV_CH)), _resident((1, CONV_CH)),
                  _resident((1, DT_PAD)), _resident((1, DT_PAD))],
        out_specs=(row(SSD_WIDTH), row(CONV_CH), row(DT_PAD), row(DT_PAD),
                   col(DA_WIDTH), row(DA_WIDTH), col(DA_WIDTH)),
        scratch_shapes=[pltpu.VMEM((CONV_CH // LANES, CONV_HALO + ROW_TILE + CONV_HALO,
                                    LANES), F32)],
        compiler_params=_params("arbitrary"),
        name="in_proj",
    )(x1, nw, w_t, cw, cb, dtb, alog)


def _ssd_step(xbc_ref, cum_ref, src_ref, o_ref, state_ref):
    L = SSD_CHUNK
    hp = SSD_HEAD_DIM
    row_i = lax.broadcasted_iota(jnp.int32, (L, L), 0)
    col_i = lax.broadcasted_iota(jnp.int32, (L, L), 1)
    causal = col_i <= row_i
    first = col_i < hp
    keep = (first.astype(BF16), 1.0 - first.astype(BF16))
    for r in range(SSD_CHUNKS_PER_STEP):
        rows = slice(r * SSD_CHUNK, (r + 1) * SSD_CHUNK)
        cum_col = cum_ref[rows, :]
        src_row = src_ref[rows, :].T
        for g in range(SSD_GROUPS):
            b_lo = SSD_WIDTH + g * SSD_STATE
            c_lo = SSD_WIDTH + (SSD_GROUPS + g) * SSD_STATE
            b_g = xbc_ref[rows, b_lo:b_lo + SSD_STATE]
            c_g = xbc_ref[rows, c_lo:c_lo + SSD_STATE]
            cb = _dot_nt(c_g, b_g)
            bt_g = b_g.astype(F32).T
            for pr in range(HEADS_PER_GROUP // 2):
                pair = g * (HEADS_PER_GROUP // 2) + pr
                lanes = slice(pair * 2 * hp, (pair + 1) * 2 * hp)
                x_pair = xbc_ref[rows, lanes]
                x_bd = jnp.concatenate([x_pair * keep[0], x_pair * keep[1]], axis=0)
                cols, w_parts, b_parts = [], [], []
                for t in range(2):
                    h = 2 * pair + t
                    col_b = jnp.broadcast_to(cum_col[:, h:h + 1], (L, L))
                    row_b = jnp.broadcast_to(src_row[h:h + 1, :], (L, L))
                    dec = jnp.exp2(jnp.where(causal, col_b - row_b, -jnp.inf))
                    w_parts.append((cb * dec).astype(BF16))
                    b_parts.append((bt_g * dec[L - 1:L, :]).astype(BF16))
                    cols.append(col_b)
                from_start = jnp.exp2(jnp.where(first, cols[0], cols[1]))
                s_prev = state_ref[pair]
                y = (_dot(jnp.concatenate(w_parts, axis=1), x_bd)
                     + _dot(c_g, s_prev.astype(BF16)) * from_start)
                o_ref[rows, lanes] = y.astype(BF16)
                new = _dot(jnp.concatenate(b_parts, axis=1), x_bd)
                state_ref[pair] = s_prev * from_start[L - 1:L, :] + new


def _ssd_gate_norm(y, xs, z, dskip, nw):
    y = y.astype(F32) + dskip * xs.astype(F32)
    y = y * _silu(z.astype(F32))
    gw = SSD_WIDTH // SSD_GROUPS
    outs = []
    for g in range(SSD_GROUPS):
        yg = y[:, g * gw:(g + 1) * gw]
        ms = jnp.mean(yg * yg, axis=-1, keepdims=True)
        outs.append(yg * lax.rsqrt(ms + EPS))
    return jnp.concatenate(outs, axis=-1) * nw


def _rel_bias_kernel(rb_ref, o_ref):
    k_i = lax.broadcasted_iota(jnp.int32, (BIAS_TILE, BIAS_TILE), 0)
    q_i = lax.broadcasted_iota(jnp.int32, (BIAS_TILE, BIAS_TILE), 1)
    max_exact = NUM_BUCKETS // 2
    for d in range(2):
        n = q_i - k_i + d * BIAS_TILE
        nf = jnp.maximum(n, 1).astype(F32)
        large = max_exact + (jnp.log(nf / max_exact) / math.log(MAX_DISTANCE / max_exact)
                             * (NUM_BUCKETS - max_exact)).astype(jnp.int32)
        large = jnp.minimum(large, NUM_BUCKETS - 1)
        bucket = jnp.where(n < max_exact, n, large)
        hits = [bucket == b for b in range(NUM_BUCKETS)]
        for h in range(DA_HEADS):
            tile = jnp.zeros((BIAS_TILE, BIAS_TILE), F32)
            for b in range(NUM_BUCKETS):
                tile = jnp.where(hits[b], rb_ref[b, h], tile)
            tile = (tile - rb_ref[NUM_BUCKETS - 1, h]) * LOG2_E
            o_ref[h, d] = jnp.where(n >= 0, tile, -jnp.inf)


def _rel_bias_tiles(rel_bias):
    return pl.pallas_call(
        _rel_bias_kernel,
        out_shape=jax.ShapeDtypeStruct((DA_HEADS, 2, BIAS_TILE, BIAS_TILE), F32),
        in_specs=[pl.BlockSpec(memory_space=pltpu.SMEM)],
        out_specs=pl.BlockSpec(memory_space=pltpu.VMEM),
        compiler_params=_params(),
        name="rel_bias",
    )(rel_bias)


def _attn_kernel(qt_ref, k_ref, vt_ref, bt_ref, lam_ref, sw_ref, o_ref,
                 s0_ref, s1_ref, m_ref, acc_ref, *, lambda_init, seq):
    tb = ATTN_BLOCK
    dh = DA_HEAD_DIM
    dv = DA_V_DIM
    nb = seq // tb

    def q_operand(i, c, q0, q1):
        blk = qt_ref[c * dh:(c + 1) * dh, i * tb + q0:i * tb + q1]
        pad = jnp.zeros_like(blk)
        return jnp.concatenate([blk, pad] if c == 0 else [pad, blk], axis=0)

    def v_operand(j, k0, k1):
        ones = jnp.ones((ATTN_ONES_ROWS, k1 - k0), BF16)
        return jnp.concatenate([vt_ref[:, j * tb + k0:j * tb + k1], ones], axis=0)

    s_refs = (s0_ref, s1_ref)
    half = tb // 2
    bt = BIAS_TILE
    nsub = tb // bt
    full = ((0, tb, 0, tb),)
    diag_parts = ((0, half, 0, half), (0, tb, half, tb))

    zero = jnp.minimum(pl.program_id(0), 0)

    def rows(lo, hi):
        return pl.ds(pl.multiple_of(zero + lo, BIAS_TILE), hi - lo)

    def scores(c, step):
        i, j, kind = step
        k0 = j * tb
        if kind != "diag":
            s_refs[c][rows(0, tb), 0:tb] = _dot(k_ref[k0:k0 + tb, :],
                                                q_operand(i, c, 0, tb))
        else:
            s_refs[c][rows(0, half), 0:tb] = _dot(k_ref[k0:k0 + half, :],
                                                  q_operand(i, c, 0, tb))
            s_refs[c][rows(half, tb), half:tb] = _dot(k_ref[k0 + half:k0 + tb, :],
                                                      q_operand(i, c, half, tb))

    def add_bias(c, kind):
        s_ref = s_refs[c]
        if kind == "near":
            near = (rows(tb - bt, tb), slice(0, bt))
            s_ref[near] = s_ref[near] + bt_ref[1]
        elif kind == "diag":
            for a in range(nsub):
                on = (rows(a * bt, (a + 1) * bt), slice(a * bt, (a + 1) * bt))
                s_ref[on] = s_ref[on] + bt_ref[0]
                if a + 1 < nsub:
                    off = (rows(a * bt, (a + 1) * bt), slice((a + 1) * bt, (a + 2) * bt))
                    s_ref[off] = s_ref[off] + bt_ref[1]
            for k0, k1, q0, q1 in diag_parts:
                for a in range(k0 // bt, k1 // bt):
                    for b in range(q0 // bt, q1 // bt):
                        if a > b:
                            s_ref[rows(a * bt, (a + 1) * bt), b * bt:(b + 1) * bt] = (
                                jnp.full((bt, bt), -jnp.inf, F32))

    def softmax_pv(c, step):
        i, j, kind = step
        for k0, k1, q0, q1 in (diag_parts if kind == "diag" else full):
            s = s_refs[c][rows(k0, k1), q0:q1]
            m_blk = jnp.max(s, axis=0, keepdims=True)
            if j == 0:
                m_new = m_blk
            else:
                m_old = m_ref[i, c, :, q0:q1]
                m_new = jnp.maximum(m_old, m_blk)
            pv = _dot(v_operand(j, k0, k1), jnp.exp2(s - m_new).astype(BF16))
            if j == 0:
                acc_ref[i, c, :, q0:q1] = pv
            else:
                acc_ref[i, c, :, q0:q1] = (acc_ref[i, c, :, q0:q1]
                                           * jnp.exp2(m_old - m_new) + pv)
            m_ref[i, c, :, q0:q1] = m_new

    steps = []
    for j in range(nb):
        for i in range(j, nb):
            steps.append((i, j, "diag" if i == j else "near" if i == j + 1 else "far"))

    scores(0, steps[0])
    for t, step in enumerate(steps):
        scores(1, step)
        add_bias(0, step[2])
        softmax_pv(0, step)
        if t + 1 < len(steps):
            scores(0, steps[t + 1])
        add_bias(1, step[2])
        softmax_pv(1, step)

    lp = lam_ref[...]
    lam = (jnp.exp(jnp.sum(lp[0:1] * lp[1:2], axis=-1, keepdims=True))
           - jnp.exp(jnp.sum(lp[2:3] * lp[3:4], axis=-1, keepdims=True))
           + lambda_init)
    for i in range(nb):
        a0 = acc_ref[i, 0]
        a1 = acc_ref[i, 1]
        o = a0[0:dv] / a0[dv:dv + 1] - lam * (a1[0:dv] / a1[dv:dv + 1])
        o = o * lax.rsqrt(jnp.mean(o * o, axis=0, keepdims=True) + EPS)
        o = o * sw_ref[...] * (1.0 - lambda_init)
        o_ref[i * tb:(i + 1) * tb, :] = o.T.astype(BF16)


def _mixers_kernel(qt_ref, k_ref, vt_ref, bt_ref, lam_ref, sw_ref, xbc_ref, cum_ref,
                   src_ref, o_attn_ref, o_scan_ref, s0_ref, s1_ref, m_ref, acc_ref,
                   state_ref, *, lambda_init, seq):
    @pl.when(pl.program_id(1) == 0)
    def _():
        state_ref[...] = jnp.zeros_like(state_ref)

    _ssd_step(xbc_ref, cum_ref, src_ref, o_scan_ref, state_ref)
    _attn_kernel(qt_ref, k_ref, vt_ref, bt_ref, lam_ref, sw_ref, o_attn_ref,
                 s0_ref, s1_ref, m_ref, acc_ref, lambda_init=lambda_init, seq=seq)


def _mixers(q_t, k, v_t, tiles, lam_p, sw, xbc, cum, src, *, batch, seq, lambda_init):
    nb = seq // ATTN_BLOCK
    step_rows = SSD_CHUNKS_PER_STEP * SSD_CHUNK
    assert seq // step_rows == DA_HEADS
    seq_head = pl.BlockSpec((seq, DA_V_DIM), lambda b, h: (b, h))
    head_seq = pl.BlockSpec((DA_V_DIM, seq), lambda b, h: (h, b))
    acc_rows = DA_V_DIM + ATTN_ONES_ROWS

    def scan_rows(n):
        return pl.BlockSpec((step_rows, n), lambda b, h: (b * DA_HEADS + h, 0))

    return pl.pallas_call(
        functools.partial(_mixers_kernel, lambda_init=lambda_init, seq=seq),
        out_shape=(jax.ShapeDtypeStruct((batch * seq, DA_WIDTH), BF16),
                   jax.ShapeDtypeStruct((batch * seq, SSD_WIDTH), BF16)),
        grid=(batch, DA_HEADS),
        in_specs=[head_seq, seq_head, head_seq,
                  pl.BlockSpec((None, 2, BIAS_TILE, BIAS_TILE),
                               lambda b, h: (h, 0, 0, 0)),
                  _resident((4, DA_HEAD_DIM)), _resident((DA_V_DIM, ATTN_BLOCK)),
                  scan_rows(CONV_CH), scan_rows(DT_PAD), scan_rows(DT_PAD)],
        out_specs=(seq_head, scan_rows(SSD_WIDTH)),
        scratch_shapes=[pltpu.VMEM((ATTN_BLOCK, ATTN_BLOCK + LANES), F32),
                        pltpu.VMEM((ATTN_BLOCK, ATTN_BLOCK + LANES), F32),
                        pltpu.VMEM((nb, 2, 1, ATTN_BLOCK), F32),
                        pltpu.VMEM((nb, 2, acc_rows, ATTN_BLOCK), F32),
                        pltpu.VMEM((SSD_HEADS // 2, SSD_STATE, 2 * SSD_HEAD_DIM), F32)],
        compiler_params=_params("parallel", "arbitrary"),
        name="mixers",
    )(q_t, k, v_t, tiles, lam_p, sw, xbc, cum, src)


def _mem_kv_kernel(mem_ref, nw_ref, wk_ref, wv_ref, k_ref, v_ref):
    mn = _rms(mem_ref[...], nw_ref[...])
    k_ref[...] = _dot(mn, wk_ref[...]).astype(BF16)
    v_ref[...] = _dot(mn, wv_ref[...]).astype(BF16)


def _mem_kv(mem, nw, wk, wv):
    t = mem.shape[0]
    row = pl.BlockSpec((MEM_LEN, D_MODEL), lambda i: (i, 0))
    return pl.pallas_call(
        _mem_kv_kernel,
        out_shape=(jax.ShapeDtypeStruct((t, D_MODEL), BF16),
                   jax.ShapeDtypeStruct((t, D_MODEL), BF16)),
        grid=(t // MEM_LEN,),
        in_specs=[row, _resident((1, D_MODEL)), _resident((D_MODEL, D_MODEL)),
                  _resident((D_MODEL, D_MODEL))],
        out_specs=(row, row),
        compiler_params=_params("parallel"),
        name="mem_kv",
    )(mem, nw, wk, wv)


def _mix_out_cross_kernel(x_ref, ys_ref, xs_ref, z_ref, ya_ref, dskip_ref, snw_ref,
                          ws_ref, wa_ref, nw_ref, wq_ref, k_ref, v_ref, wo_ref, o_ref):
    y_ssd = _ssd_gate_norm(ys_ref[...], xs_ref[...], z_ref[...], dskip_ref[...],
                           snw_ref[...])
    x = (x_ref[...] + _dot(ya_ref[...].astype(F32), wa_ref[...])
         + _dot(y_ssd, ws_ref[...]))
    h = _rms(x, nw_ref[...])
    q = (_dot(h, wq_ref[...]) * (CROSS_HEAD_DIM ** -0.5)).astype(BF16)
    outs = []
    for hh in range(CROSS_HEADS):
        sl = slice(hh * CROSS_HEAD_DIM, (hh + 1) * CROSS_HEAD_DIM)
        s = _dot_nt(q[:, sl], k_ref[:, sl])
        p = jnp.exp(s - jnp.max(s, axis=-1, keepdims=True))
        o = _dot(p.astype(BF16), v_ref[:, sl])
        outs.append(o / jnp.sum(p, axis=-1, keepdims=True))
    o_ref[...] = x + _dot(jnp.concatenate(outs, axis=-1), wo_ref[...])


def _mix_out_cross(x1, y_scan, xbc, z, y_attn, dskip, snw, w_out, nw, wq, k, v, wo,
                   *, seq):
    t = x1.shape[0]
    tiles_per_batch = seq // ROW_TILE
    row = pl.BlockSpec((ROW_TILE, D_MODEL), lambda i: (i, 0))
    mem = pl.BlockSpec((MEM_LEN, D_MODEL), lambda i: (i // tiles_per_batch, 0))
    assert SSD_WIDTH == D_MODEL
    assert SSD_WIDTH == DA_WIDTH

    def w_half(j):
        return pl.BlockSpec((SSD_WIDTH, D_MODEL), lambda i: (j, 0),
                            pipeline_mode=pl.Buffered(1))

    return pl.pallas_call(
        _mix_out_cross_kernel,
        out_shape=jax.ShapeDtypeStruct((t, D_MODEL), F32),
        grid=(t // ROW_TILE,),
        in_specs=[row, row, row, row, row, _resident((1, SSD_WIDTH)),
                  _resident((1, SSD_WIDTH)), w_half(0), w_half(1),
                  _resident((1, D_MODEL)), _resident((D_MODEL, D_MODEL)), mem, mem,
                  _resident((D_MODEL, D_MODEL))],
        out_specs=row,
        compiler_params=_params("parallel"),
        name="mix_out_cross",
    )(x1, y_scan, xbc, z, y_attn, dskip, snw, w_out, w_out, nw, wq, k, v, wo)


def kernel(x, mem, norm_ffn1_w, ffn1_w_in, ffn1_w_out, norm_mix_w, w_in_mix, conv_w, conv_b, dt_bias, a_log, d_skip, ssd_norm_w, lambda_q1, lambda_k1, lambda_q2, lambda_k2, subln_w, rel_bias, w_out_mix, norm_cross_w, norm_mem_w, w_cq, w_ck, w_cv, w_co, norm_ffn2_w, ffn2_w_in, ffn2_w_out, norm_final_w):
    batch, seq, _ = x.shape
    depth = ffn1_w_in.shape[0]
    t = batch * seq
    xf = x.reshape(t, D_MODEL)
    memf = mem.reshape(batch * MEM_LEN, D_MODEL)
    fw = norm_final_w.reshape(1, D_MODEL)
    tiles = _rel_bias_tiles(rel_bias)

    def vec(p):
        return p.reshape(1, -1)

    def pad_lanes(p):
        return jnp.pad(p.reshape(1, -1), ((0, 0), (0, DT_PAD - p.shape[-1])))

    for l in range(depth):
        lambda_init = 0.8 - 0.6 * math.exp(-0.3 * l)
        xf = _ffn(xf, vec(norm_ffn1_w[l]), ffn1_w_in[l], ffn1_w_out[l], fw,
                  final=False, name="ffn1")

        z, xbc, cum, src, q_t, k, v_t = _in_proj(
            xf, vec(norm_mix_w[l]), w_in_mix[l].T,
            conv_w[l], vec(conv_b[l]), pad_lanes(dt_bias[l]), pad_lanes(a_log[l]),
            seq=seq)

        lam_p = jnp.stack([lambda_q1[l], lambda_k1[l], lambda_q2[l], lambda_k2[l]])
        sw = jnp.broadcast_to(subln_w[l].reshape(DA_V_DIM, 1), (DA_V_DIM, ATTN_BLOCK))
        y_attn, y_scan = _mixers(q_t, k, v_t, tiles, lam_p, sw, xbc, cum, src,
                                 batch=batch, seq=seq, lambda_init=lambda_init)

        mk, mv = _mem_kv(memf, vec(norm_mem_w[l]), w_ck[l], w_cv[l])
        xf = _mix_out_cross(xf, y_scan, xbc, z, y_attn,
                            vec(jnp.repeat(d_skip[l], SSD_HEAD_DIM)),
                            vec(ssd_norm_w[l]), w_out_mix[l],
                            vec(norm_cross_w[l]), w_cq[l], mk, mv, w_co[l], seq=seq)

        xf = _ffn(xf, vec(norm_ffn2_w[l]), ffn2_w_in[l], ffn2_w_out[l], fw,
                  final=(l == depth - 1), name="ffn2")
    return xf.reshape(batch, seq, D_MODEL)
```

```python
import functools
import math

import jax
import jax.numpy as jnp
from jax import lax
from jax.experimental import pallas as pl
from jax.experimental.pallas import tpu as pltpu

F32 = jnp.float32
BF16 = jnp.bfloat16

D_MODEL = 1024
MEM_LEN = 256
EPS = 1e-6

SSD_HEADS = 16
SSD_HEAD_DIM = 64
SSD_WIDTH = SSD_HEADS * SSD_HEAD_DIM
SSD_GROUPS = 4
SSD_STATE = 128
CONV_WIDTH = 4
SSD_CHUNK = 128
CONV_CH = SSD_WIDTH + 2 * SSD_GROUPS * SSD_STATE
HEADS_PER_GROUP = SSD_HEADS // SSD_GROUPS

DA_HEADS = 8
DA_HEAD_DIM = 64
DA_V_DIM = 2 * DA_HEAD_DIM
DA_WIDTH = DA_HEADS * DA_V_DIM

NUM_BUCKETS = 32
MAX_DISTANCE = 128

CROSS_HEADS = 4
CROSS_HEAD_DIM = D_MODEL // CROSS_HEADS

D_FF = 2816
LOG2_E = math.log2(math.e)

LANES = 128
SUBLANES = 8
MXU_DIM = 256
VMEM_LIMIT_BYTES = 56 * 1024 * 1024

ROW_TILE = 512
FF_CHUNK = MXU_DIM
ATTN_BLOCK = 512
ATTN_ONES_ROWS = 16
BIAS_TILE = MAX_DISTANCE
DT_PAD = LANES
CONV_HALO = SUBLANES
SSD_CHUNKS_PER_STEP = 4


def _resident(shape):
    nd = len(shape)
    return pl.BlockSpec(shape, lambda *_: (0,) * nd, pipeline_mode=pl.Buffered(1))


def _params(*sem):
    return pltpu.CompilerParams(dimension_semantics=sem,
                                vmem_limit_bytes=VMEM_LIMIT_BYTES)


def _rms(x, w):
    ms = jnp.mean(x * x, axis=-1, keepdims=True)
    return x * lax.rsqrt(ms + EPS) * w


def _silu(x):
    h = 0.5 * x
    return h + h * jnp.tanh(h)


def _dot(a, b):
    return jnp.dot(a, b, preferred_element_type=F32)


def _dot_nt(a, b):
    return lax.dot_general(a, b, (((1,), (1,)), ((), ())),
                           preferred_element_type=F32)


def _ffn_kernel(x_ref, nw_ref, win_ref, wout_ref, fw_ref, o_ref, *, final):
    x = x_ref[...]
    h = _rms(x, nw_ref[...])
    acc = None
    for c in range(D_FF // FF_CHUNK):
        lo = c * FF_CHUNK
        g = _dot(h, win_ref[:, lo:lo + FF_CHUNK])
        u = _dot(h, win_ref[:, D_FF + lo:D_FF + lo + FF_CHUNK])
        a = _silu(g) * u
        d = _dot(a, wout_ref[lo:lo + FF_CHUNK, :])
        acc = d if acc is None else acc + d
    y = x + 0.5 * acc
    if final:
        y = _rms(y, fw_ref[...])
    o_ref[...] = y


def _ffn(x, nw, win, wout, fw, *, final, name):
    t = x.shape[0]
    row = pl.BlockSpec((ROW_TILE, D_MODEL), lambda i: (i, 0))
    return pl.pallas_call(
        functools.partial(_ffn_kernel, final=final),
        out_shape=jax.ShapeDtypeStruct((t, D_MODEL), F32),
        grid=(t // ROW_TILE,),
        in_specs=[row, _resident((1, D_MODEL)), _resident((D_MODEL, 2 * D_FF)),
                  _resident((D_FF, D_MODEL)), _resident((1, D_MODEL))],
        out_specs=row,
        compiler_params=_params("parallel"),
        name=name,
    )(x, nw, win, wout, fw)


def _in_proj_kernel(x_ref, nw_ref, wt_ref, cw_ref, cb_ref, dtb_ref, alog_ref,
                    z_ref, xbc_ref, cum_ref, src_ref, qt_ref, k_ref, vt_ref, ext_ref,
                    *, tiles_per_seq):
    i = pl.program_id(0)
    tm = ROW_TILE
    h = _rms(x_ref[...], nw_ref[...])
    o_xbc = SSD_WIDTH
    o_dt = o_xbc + CONV_CH
    o_q = o_dt + SSD_HEADS
    o_k = o_q + DA_WIDTH
    o_v = o_k + DA_WIDTH

    def proj(lo, width):
        return _dot_nt(h, wt_ref[lo:lo + width, :])

    @pl.when(i % tiles_per_seq == 0)
    def _():
        ext_ref[:, 0:CONV_HALO, :] = jnp.zeros((CONV_CH // LANES, CONV_HALO, LANES), F32)

    x_dt = proj(o_dt, DT_PAD) + dtb_ref[...]
    dt = jnp.maximum(x_dt, 0.0) + jnp.log1p(jnp.exp(-jnp.abs(x_dt)))
    log2_dt = jnp.log2(dt)
    la = dt * (-LOG2_E * jnp.exp(alog_ref[...]))
    la_hi = la.astype(BF16)
    la_mid = (la - la_hi.astype(F32)).astype(BF16)
    la_lo = (la - la_hi.astype(F32) - la_mid.astype(F32)).astype(BF16)
    row_i = lax.broadcasted_iota(jnp.int32, (SSD_CHUNK, SSD_CHUNK), 0)
    col_i = lax.broadcasted_iota(jnp.int32, (SSD_CHUNK, SSD_CHUNK), 1)
    tril = (col_i <= row_i).astype(BF16)

    wide = ((z_ref, 0, SSD_WIDTH, 1.0, False),
            (qt_ref, o_q, DA_WIDTH, DA_HEAD_DIM ** -0.5 * LOG2_E, True),
            (k_ref, o_k, DA_WIDTH, 1.0, False), (vt_ref, o_v, DA_WIDTH, 1.0, True))
    n_chunks = tm // SSD_CHUNK
    assert n_chunks == len(wide)
    u = proj(o_xbc, CONV_CH)
    for t in range(CONV_CH // LANES):
        ext_ref[t, CONV_HALO:CONV_HALO + tm, :] = u[:, t * LANES:(t + 1) * LANES]
    for r in range(n_chunks):
        lo = r * SSD_CHUNK
        rows = slice(lo, lo + SSD_CHUNK)
        cum = (_dot(tril, la_hi[rows]) + _dot(tril, la_mid[rows])
               + _dot(tril, la_lo[rows]))
        cum_ref[rows, :] = cum
        src_ref[rows, :] = cum - log2_dt[rows]
        base = CONV_HALO + lo
        acts = []
        for t in range(CONV_CH // LANES):
            ln = slice(t * LANES, (t + 1) * LANES)
            acc = cb_ref[:, ln]
            for j in range(CONV_WIDTH):
                start = base - (CONV_WIDTH - 1) + j
                acc = acc + cw_ref[j:j + 1, ln] * ext_ref[t, start:start + SSD_CHUNK, :]
            acts.append(_silu(acc))
        xbc_ref[rows, :] = jnp.concatenate(acts, axis=-1).astype(BF16)
        o_ref, w_lo, w_width, scale, feature_major = wide[r]
        if feature_major:
            out = _dot_nt(wt_ref[w_lo:w_lo + w_width, :], h)
        else:
            out = proj(w_lo, w_width)
        o_ref[...] = (out if scale == 1.0 else out * scale).astype(BF16)
    ext_ref[:, 0:CONV_HALO, :] = ext_ref[:, tm:tm + CONV_HALO, :]


def _in_proj(x1, nw, w_t, cw, cb, dtb, alog, *, seq):
    t = x1.shape[0]
    in_width = w_t.shape[0]
    assert (SSD_WIDTH + CONV_CH + SSD_HEADS) % SUBLANES == 0
    assert SSD_WIDTH + CONV_CH + DT_PAD <= in_width

    def row(n):
        return pl.BlockSpec((ROW_TILE, n), lambda i: (i, 0))

    def col(n):
        return pl.BlockSpec((n, ROW_TILE), lambda i: (0, i))

    return pl.pallas_call(
        functools.partial(_in_proj_kernel, tiles_per_seq=seq // ROW_TILE),
        out_shape=(jax.ShapeDtypeStruct((t, SSD_WIDTH), BF16),
                   jax.ShapeDtypeStruct((t, CONV_CH), BF16),
                   jax.ShapeDtypeStruct((t, DT_PAD), F32),
                   jax.ShapeDtypeStruct((t, DT_PAD), F32),
                   jax.ShapeDtypeStruct((DA_WIDTH, t), BF16),
                   jax.ShapeDtypeStruct((t, DA_WIDTH), BF16),
                   jax.ShapeDtypeStruct((DA_WIDTH, t), BF16)),
        grid=(t // ROW_TILE,),
        in_specs=[row(D_MODEL), _resident((1, D_MODEL)),
                  _resident((in_width, D_MODEL)),
                  _resident((CONV_WIDTH, CONV_CH)), _resident((1, CONV_CH)),
                  _resident((1, DT_PAD)), _resident((1, DT_PAD))],
        out_specs=(row(SSD_WIDTH), row(CONV_CH), row(DT_PAD), row(DT_PAD),
                   col(DA_WIDTH), row(DA_WIDTH), col(DA_WIDTH)),
        scratch_shapes=[pltpu.VMEM((CONV_CH // LANES, CONV_HALO + ROW_TILE + CONV_HALO,
                                    LANES), F32)],
        compiler_params=_params("arbitrary"),
        name="in_proj",
    )(x1, nw, w_t, cw, cb, dtb, alog)


def _ssd_step(xbc_ref, cum_ref, src_ref, o_ref, state_ref):
    L = SSD_CHUNK
    hp = SSD_HEAD_DIM
    row_i = lax.broadcasted_iota(jnp.int32, (L, L), 0)
    col_i = lax.broadcasted_iota(jnp.int32, (L, L), 1)
    causal = col_i <= row_i
    first = col_i < hp
    keep = (first.astype(BF16), 1.0 - first.astype(BF16))
    for r in range(SSD_CHUNKS_PER_STEP):
        rows = slice(r * SSD_CHUNK, (r + 1) * SSD_CHUNK)
        cum_col = cum_ref[rows, :]
        src_row = src_ref[rows, :].T
        for g in range(SSD_GROUPS):
            b_lo = SSD_WIDTH + g * SSD_STATE
            c_lo = SSD_WIDTH + (SSD_GROUPS + g) * SSD_STATE
            b_g = xbc_ref[rows, b_lo:b_lo + SSD_STATE]
            c_g = xbc_ref[rows, c_lo:c_lo + SSD_STATE]
            cb = _dot_nt(c_g, b_g)
            bt_g = b_g.astype(F32).T
            for pr in range(HEADS_PER_GROUP // 2):
                pair = g * (HEADS_PER_GROUP // 2) + pr
                lanes = slice(pair * 2 * hp, (pair + 1) * 2 * hp)
                x_pair = xbc_ref[rows, lanes]
                x_bd = jnp.concatenate([x_pair * keep[0], x_pair * keep[1]], axis=0)
                cols, w_parts, b_parts = [], [], []
                for t in range(2):
                    h = 2 * pair + t
                    col_b = jnp.broadcast_to(cum_col[:, h:h + 1], (L, L))
                    row_b = jnp.broadcast_to(src_row[h:h + 1, :], (L, L))
                    dec = jnp.exp2(jnp.where(causal, col_b - row_b, -jnp.inf))
                    w_parts.append((cb * dec).astype(BF16))
                    b_parts.append((bt_g * dec[L - 1:L, :]).astype(BF16))
                    cols.append(col_b)
                from_start = jnp.exp2(jnp.where(first, cols[0], cols[1]))
                s_prev = state_ref[pair]
                y = (_dot(jnp.concatenate(w_parts, axis=1), x_bd)
                     + _dot(c_g, s_prev.astype(BF16)) * from_start)
                o_ref[rows, lanes] = y.astype(BF16)
                new = _dot(jnp.concatenate(b_parts, axis=1), x_bd)
                state_ref[pair] = s_prev * from_start[L - 1:L, :] + new


def _ssd_gate_norm(y, xs, z, dskip, nw):
    y = y.astype(F32) + dskip * xs.astype(F32)
    y = y * _silu(z.astype(F32))
    gw = SSD_WIDTH // SSD_GROUPS
    outs = []
    for g in range(SSD_GROUPS):
        yg = y[:, g * gw:(g + 1) * gw]
        ms = jnp.mean(yg * yg, axis=-1, keepdims=True)
        outs.append(yg * lax.rsqrt(ms + EPS))
    return jnp.concatenate(outs, axis=-1) * nw


def _rel_bias_kernel(rb_ref, o_ref):
    k_i = lax.broadcasted_iota(jnp.int32, (BIAS_TILE, BIAS_TILE), 0)
    q_i = lax.broadcasted_iota(jnp.int32, (BIAS_TILE, BIAS_TILE), 1)
    max_exact = NUM_BUCKETS // 2
    for d in range(2):
        n = q_i - k_i + d * BIAS_TILE
        nf = jnp.maximum(n, 1).astype(F32)
        large = max_exact + (jnp.log(nf / max_exact) / math.log(MAX_DISTANCE / max_exact)
                             * (NUM_BUCKETS - max_exact)).astype(jnp.int32)
        large = jnp.minimum(large, NUM_BUCKETS - 1)
        bucket = jnp.where(n < max_exact, n, large)
        hits = [bucket == b for b in range(NUM_BUCKETS)]
        for h in range(DA_HEADS):
            tile = jnp.zeros((BIAS_TILE, BIAS_TILE), F32)
            for b in range(NUM_BUCKETS):
                tile = jnp.where(hits[b], rb_ref[b, h], tile)
            tile = (tile - rb_ref[NUM_BUCKETS - 1, h]) * LOG2_E
            o_ref[h, d] = jnp.where(n >= 0, tile, -jnp.inf)


def _rel_bias_tiles(rel_bias):
    return pl.pallas_call(
        _rel_bias_kernel,
        out_shape=jax.ShapeDtypeStruct((DA_HEADS, 2, BIAS_TILE, BIAS_TILE), F32),
        in_specs=[pl.BlockSpec(memory_space=pltpu.SMEM)],
        out_specs=pl.BlockSpec(memory_space=pltpu.VMEM),
        compiler_params=_params(),
        name="rel_bias",
    )(rel_bias)


def _attn_kernel(qt_ref, k_ref, vt_ref, bt_ref, lam_ref, sw_ref, o_ref,
                 s0_ref, s1_ref, m_ref, acc_ref, *, lambda_init, seq):
    tb = ATTN_BLOCK
    dh = DA_HEAD_DIM
    dv = DA_V_DIM
    nb = seq // tb

    def q_operand(i, c, q0, q1):
        blk = qt_ref[c * dh:(c + 1) * dh, i * tb + q0:i * tb + q1]
        pad = jnp.zeros_like(blk)
        return jnp.concatenate([blk, pad] if c == 0 else [pad, blk], axis=0)

    def v_operand(j, k0, k1):
        ones = jnp.ones((ATTN_ONES_ROWS, k1 - k0), BF16)
        return jnp.concatenate([vt_ref[:, j * tb + k0:j * tb + k1], ones], axis=0)

    s_refs = (s0_ref, s1_ref)
    half = tb // 2
    bt = BIAS_TILE
    nsub = tb // bt
    full = ((0, tb, 0, tb),)
    diag_parts = ((0, half, 0, half), (0, tb, half, tb))

    zero = jnp.minimum(pl.program_id(0), 0)

    def rows(lo, hi):
        return pl.ds(pl.multiple_of(zero + lo, BIAS_TILE), hi - lo)

    acc_rows = pl.ds(pl.multiple_of(zero, SUBLANES), dv + ATTN_ONES_ROWS)

    def scores(c, step):
        i, j, kind = step
        k0 = j * tb
        if kind != "diag":
            s_refs[c][rows(0, tb), 0:tb] = _dot(k_ref[k0:k0 + tb, :],
                                                q_operand(i, c, 0, tb))
        else:
            s_refs[c][rows(0, half), 0:tb] = _dot(k_ref[k0:k0 + half, :],
                                                  q_operand(i, c, 0, tb))
            s_refs[c][rows(half, tb), half:tb] = _dot(k_ref[k0 + half:k0 + tb, :],
                                                      q_operand(i, c, half, tb))

    def add_bias(c, kind):
        s_ref = s_refs[c]
        if kind == "near":
            near = (rows(tb - bt, tb), slice(0, bt))
            s_ref[near] = s_ref[near] + bt_ref[1]
        elif kind == "diag":
            for a in range(nsub):
                on = (rows(a * bt, (a + 1) * bt), slice(a * bt, (a + 1) * bt))
                s_ref[on] = s_ref[on] + bt_ref[0]
                if a + 1 < nsub:
                    off = (rows(a * bt, (a + 1) * bt), slice((a + 1) * bt, (a + 2) * bt))
                    s_ref[off] = s_ref[off] + bt_ref[1]
            for k0, k1, q0, q1 in diag_parts:
                for a in range(k0 // bt, k1 // bt):
                    for b in range(q0 // bt, q1 // bt):
                        if a > b:
                            s_ref[rows(a * bt, (a + 1) * bt), b * bt:(b + 1) * bt] = (
                                jnp.full((bt, bt), -jnp.inf, F32))

    def softmax_pv(c, step):
        i, j, kind = step
        for k0, k1, q0, q1 in (diag_parts if kind == "diag" else full):
            s = s_refs[c][rows(k0, k1), q0:q1]
            m_blk = jnp.max(s, axis=0, keepdims=True)
            if j == 0:
                m_new = m_blk
            else:
                m_old = m_ref[i, c, :, q0:q1]
                m_new = jnp.maximum(m_old, m_blk)
            pv = _dot(v_operand(j, k0, k1), jnp.exp2(s - m_new).astype(BF16))
            if j == 0:
                acc_ref[i, c, acc_rows, q0:q1] = pv
            else:
                acc_ref[i, c, acc_rows, q0:q1] = (acc_ref[i, c, acc_rows, q0:q1]
                                                  * jnp.exp2(m_old - m_new) + pv)
            m_ref[i, c, :, q0:q1] = m_new

    steps = []
    for j in range(nb):
        for i in range(j, nb):
            steps.append((i, j, "diag" if i == j else "near" if i == j + 1 else "far"))

    scores(0, steps[0])
    for t, step in enumerate(steps):
        scores(1, step)
        add_bias(0, step[2])
        softmax_pv(0, step)
        if t + 1 < len(steps):
            scores(0, steps[t + 1])
        add_bias(1, step[2])
        softmax_pv(1, step)

    lp = lam_ref[...]
    lam = (jnp.exp(jnp.sum(lp[0:1] * lp[1:2], axis=-1, keepdims=True))
           - jnp.exp(jnp.sum(lp[2:3] * lp[3:4], axis=-1, keepdims=True))
           + lambda_init)
    for i in range(nb):
        a0 = acc_ref[i, 0, acc_rows, :]
        a1 = acc_ref[i, 1, acc_rows, :]
        o = a0[0:dv] / a0[dv:dv + 1] - lam * (a1[0:dv] / a1[dv:dv + 1])
        o = o * lax.rsqrt(jnp.mean(o * o, axis=0, keepdims=True) + EPS)
        o = o * sw_ref[...] * (1.0 - lambda_init)
        o_ref[i * tb:(i + 1) * tb, :] = o.T.astype(BF16)


def _mixers_kernel(qt_ref, k_ref, vt_ref, bt_ref, lam_ref, sw_ref, xbc_ref, cum_ref,
                   src_ref, o_attn_ref, o_scan_ref, s0_ref, s1_ref, m_ref, acc_ref,
                   state_ref, *, lambda_init, seq):
    @pl.when(pl.program_id(1) == 0)
    def _():
        state_ref[...] = jnp.zeros_like(state_ref)

    _ssd_step(xbc_ref, cum_ref, src_ref, o_scan_ref, state_ref)
    _attn_kernel(qt_ref, k_ref, vt_ref, bt_ref, lam_ref, sw_ref, o_attn_ref,
                 s0_ref, s1_ref, m_ref, acc_ref, lambda_init=lambda_init, seq=seq)


def _mixers(q_t, k, v_t, tiles, lam_p, sw, xbc, cum, src, *, batch, seq, lambda_init):
    nb = seq // ATTN_BLOCK
    step_rows = SSD_CHUNKS_PER_STEP * SSD_CHUNK
    assert seq // step_rows == DA_HEADS
    seq_head = pl.BlockSpec((seq, DA_V_DIM), lambda b, h: (b, h))
    head_seq = pl.BlockSpec((DA_V_DIM, seq), lambda b, h: (h, b))
    acc_rows = DA_V_DIM + ATTN_ONES_ROWS

    def scan_rows(n):
        return pl.BlockSpec((step_rows, n), lambda b, h: (b * DA_HEADS + h, 0))

    return pl.pallas_call(
        functools.partial(_mixers_kernel, lambda_init=lambda_init, seq=seq),
        out_shape=(jax.ShapeDtypeStruct((batch * seq, DA_WIDTH), BF16),
                   jax.ShapeDtypeStruct((batch * seq, SSD_WIDTH), BF16)),
        grid=(batch, DA_HEADS),
        in_specs=[head_seq, seq_head, head_seq,
                  pl.BlockSpec((None, 2, BIAS_TILE, BIAS_TILE),
                               lambda b, h: (h, 0, 0, 0)),
                  _resident((4, DA_HEAD_DIM)), _resident((DA_V_DIM, ATTN_BLOCK)),
                  scan_rows(CONV_CH), scan_rows(DT_PAD), scan_rows(DT_PAD)],
        out_specs=(seq_head, scan_rows(SSD_WIDTH)),
        scratch_shapes=[pltpu.VMEM((ATTN_BLOCK, ATTN_BLOCK + LANES), F32),
                        pltpu.VMEM((ATTN_BLOCK, ATTN_BLOCK + LANES), F32),
                        pltpu.VMEM((nb, 2, 1, ATTN_BLOCK), F32),
                        pltpu.VMEM((nb, 2, acc_rows, ATTN_BLOCK), F32),
                        pltpu.VMEM((SSD_HEADS // 2, SSD_STATE, 2 * SSD_HEAD_DIM), F32)],
        compiler_params=_params("parallel", "arbitrary"),
        name="mixers",
    )(q_t, k, v_t, tiles, lam_p, sw, xbc, cum, src)


def _mem_kv_kernel(mem_ref, nw_ref, wk_ref, wv_ref, k_ref, v_ref):
    mn = _rms(mem_ref[...], nw_ref[...])
    k_ref[...] = _dot(mn, wk_ref[...]).astype(BF16)
    v_ref[...] = _dot(mn, wv_ref[...]).astype(BF16)


def _mem_kv(mem, nw, wk, wv):
    t = mem.shape[0]
    row = pl.BlockSpec((MEM_LEN, D_MODEL), lambda i: (i, 0))
    return pl.pallas_call(
        _mem_kv_kernel,
        out_shape=(jax.ShapeDtypeStruct((t, D_MODEL), BF16),
                   jax.ShapeDtypeStruct((t, D_MODEL), BF16)),
        grid=(t // MEM_LEN,),
        in_specs=[row, _resident((1, D_MODEL)), _resident((D_MODEL, D_MODEL)),
                  _resident((D_MODEL, D_MODEL))],
        out_specs=(row, row),
        compiler_params=_params("parallel"),
        name="mem_kv",
    )(mem, nw, wk, wv)


def _mix_out_cross_kernel(x_ref, ys_ref, xs_ref, z_ref, ya_ref, dskip_ref, snw_ref,
                          ws_ref, wa_ref, nw_ref, wq_ref, k_ref, v_ref, wo_ref, o_ref):
    y_ssd = _ssd_gate_norm(ys_ref[...], xs_ref[...], z_ref[...], dskip_ref[...],
                           snw_ref[...])
    x = (x_ref[...] + _dot(ya_ref[...].astype(F32), wa_ref[...])
         + _dot(y_ssd, ws_ref[...]))
    h = _rms(x, nw_ref[...])
    q = (_dot(h, wq_ref[...]) * (CROSS_HEAD_DIM ** -0.5)).astype(BF16)
    outs = []
    for hh in range(CROSS_HEADS):
        sl = slice(hh * CROSS_HEAD_DIM, (hh + 1) * CROSS_HEAD_DIM)
        s = _dot_nt(q[:, sl], k_ref[:, sl])
        p = jnp.exp(s - jnp.max(s, axis=-1, keepdims=True))
        o = _dot(p.astype(BF16), v_ref[:, sl])
        outs.append(o / jnp.sum(p, axis=-1, keepdims=True))
    o_ref[...] = x + _dot(jnp.concatenate(outs, axis=-1), wo_ref[...])


def _mix_out_cross(x1, y_scan, xbc, z, y_attn, dskip, snw, w_out, nw, wq, k, v, wo,
                   *, seq):
    t = x1.shape[0]
    tiles_per_batch = seq // ROW_TILE
    row = pl.BlockSpec((ROW_TILE, D_MODEL), lambda i: (i, 0))
    mem = pl.BlockSpec((MEM_LEN, D_MODEL), lambda i: (i // tiles_per_batch, 0))
    assert SSD_WIDTH == D_MODEL
    assert SSD_WIDTH == DA_WIDTH

    def w_half(j):
        return pl.BlockSpec((SSD_WIDTH, D_MODEL), lambda i: (j, 0),
                            pipeline_mode=pl.Buffered(1))

    return pl.pallas_call(
        _mix_out_cross_kernel,
        out_shape=jax.ShapeDtypeStruct((t, D_MODEL), F32),
        grid=(t // ROW_TILE,),
        in_specs=[row, row, row, row, row, _resident((1, SSD_WIDTH)),
                  _resident((1, SSD_WIDTH)), w_half(0), w_half(1),
                  _resident((1, D_MODEL)), _resident((D_MODEL, D_MODEL)), mem, mem,
                  _resident((D_MODEL, D_MODEL))],
        out_specs=row,
        compiler_params=_params("parallel"),
        name="mix_out_cross",
    )(x1, y_scan, xbc, z, y_attn, dskip, snw, w_out, w_out, nw, wq, k, v, wo)


def kernel(x, mem, norm_ffn1_w, ffn1_w_in, ffn1_w_out, norm_mix_w, w_in_mix, conv_w, conv_b, dt_bias, a_log, d_skip, ssd_norm_w, lambda_q1, lambda_k1, lambda_q2, lambda_k2, subln_w, rel_bias, w_out_mix, norm_cross_w, norm_mem_w, w_cq, w_ck, w_cv, w_co, norm_ffn2_w, ffn2_w_in, ffn2_w_out, norm_final_w):
    batch, seq, _ = x.shape
    depth = ffn1_w_in.shape[0]
    t = batch * seq
    xf = x.reshape(t, D_MODEL)
    memf = mem.reshape(batch * MEM_LEN, D_MODEL)
    fw = norm_final_w.reshape(1, D_MODEL)
    tiles = _rel_bias_tiles(rel_bias)

    def vec(p):
        return p.reshape(1, -1)

    def pad_lanes(p):
        return jnp.pad(p.reshape(1, -1), ((0, 0), (0, DT_PAD - p.shape[-1])))

    for l in range(depth):
        lambda_init = 0.8 - 0.6 * math.exp(-0.3 * l)
        xf = _ffn(xf, vec(norm_ffn1_w[l]), ffn1_w_in[l], ffn1_w_out[l], fw,
                  final=False, name="ffn1")

        z, xbc, cum, src, q_t, k, v_t = _in_proj(
            xf, vec(norm_mix_w[l]), w_in_mix[l].T,
            conv_w[l], vec(conv_b[l]), pad_lanes(dt_bias[l]), pad_lanes(a_log[l]),
            seq=seq)

        lam_p = jnp.stack([lambda_q1[l], lambda_k1[l], lambda_q2[l], lambda_k2[l]])
        sw = jnp.broadcast_to(subln_w[l].reshape(DA_V_DIM, 1), (DA_V_DIM, ATTN_BLOCK))
        y_attn, y_scan = _mixers(q_t, k, v_t, tiles, lam_p, sw, xbc, cum, src,
                                 batch=batch, seq=seq, lambda_init=lambda_init)

        mk, mv = _mem_kv(memf, vec(norm_mem_w[l]), w_ck[l], w_cv[l])
        xf = _mix_out_cross(xf, y_scan, xbc, z, y_attn,
                            vec(jnp.repeat(d_skip[l], SSD_HEAD_DIM)),
                            vec(ssd_norm_w[l]), w_out_mix[l],
                            vec(norm_cross_w[l]), w_cq[l], mk, mv, w_co[l], seq=seq)

        xf = _ffn(xf, vec(norm_ffn2_w[l]), ffn2_w_in[l], ffn2_w_out[l], fw,
                  final=(l == depth - 1), name="ffn2")
    return xf.reshape(batch, seq, D_MODEL)
```

```python
import functools
import math

import jax
import jax.numpy as jnp
from jax import lax
from jax.experimental import pallas as pl
from jax.experimental.pallas import tpu as pltpu

F32 = jnp.float32
BF16 = jnp.bfloat16

D_MODEL = 1024
MEM_LEN = 256
EPS = 1e-6

SSD_HEADS = 16
SSD_HEAD_DIM = 64
SSD_WIDTH = SSD_HEADS * SSD_HEAD_DIM
SSD_GROUPS = 4
SSD_STATE = 128
CONV_WIDTH = 4
SSD_CHUNK = 128
CONV_CH = SSD_WIDTH + 2 * SSD_GROUPS * SSD_STATE
HEADS_PER_GROUP = SSD_HEADS // SSD_GROUPS

DA_HEADS = 8
DA_HEAD_DIM = 64
DA_V_DIM = 2 * DA_HEAD_DIM
DA_WIDTH = DA_HEADS * DA_V_DIM

NUM_BUCKETS = 32
MAX_DISTANCE = 128

CROSS_HEADS = 4
CROSS_HEAD_DIM = D_MODEL // CROSS_HEADS

D_FF = 2816
LOG2_E = math.log2(math.e)

LANES = 128
SUBLANES = 8
MXU_DIM = 256
VMEM_LIMIT_BYTES = 56 * 1024 * 1024

ROW_TILE = 512
FF_CHUNK = MXU_DIM
ATTN_BLOCK = 512
ATTN_ONES_ROWS = 16
BIAS_TILE = MAX_DISTANCE
DT_PAD = LANES
CONV_HALO = SUBLANES
SSD_CHUNKS_PER_STEP = 4


def _resident(shape):
    nd = len(shape)
    return pl.BlockSpec(shape, lambda *_: (0,) * nd, pipeline_mode=pl.Buffered(1))


def _params(*sem):
    return pltpu.CompilerParams(dimension_semantics=sem,
                                vmem_limit_bytes=VMEM_LIMIT_BYTES)


def _rms(x, w):
    ms = jnp.mean(x * x, axis=-1, keepdims=True)
    return x * lax.rsqrt(ms + EPS) * w


def _silu(x):
    h = 0.5 * x
    return h + h * jnp.tanh(h)


def _dot(a, b):
    return jnp.dot(a, b, preferred_element_type=F32)


def _dot_nt(a, b):
    return lax.dot_general(a, b, (((1,), (1,)), ((), ())),
                           preferred_element_type=F32)


def _ffn_kernel(x_ref, nw_ref, win_ref, wout_ref, fw_ref, o_ref, *, final):
    x = x_ref[...]
    h = _rms(x, nw_ref[...])
    acc = None
    for c in range(D_FF // FF_CHUNK):
        lo = c * FF_CHUNK
        g = _dot(h, win_ref[:, lo:lo + FF_CHUNK])
        u = _dot(h, win_ref[:, D_FF + lo:D_FF + lo + FF_CHUNK])
        a = _silu(g) * u
        d = _dot(a, wout_ref[lo:lo + FF_CHUNK, :])
        acc = d if acc is None else acc + d
    y = x + 0.5 * acc
    if final:
        y = _rms(y, fw_ref[...])
    o_ref[...] = y


def _ffn(x, nw, win, wout, fw, *, final, name):
    t = x.shape[0]
    row = pl.BlockSpec((ROW_TILE, D_MODEL), lambda i: (i, 0))
    return pl.pallas_call(
        functools.partial(_ffn_kernel, final=final),
        out_shape=jax.ShapeDtypeStruct((t, D_MODEL), F32),
        grid=(t // ROW_TILE,),
        in_specs=[row, _resident((1, D_MODEL)), _resident((D_MODEL, 2 * D_FF)),
                  _resident((D_FF, D_MODEL)), _resident((1, D_MODEL))],
        out_specs=row,
        compiler_params=_params("parallel"),
        name=name,
    )(x, nw, win, wout, fw)


def _in_proj_kernel(x_ref, nw_ref, wt_ref, cw_ref, cb_ref, dtb_ref, alog_ref,
                    z_ref, xbc_ref, cum_ref, src_ref, qt_ref, k_ref, vt_ref, ext_ref,
                    *, tiles_per_seq):
    i = pl.program_id(0)
    tm = ROW_TILE
    h = _rms(x_ref[...], nw_ref[...])
    o_xbc = SSD_WIDTH
    o_dt = o_xbc + CONV_CH
    o_q = o_dt + SSD_HEADS
    o_k = o_q + DA_WIDTH
    o_v = o_k + DA_WIDTH

    def proj(lo, width):
        return _dot_nt(h, wt_ref[lo:lo + width, :])

    @pl.when(i % tiles_per_seq == 0)
    def _():
        ext_ref[:, 0:CONV_HALO, :] = jnp.zeros((CONV_CH // LANES, CONV_HALO, LANES), F32)

    x_dt = proj(o_dt, DT_PAD) + dtb_ref[...]
    dt = jnp.maximum(x_dt, 0.0) + jnp.log1p(jnp.exp(-jnp.abs(x_dt)))
    log2_dt = jnp.log2(dt)
    la = dt * (-LOG2_E * jnp.exp(alog_ref[...]))
    la_hi = la.astype(BF16)
    la_mid = (la - la_hi.astype(F32)).astype(BF16)
    la_lo = (la - la_hi.astype(F32) - la_mid.astype(F32)).astype(BF16)
    row_i = lax.broadcasted_iota(jnp.int32, (SSD_CHUNK, SSD_CHUNK), 0)
    col_i = lax.broadcasted_iota(jnp.int32, (SSD_CHUNK, SSD_CHUNK), 1)
    tril = (col_i <= row_i).astype(BF16)

    wide = ((z_ref, 0, SSD_WIDTH, 1.0, False),
            (qt_ref, o_q, DA_WIDTH, DA_HEAD_DIM ** -0.5 * LOG2_E, True),
            (k_ref, o_k, DA_WIDTH, 1.0, False), (vt_ref, o_v, DA_WIDTH, 1.0, True))
    n_chunks = tm // SSD_CHUNK
    assert n_chunks == len(wide)
    u = proj(o_xbc, CONV_CH)
    for t in range(CONV_CH // LANES):
        ext_ref[t, CONV_HALO:CONV_HALO + tm, :] = u[:, t * LANES:(t + 1) * LANES]
    for r in range(n_chunks):
        lo = r * SSD_CHUNK
        rows = slice(lo, lo + SSD_CHUNK)
        cum = (_dot(tril, la_hi[rows]) + _dot(tril, la_mid[rows])
               + _dot(tril, la_lo[rows]))
        cum_ref[rows, :] = cum
        src_ref[rows, :] = cum - log2_dt[rows]
        base = CONV_HALO + lo
        acts = []
        for t in range(CONV_CH // LANES):
            ln = slice(t * LANES, (t + 1) * LANES)
            acc = cb_ref[:, ln]
            for j in range(CONV_WIDTH):
                start = base - (CONV_WIDTH - 1) + j
                acc = acc + cw_ref[j:j + 1, ln] * ext_ref[t, start:start + SSD_CHUNK, :]
            acts.append(_silu(acc))
        xbc_ref[rows, :] = jnp.concatenate(acts, axis=-1).astype(BF16)
        o_ref, w_lo, w_width, scale, feature_major = wide[r]
        if feature_major:
            out = _dot_nt(wt_ref[w_lo:w_lo + w_width, :], h)
        else:
            out = proj(w_lo, w_width)
        o_ref[...] = (out if scale == 1.0 else out * scale).astype(BF16)
    ext_ref[:, 0:CONV_HALO, :] = ext_ref[:, tm:tm + CONV_HALO, :]


def _in_proj(x1, nw, w_t, cw, cb, dtb, alog, *, seq):
    t = x1.shape[0]
    in_width = w_t.shape[0]
    assert (SSD_WIDTH + CONV_CH + SSD_HEADS) % SUBLANES == 0
    assert SSD_WIDTH + CONV_CH + DT_PAD <= in_width

    def row(n):
        return pl.BlockSpec((ROW_TILE, n), lambda i: (i, 0))

    def col(n):
        return pl.BlockSpec((n, ROW_TILE), lambda i: (0, i))

    return pl.pallas_call(
        functools.partial(_in_proj_kernel, tiles_per_seq=seq // ROW_TILE),
        out_shape=(jax.ShapeDtypeStruct((t, SSD_WIDTH), BF16),
                   jax.ShapeDtypeStruct((t, CONV_CH), BF16),
                   jax.ShapeDtypeStruct((t, DT_PAD), F32),
                   jax.ShapeDtypeStruct((t, DT_PAD), F32),
                   jax.ShapeDtypeStruct((DA_WIDTH, t), BF16),
                   jax.ShapeDtypeStruct((t, DA_WIDTH), BF16),
                   jax.ShapeDtypeStruct((DA_WIDTH, t), BF16)),
        grid=(t // ROW_TILE,),
        in_specs=[row(D_MODEL), _resident((1, D_MODEL)),
                  _resident((in_width, D_MODEL)),
                  _resident((CONV_WIDTH, CONV_CH)), _resident((1, CONV_CH)),
                  _resident((1, DT_PAD)), _resident((1, DT_PAD))],
        out_specs=(row(SSD_WIDTH), row(CONV_CH), row(DT_PAD), row(DT_PAD),
                   col(DA_WIDTH), row(DA_WIDTH), col(DA_WIDTH)),
        scratch_shapes=[pltpu.VMEM((CONV_CH // LANES, CONV_HALO + ROW_TILE + CONV_HALO,
                                    LANES), F32)],
        compiler_params=_params("arbitrary"),
        name="in_proj",
    )(x1, nw, w_t, cw, cb, dtb, alog)


def _ssd_step(xbc_ref, cum_ref, src_ref, o_ref, state_ref):
    L = SSD_CHUNK
    hp = SSD_HEAD_DIM
    row_i = lax.broadcasted_iota(jnp.int32, (L, L), 0)
    col_i = lax.broadcasted_iota(jnp.int32, (L, L), 1)
    causal = col_i <= row_i
    first = col_i < hp
    keep = (first.astype(BF16), 1.0 - first.astype(BF16))
    for r in range(SSD_CHUNKS_PER_STEP):
        rows = slice(r * SSD_CHUNK, (r + 1) * SSD_CHUNK)
        cum_col = cum_ref[rows, :]
        src_row = src_ref[rows, :].T
        for g in range(SSD_GROUPS):
            b_lo = SSD_WIDTH + g * SSD_STATE
            c_lo = SSD_WIDTH + (SSD_GROUPS + g) * SSD_STATE
            b_g = xbc_ref[rows, b_lo:b_lo + SSD_STATE]
            c_g = xbc_ref[rows, c_lo:c_lo + SSD_STATE]
            cb = _dot_nt(c_g, b_g)
            bt_g = b_g.astype(F32).T
            for pr in range(HEADS_PER_GROUP // 2):
                pair = g * (HEADS_PER_GROUP // 2) + pr
                lanes = slice(pair * 2 * hp, (pair + 1) * 2 * hp)
                x_pair = xbc_ref[rows, lanes]
                x_bd = jnp.concatenate([x_pair * keep[0], x_pair * keep[1]], axis=0)
                cols, w_parts, b_parts = [], [], []
                for t in range(2):
                    h = 2 * pair + t
                    col_b = jnp.broadcast_to(cum_col[:, h:h + 1], (L, L))
                    row_b = jnp.broadcast_to(src_row[h:h + 1, :], (L, L))
                    dec = jnp.exp2(jnp.where(causal, col_b - row_b, -jnp.inf))
                    w_parts.append((cb * dec).astype(BF16))
                    b_parts.append((bt_g * dec[L - 1:L, :]).astype(BF16))
                    cols.append(col_b)
                from_start = jnp.exp2(jnp.where(first, cols[0], cols[1]))
                s_prev = state_ref[pair]
                y = (_dot(jnp.concatenate(w_parts, axis=1), x_bd)
                     + _dot(c_g, s_prev.astype(BF16)) * from_start)
                o_ref[rows, lanes] = y.astype(BF16)
                new = _dot(jnp.concatenate(b_parts, axis=1), x_bd)
                state_ref[pair] = s_prev * from_start[L - 1:L, :] + new


def _ssd_gate_norm(y, xs, z, dskip, nw):
    y = y.astype(F32) + dskip * xs.astype(F32)
    y = y * _silu(z.astype(F32))
    gw = SSD_WIDTH // SSD_GROUPS
    outs = []
    for g in range(SSD_GROUPS):
        yg = y[:, g * gw:(g + 1) * gw]
        ms = jnp.mean(yg * yg, axis=-1, keepdims=True)
        outs.append(yg * lax.rsqrt(ms + EPS))
    return jnp.concatenate(outs, axis=-1) * nw


def _rel_bias_kernel(rb_ref, o_ref):
    k_i = lax.broadcasted_iota(jnp.int32, (BIAS_TILE, BIAS_TILE), 0)
    q_i = lax.broadcasted_iota(jnp.int32, (BIAS_TILE, BIAS_TILE), 1)
    max_exact = NUM_BUCKETS // 2
    for d in range(2):
        n = q_i - k_i + d * BIAS_TILE
        nf = jnp.maximum(n, 1).astype(F32)
        large = max_exact + (jnp.log(nf / max_exact) / math.log(MAX_DISTANCE / max_exact)
                             * (NUM_BUCKETS - max_exact)).astype(jnp.int32)
        large = jnp.minimum(large, NUM_BUCKETS - 1)
        bucket = jnp.where(n < max_exact, n, large)
        hits = [bucket == b for b in range(NUM_BUCKETS)]
        for h in range(DA_HEADS):
            tile = jnp.zeros((BIAS_TILE, BIAS_TILE), F32)
            for b in range(NUM_BUCKETS):
                tile = jnp.where(hits[b], rb_ref[b, h], tile)
            tile = (tile - rb_ref[NUM_BUCKETS - 1, h]) * LOG2_E
            o_ref[h, d] = jnp.where(n >= 0, tile, -jnp.inf)


def _rel_bias_tiles(rel_bias):
    return pl.pallas_call(
        _rel_bias_kernel,
        out_shape=jax.ShapeDtypeStruct((DA_HEADS, 2, BIAS_TILE, BIAS_TILE), F32),
        in_specs=[pl.BlockSpec(memory_space=pltpu.SMEM)],
        out_specs=pl.BlockSpec(memory_space=pltpu.VMEM),
        compiler_params=_params(),
        name="rel_bias",
    )(rel_bias)


def _attn_kernel(qt_ref, k_ref, vt_ref, bt_ref, lam_ref, sw_ref, o_ref,
                 s0_ref, s1_ref, m_ref, acc_ref, *, lambda_init, seq):
    tb = ATTN_BLOCK
    dh = DA_HEAD_DIM
    dv = DA_V_DIM
    nb = seq // tb

    def q_operand(i, c, q0, q1):
        blk = qt_ref[c * dh:(c + 1) * dh, i * tb + q0:i * tb + q1]
        pad = jnp.zeros_like(blk)
        return jnp.concatenate([blk, pad] if c == 0 else [pad, blk], axis=0)

    def v_operand(j, k0, k1):
        ones = jnp.ones((ATTN_ONES_ROWS, k1 - k0), BF16)
        return jnp.concatenate([vt_ref[:, j * tb + k0:j * tb + k1], ones], axis=0)

    s_refs = (s0_ref, s1_ref)
    half = tb // 2
    bt = BIAS_TILE
    nsub = tb // bt
    full = ((0, tb, 0, tb),)
    diag_parts = ((0, half, 0, half), (0, tb, half, tb))

    zero = jnp.minimum(pl.program_id(0), 0)

    def rows(lo, hi):
        return pl.ds(pl.multiple_of(zero + lo, BIAS_TILE), hi - lo)

    acc_rows = pl.ds(pl.multiple_of(zero, SUBLANES), dv + ATTN_ONES_ROWS)

    def scores(c, step):
        i, j, kind = step
        k0 = j * tb
        if kind != "diag":
            s_refs[c][rows(0, tb), 0:tb] = _dot(k_ref[k0:k0 + tb, :],
                                                q_operand(i, c, 0, tb))
        else:
            s_refs[c][rows(0, half), 0:tb] = _dot(k_ref[k0:k0 + half, :],
                                                  q_operand(i, c, 0, tb))
            s_refs[c][rows(half, tb), half:tb] = _dot(k_ref[k0 + half:k0 + tb, :],
                                                      q_operand(i, c, half, tb))

    def add_bias(c, kind):
        s_ref = s_refs[c]
        if kind == "near":
            near = (rows(tb - bt, tb), slice(0, bt))
            s_ref[near] = s_ref[near] + bt_ref[1]
        elif kind == "diag":
            for a in range(nsub):
                on = (rows(a * bt, (a + 1) * bt), slice(a * bt, (a + 1) * bt))
                s_ref[on] = s_ref[on] + bt_ref[0]
                if a + 1 < nsub:
                    off = (rows(a * bt, (a + 1) * bt), slice((a + 1) * bt, (a + 2) * bt))
                    s_ref[off] = s_ref[off] + bt_ref[1]
            for k0, k1, q0, q1 in diag_parts:
                for a in range(k0 // bt, k1 // bt):
                    for b in range(q0 // bt, q1 // bt):
                        if a > b:
                            s_ref[rows(a * bt, (a + 1) * bt), b * bt:(b + 1) * bt] = (
                                jnp.full((bt, bt), -jnp.inf, F32))

    def softmax_pv(c, step):
        i, j, kind = step
        for k0, k1, q0, q1 in (diag_parts if kind == "diag" else full):
            s = s_refs[c][rows(k0, k1), q0:q1]
            m_blk = jnp.max(s, axis=0, keepdims=True)
            if j == 0:
                m_new = m_blk
            else:
                m_old = m_ref[i, c, :, q0:q1]
                m_new = jnp.maximum(m_old, m_blk)
            pv = _dot(v_operand(j, k0, k1), jnp.exp2(s - m_new).astype(BF16))
            if j == 0:
                acc_ref[i, c, acc_rows, q0:q1] = pv
            else:
                acc_ref[i, c, acc_rows, q0:q1] = (acc_ref[i, c, acc_rows, q0:q1]
                                                  * jnp.exp2(m_old - m_new) + pv)
            m_ref[i, c, :, q0:q1] = m_new

    steps = []
    for j in range(nb):
        for i in range(j, nb):
            steps.append((i, j, "diag" if i == j else "near" if i == j + 1 else "far"))

    scores(0, steps[0])
    for t, step in enumerate(steps):
        scores(1, step)
        add_bias(0, step[2])
        softmax_pv(0, step)
        if t + 1 < len(steps):
            scores(0, steps[t + 1])
        add_bias(1, step[2])
        softmax_pv(1, step)

    lp = lam_ref[...]
    lam = (jnp.exp(jnp.sum(lp[0:1] * lp[1:2], axis=-1, keepdims=True))
           - jnp.exp(jnp.sum(lp[2:3] * lp[3:4], axis=-1, keepdims=True))
           + lambda_init)
    for i in range(nb):
        a0 = acc_ref[i, 0, acc_rows, 0:tb]
        a1 = acc_ref[i, 1, acc_rows, 0:tb]
        o = a0[0:dv] / a0[dv:dv + 1] - lam * (a1[0:dv] / a1[dv:dv + 1])
        o = o * lax.rsqrt(jnp.mean(o * o, axis=0, keepdims=True) + EPS)
        o = o * sw_ref[...] * (1.0 - lambda_init)
        o_ref[i * tb:(i + 1) * tb, :] = o.T.astype(BF16)


def _mixers_kernel(qt_ref, k_ref, vt_ref, bt_ref, lam_ref, sw_ref, xbc_ref, cum_ref,
                   src_ref, o_attn_ref, o_scan_ref, s0_ref, s1_ref, m_ref, acc_ref,
                   state_ref, *, lambda_init, seq):
    @pl.when(pl.program_id(1) == 0)
    def _():
        state_ref[...] = jnp.zeros_like(state_ref)

    _ssd_step(xbc_ref, cum_ref, src_ref, o_scan_ref, state_ref)
    _attn_kernel(qt_ref, k_ref, vt_ref, bt_ref, lam_ref, sw_ref, o_attn_ref,
                 s0_ref, s1_ref, m_ref, acc_ref, lambda_init=lambda_init, seq=seq)


def _mixers(q_t, k, v_t, tiles, lam_p, sw, xbc, cum, src, *, batch, seq, lambda_init):
    nb = seq // ATTN_BLOCK
    step_rows = SSD_CHUNKS_PER_STEP * SSD_CHUNK
    assert seq // step_rows == DA_HEADS
    seq_head = pl.BlockSpec((seq, DA_V_DIM), lambda b, h: (b, h))
    head_seq = pl.BlockSpec((DA_V_DIM, seq), lambda b, h: (h, b))
    acc_rows = DA_V_DIM + ATTN_ONES_ROWS

    def scan_rows(n):
        return pl.BlockSpec((step_rows, n), lambda b, h: (b * DA_HEADS + h, 0))

    return pl.pallas_call(
        functools.partial(_mixers_kernel, lambda_init=lambda_init, seq=seq),
        out_shape=(jax.ShapeDtypeStruct((batch * seq, DA_WIDTH), BF16),
                   jax.ShapeDtypeStruct((batch * seq, SSD_WIDTH), BF16)),
        grid=(batch, DA_HEADS),
        in_specs=[head_seq, seq_head, head_seq,
                  pl.BlockSpec((None, 2, BIAS_TILE, BIAS_TILE),
                               lambda b, h: (h, 0, 0, 0)),
                  _resident((4, DA_HEAD_DIM)), _resident((DA_V_DIM, ATTN_BLOCK)),
                  scan_rows(CONV_CH), scan_rows(DT_PAD), scan_rows(DT_PAD)],
        out_specs=(seq_head, scan_rows(SSD_WIDTH)),
        scratch_shapes=[pltpu.VMEM((ATTN_BLOCK, ATTN_BLOCK + LANES), F32),
                        pltpu.VMEM((ATTN_BLOCK, ATTN_BLOCK + LANES), F32),
                        pltpu.VMEM((nb, 2, 1, ATTN_BLOCK), F32),
                        pltpu.VMEM((nb, 2, acc_rows, ATTN_BLOCK + LANES), F32),
                        pltpu.VMEM((SSD_HEADS // 2, SSD_STATE, 2 * SSD_HEAD_DIM), F32)],
        compiler_params=_params("parallel", "arbitrary"),
        name="mixers",
    )(q_t, k, v_t, tiles, lam_p, sw, xbc, cum, src)


def _mem_kv_kernel(mem_ref, nw_ref, wk_ref, wv_ref, k_ref, v_ref):
    mn = _rms(mem_ref[...], nw_ref[...])
    k_ref[...] = _dot(mn, wk_ref[...]).astype(BF16)
    v_ref[...] = _dot(mn, wv_ref[...]).astype(BF16)


def _mem_kv(mem, nw, wk, wv):
    t = mem.shape[0]
    row = pl.BlockSpec((MEM_LEN, D_MODEL), lambda i: (i, 0))
    return pl.pallas_call(
        _mem_kv_kernel,
        out_shape=(jax.ShapeDtypeStruct((t, D_MODEL), BF16),
                   jax.ShapeDtypeStruct((t, D_MODEL), BF16)),
        grid=(t // MEM_LEN,),
        in_specs=[row, _resident((1, D_MODEL)), _resident((D_MODEL, D_MODEL)),
                  _resident((D_MODEL, D_MODEL))],
        out_specs=(row, row),
        compiler_params=_params("parallel"),
        name="mem_kv",
    )(mem, nw, wk, wv)


def _mix_out_cross_kernel(x_ref, ys_ref, xs_ref, z_ref, ya_ref, dskip_ref, snw_ref,
                          ws_ref, wa_ref, nw_ref, wq_ref, k_ref, v_ref, wo_ref, o_ref):
    y_ssd = _ssd_gate_norm(ys_ref[...], xs_ref[...], z_ref[...], dskip_ref[...],
                           snw_ref[...])
    x = (x_ref[...] + _dot(ya_ref[...].astype(F32), wa_ref[...])
         + _dot(y_ssd, ws_ref[...]))
    h = _rms(x, nw_ref[...])
    q = (_dot(h, wq_ref[...]) * (CROSS_HEAD_DIM ** -0.5)).astype(BF16)
    outs = []
    for hh in range(CROSS_HEADS):
        sl = slice(hh * CROSS_HEAD_DIM, (hh + 1) * CROSS_HEAD_DIM)
        s = _dot_nt(q[:, sl], k_ref[:, sl])
        p = jnp.exp(s - jnp.max(s, axis=-1, keepdims=True))
        o = _dot(p.astype(BF16), v_ref[:, sl])
        outs.append(o / jnp.sum(p, axis=-1, keepdims=True))
    o_ref[...] = x + _dot(jnp.concatenate(outs, axis=-1), wo_ref[...])


def _mix_out_cross(x1, y_scan, xbc, z, y_attn, dskip, snw, w_out, nw, wq, k, v, wo,
                   *, seq):
    t = x1.shape[0]
    tiles_per_batch = seq // ROW_TILE
    row = pl.BlockSpec((ROW_TILE, D_MODEL), lambda i: (i, 0))
    mem = pl.BlockSpec((MEM_LEN, D_MODEL), lambda i: (i // tiles_per_batch, 0))
    assert SSD_WIDTH == D_MODEL
    assert SSD_WIDTH == DA_WIDTH

    def w_half(j):
        return pl.BlockSpec((SSD_WIDTH, D_MODEL), lambda i: (j, 0),
                            pipeline_mode=pl.Buffered(1))

    return pl.pallas_call(
        _mix_out_cross_kernel,
        out_shape=jax.ShapeDtypeStruct((t, D_MODEL), F32),
        grid=(t // ROW_TILE,),
        in_specs=[row, row, row, row, row, _resident((1, SSD_WIDTH)),
                  _resident((1, SSD_WIDTH)), w_half(0), w_half(1),
                  _resident((1, D_MODEL)), _resident((D_MODEL, D_MODEL)), mem, mem,
                  _resident((D_MODEL, D_MODEL))],
        out_specs=row,
        compiler_params=_params("parallel"),
        name="mix_out_cross",
    )(x1, y_scan, xbc, z, y_attn, dskip, snw, w_out, w_out, nw, wq, k, v, wo)


def kernel(x, mem, norm_ffn1_w, ffn1_w_in, ffn1_w_out, norm_mix_w, w_in_mix, conv_w, conv_b, dt_bias, a_log, d_skip, ssd_norm_w, lambda_q1, lambda_k1, lambda_q2, lambda_k2, subln_w, rel_bias, w_out_mix, norm_cross_w, norm_mem_w, w_cq, w_ck, w_cv, w_co, norm_ffn2_w, ffn2_w_in, ffn2_w_out, norm_final_w):
    batch, seq, _ = x.shape
    depth = ffn1_w_in.shape[0]
    t = batch * seq
    xf = x.reshape(t, D_MODEL)
    memf = mem.reshape(batch * MEM_LEN, D_MODEL)
    fw = norm_final_w.reshape(1, D_MODEL)
    tiles = _rel_bias_tiles(rel_bias)

    def vec(p):
        return p.reshape(1, -1)

    def pad_lanes(p):
        return jnp.pad(p.reshape(1, -1), ((0, 0), (0, DT_PAD - p.shape[-1])))

    for l in range(depth):
        lambda_init = 0.8 - 0.6 * math.exp(-0.3 * l)
        xf = _ffn(xf, vec(norm_ffn1_w[l]), ffn1_w_in[l], ffn1_w_out[l], fw,
                  final=False, name="ffn1")

        z, xbc, cum, src, q_t, k, v_t = _in_proj(
            xf, vec(norm_mix_w[l]), w_in_mix[l].T,
            conv_w[l], vec(conv_b[l]), pad_lanes(dt_bias[l]), pad_lanes(a_log[l]),
            seq=seq)

        lam_p = jnp.stack([lambda_q1[l], lambda_k1[l], lambda_q2[l], lambda_k2[l]])
        sw = jnp.broadcast_to(subln_w[l].reshape(DA_V_DIM, 1), (DA_V_DIM, ATTN_BLOCK))
        y_attn, y_scan = _mixers(q_t, k, v_t, tiles, lam_p, sw, xbc, cum, src,
                                 batch=batch, seq=seq, lambda_init=lambda_init)

        mk, mv = _mem_kv(memf, vec(norm_mem_w[l]), w_ck[l], w_cv[l])
        xf = _mix_out_cross(xf, y_scan, xbc, z, y_attn,
                            vec(jnp.repeat(d_skip[l], SSD_HEAD_DIM)),
                            vec(ssd_norm_w[l]), w_out_mix[l],
                            vec(norm_cross_w[l]), w_cq[l], mk, mv, w_co[l], seq=seq)

        xf = _ffn(xf, vec(norm_ffn2_w[l]), ffn2_w_in[l], ffn2_w_out[l], fw,
                  final=(l == depth - 1), name="ffn2")
    return xf.reshape(batch, seq, D_MODEL)
```

```python
import functools
import math

import jax
import jax.numpy as jnp
from jax import lax
from jax.experimental import pallas as pl
from jax.experimental.pallas import tpu as pltpu

F32 = jnp.float32
BF16 = jnp.bfloat16

D_MODEL = 1024
MEM_LEN = 256
EPS = 1e-6

SSD_HEADS = 16
SSD_HEAD_DIM = 64
SSD_WIDTH = SSD_HEADS * SSD_HEAD_DIM
SSD_GROUPS = 4
SSD_STATE = 128
CONV_WIDTH = 4
SSD_CHUNK = 128
CONV_CH = SSD_WIDTH + 2 * SSD_GROUPS * SSD_STATE
HEADS_PER_GROUP = SSD_HEADS // SSD_GROUPS

DA_HEADS = 8
DA_HEAD_DIM = 64
DA_V_DIM = 2 * DA_HEAD_DIM
DA_WIDTH = DA_HEADS * DA_V_DIM

NUM_BUCKETS = 32
MAX_DISTANCE = 128

CROSS_HEADS = 4
CROSS_HEAD_DIM = D_MODEL // CROSS_HEADS

D_FF = 2816
LOG2_E = math.log2(math.e)

LANES = 128
SUBLANES = 8
MXU_DIM = 256
VMEM_LIMIT_BYTES = 56 * 1024 * 1024

ROW_TILE = 512
FF_CHUNK = MXU_DIM
ATTN_BLOCK = 512
ATTN_ONES_ROWS = 16
BIAS_TILE = MAX_DISTANCE
DT_PAD = LANES
CONV_HALO = SUBLANES
SSD_CHUNKS_PER_STEP = 4


def _resident(shape):
    nd = len(shape)
    return pl.BlockSpec(shape, lambda *_: (0,) * nd, pipeline_mode=pl.Buffered(1))


def _params(*sem):
    return pltpu.CompilerParams(dimension_semantics=sem,
                                vmem_limit_bytes=VMEM_LIMIT_BYTES)


def _rms(x, w):
    ms = jnp.mean(x * x, axis=-1, keepdims=True)
    return x * lax.rsqrt(ms + EPS) * w


def _silu(x):
    h = 0.5 * x
    return h + h * jnp.tanh(h)


def _dot(a, b):
    return jnp.dot(a, b, preferred_element_type=F32)


def _dot_nt(a, b):
    return lax.dot_general(a, b, (((1,), (1,)), ((), ())),
                           preferred_element_type=F32)


def _ffn_kernel(x_ref, nw_ref, win_ref, wout_ref, fw_ref, o_ref, *, final):
    x = x_ref[...]
    h = _rms(x, nw_ref[...])
    acc = None
    for c in range(D_FF // FF_CHUNK):
        lo = c * FF_CHUNK
        g = _dot(h, win_ref[:, lo:lo + FF_CHUNK])
        u = _dot(h, win_ref[:, D_FF + lo:D_FF + lo + FF_CHUNK])
        a = _silu(g) * u
        d = _dot(a, wout_ref[lo:lo + FF_CHUNK, :])
        acc = d if acc is None else acc + d
    y = x + 0.5 * acc
    if final:
        y = _rms(y, fw_ref[...])
    o_ref[...] = y


def _ffn(x, nw, win, wout, fw, *, final, name):
    t = x.shape[0]
    row = pl.BlockSpec((ROW_TILE, D_MODEL), lambda i: (i, 0))
    return pl.pallas_call(
        functools.partial(_ffn_kernel, final=final),
        out_shape=jax.ShapeDtypeStruct((t, D_MODEL), F32),
        grid=(t // ROW_TILE,),
        in_specs=[row, _resident((1, D_MODEL)), _resident((D_MODEL, 2 * D_FF)),
                  _resident((D_FF, D_MODEL)), _resident((1, D_MODEL))],
        out_specs=row,
        compiler_params=_params("parallel"),
        name=name,
    )(x, nw, win, wout, fw)


def _in_proj_kernel(x_ref, nw_ref, wt_ref, cw_ref, cb_ref, dtb_ref, alog_ref,
                    z_ref, xbc_ref, cum_ref, src_ref, qt_ref, k_ref, vt_ref, ext_ref,
                    *, tiles_per_seq):
    i = pl.program_id(0)
    tm = ROW_TILE
    h = _rms(x_ref[...], nw_ref[...])
    o_xbc = SSD_WIDTH
    o_dt = o_xbc + CONV_CH
    o_q = o_dt + SSD_HEADS
    o_k = o_q + DA_WIDTH
    o_v = o_k + DA_WIDTH

    def proj(lo, width):
        return _dot_nt(h, wt_ref[lo:lo + width, :])

    @pl.when(i % tiles_per_seq == 0)
    def _():
        ext_ref[:, 0:CONV_HALO, :] = jnp.zeros((CONV_CH // LANES, CONV_HALO, LANES), F32)

    x_dt = proj(o_dt, DT_PAD) + dtb_ref[...]
    dt = jnp.maximum(x_dt, 0.0) + jnp.log1p(jnp.exp(-jnp.abs(x_dt)))
    log2_dt = jnp.log2(dt)
    la = dt * (-LOG2_E * jnp.exp(alog_ref[...]))
    la_hi = la.astype(BF16)
    la_mid = (la - la_hi.astype(F32)).astype(BF16)
    la_lo = (la - la_hi.astype(F32) - la_mid.astype(F32)).astype(BF16)
    row_i = lax.broadcasted_iota(jnp.int32, (SSD_CHUNK, SSD_CHUNK), 0)
    col_i = lax.broadcasted_iota(jnp.int32, (SSD_CHUNK, SSD_CHUNK), 1)
    tril = (col_i <= row_i).astype(BF16)

    wide = ((z_ref, 0, SSD_WIDTH, 1.0, False),
            (qt_ref, o_q, DA_WIDTH, DA_HEAD_DIM ** -0.5 * LOG2_E, True),
            (k_ref, o_k, DA_WIDTH, 1.0, False), (vt_ref, o_v, DA_WIDTH, 1.0, True))
    n_chunks = tm // SSD_CHUNK
    assert n_chunks == len(wide)
    u = proj(o_xbc, CONV_CH)
    for t in range(CONV_CH // LANES):
        ext_ref[t, CONV_HALO:CONV_HALO + tm, :] = u[:, t * LANES:(t + 1) * LANES]
    for r in range(n_chunks):
        lo = r * SSD_CHUNK
        rows = slice(lo, lo + SSD_CHUNK)
        cum = (_dot(tril, la_hi[rows]) + _dot(tril, la_mid[rows])
               + _dot(tril, la_lo[rows]))
        cum_ref[rows, :] = cum
        src_ref[rows, :] = cum - log2_dt[rows]
        base = CONV_HALO + lo
        acts = []
        for t in range(CONV_CH // LANES):
            ln = slice(t * LANES, (t + 1) * LANES)
            acc = cb_ref[:, ln]
            for j in range(CONV_WIDTH):
                start = base - (CONV_WIDTH - 1) + j
                acc = acc + cw_ref[j:j + 1, ln] * ext_ref[t, start:start + SSD_CHUNK, :]
            acts.append(_silu(acc))
        xbc_ref[rows, :] = jnp.concatenate(acts, axis=-1).astype(BF16)
        o_ref, w_lo, w_width, scale, feature_major = wide[r]
        if feature_major:
            out = _dot_nt(wt_ref[w_lo:w_lo + w_width, :], h)
        else:
            out = proj(w_lo, w_width)
        o_ref[...] = (out if scale == 1.0 else out * scale).astype(BF16)
    ext_ref[:, 0:CONV_HALO, :] = ext_ref[:, tm:tm + CONV_HALO, :]


def _in_proj(x1, nw, w_t, cw, cb, dtb, alog, *, seq):
    t = x1.shape[0]
    in_width = w_t.shape[0]
    assert (SSD_WIDTH + CONV_CH + SSD_HEADS) % SUBLANES == 0
    assert SSD_WIDTH + CONV_CH + DT_PAD <= in_width

    def row(n):
        return pl.BlockSpec((ROW_TILE, n), lambda i: (i, 0))

    def col(n):
        return pl.BlockSpec((n, ROW_TILE), lambda i: (0, i))

    return pl.pallas_call(
        functools.partial(_in_proj_kernel, tiles_per_seq=seq // ROW_TILE),
        out_shape=(jax.ShapeDtypeStruct((t, SSD_WIDTH), BF16),
                   jax.ShapeDtypeStruct((t, CONV_CH), BF16),
                   jax.ShapeDtypeStruct((t, DT_PAD), F32),
                   jax.ShapeDtypeStruct((t, DT_PAD), F32),
                   jax.ShapeDtypeStruct((DA_WIDTH, t), BF16),
                   jax.ShapeDtypeStruct((t, DA_WIDTH), BF16),
                   jax.ShapeDtypeStruct((DA_WIDTH, t), BF16)),
        grid=(t // ROW_TILE,),
        in_specs=[row(D_MODEL), _resident((1, D_MODEL)),
                  _resident((in_width, D_MODEL)),
                  _resident((CONV_WIDTH, CONV_CH)), _resident((1, CONV_CH)),
                  _resident((1, DT_PAD)), _resident((1, DT_PAD))],
        out_specs=(row(SSD_WIDTH), row(CONV_CH), row(DT_PAD), row(DT_PAD),
                   col(DA_WIDTH), row(DA_WIDTH), col(DA_WIDTH)),
        scratch_shapes=[pltpu.VMEM((CONV_CH // LANES, CONV_HALO + ROW_TILE + CONV_HALO,
                                    LANES), F32)],
        compiler_params=_params("arbitrary"),
        name="in_proj",
    )(x1, nw, w_t, cw, cb, dtb, alog)


def _ssd_step(xbc_ref, cum_ref, src_ref, o_ref, state_ref):
    L = SSD_CHUNK
    hp = SSD_HEAD_DIM
    row_i = lax.broadcasted_iota(jnp.int32, (L, L), 0)
    col_i = lax.broadcasted_iota(jnp.int32, (L, L), 1)
    causal = col_i <= row_i
    first = col_i < hp
    keep = (first.astype(BF16), 1.0 - first.astype(BF16))
    state_rows = pl.ds(pl.multiple_of(jnp.minimum(pl.program_id(0), 0), SUBLANES),
                       SSD_STATE)
    for r in range(SSD_CHUNKS_PER_STEP):
        rows = slice(r * SSD_CHUNK, (r + 1) * SSD_CHUNK)
        cum_col = cum_ref[rows, :]
        src_row = src_ref[rows, :].T
        for g in range(SSD_GROUPS):
            b_lo = SSD_WIDTH + g * SSD_STATE
            c_lo = SSD_WIDTH + (SSD_GROUPS + g) * SSD_STATE
            b_g = xbc_ref[rows, b_lo:b_lo + SSD_STATE]
            c_g = xbc_ref[rows, c_lo:c_lo + SSD_STATE]
            cb = _dot_nt(c_g, b_g)
            bt_g = b_g.astype(F32).T
            for pr in range(HEADS_PER_GROUP // 2):
                pair = g * (HEADS_PER_GROUP // 2) + pr
                lanes = slice(pair * 2 * hp, (pair + 1) * 2 * hp)
                x_pair = xbc_ref[rows, lanes]
                x_bd = jnp.concatenate([x_pair * keep[0], x_pair * keep[1]], axis=0)
                cols, w_parts, b_parts = [], [], []
                for t in range(2):
                    h = 2 * pair + t
                    col_b = jnp.broadcast_to(cum_col[:, h:h + 1], (L, L))
                    row_b = jnp.broadcast_to(src_row[h:h + 1, :], (L, L))
                    dec = jnp.exp2(jnp.where(causal, col_b - row_b, -jnp.inf))
                    w_parts.append((cb * dec).astype(BF16))
                    b_parts.append((bt_g * dec[L - 1:L, :]).astype(BF16))
                    cols.append(col_b)
                from_start = jnp.exp2(jnp.where(first, cols[0], cols[1]))
                s_prev = state_ref[pair, state_rows, :]
                y = (_dot(jnp.concatenate(w_parts, axis=1), x_bd)
                     + _dot(c_g, s_prev.astype(BF16)) * from_start)
                o_ref[rows, lanes] = y.astype(BF16)
                new = _dot(jnp.concatenate(b_parts, axis=1), x_bd)
                state_ref[pair, state_rows, :] = s_prev * from_start[L - 1:L, :] + new


def _ssd_gate_norm(y, xs, z, dskip, nw):
    y = y.astype(F32) + dskip * xs.astype(F32)
    y = y * _silu(z.astype(F32))
    gw = SSD_WIDTH // SSD_GROUPS
    outs = []
    for g in range(SSD_GROUPS):
        yg = y[:, g * gw:(g + 1) * gw]
        ms = jnp.mean(yg * yg, axis=-1, keepdims=True)
        outs.append(yg * lax.rsqrt(ms + EPS))
    return jnp.concatenate(outs, axis=-1) * nw


def _rel_bias_kernel(rb_ref, o_ref):
    k_i = lax.broadcasted_iota(jnp.int32, (BIAS_TILE, BIAS_TILE), 0)
    q_i = lax.broadcasted_iota(jnp.int32, (BIAS_TILE, BIAS_TILE), 1)
    max_exact = NUM_BUCKETS // 2
    for d in range(2):
        n = q_i - k_i + d * BIAS_TILE
        nf = jnp.maximum(n, 1).astype(F32)
        large = max_exact + (jnp.log(nf / max_exact) / math.log(MAX_DISTANCE / max_exact)
                             * (NUM_BUCKETS - max_exact)).astype(jnp.int32)
        large = jnp.minimum(large, NUM_BUCKETS - 1)
        bucket = jnp.where(n < max_exact, n, large)
        hits = [bucket == b for b in range(NUM_BUCKETS)]
        for h in range(DA_HEADS):
            tile = jnp.zeros((BIAS_TILE, BIAS_TILE), F32)
            for b in range(NUM_BUCKETS):
                tile = jnp.where(hits[b], rb_ref[b, h], tile)
            tile = (tile - rb_ref[NUM_BUCKETS - 1, h]) * LOG2_E
            o_ref[h, d] = jnp.where(n >= 0, tile, -jnp.inf)


def _rel_bias_tiles(rel_bias):
    return pl.pallas_call(
        _rel_bias_kernel,
        out_shape=jax.ShapeDtypeStruct((DA_HEADS, 2, BIAS_TILE, BIAS_TILE), F32),
        in_specs=[pl.BlockSpec(memory_space=pltpu.SMEM)],
        out_specs=pl.BlockSpec(memory_space=pltpu.VMEM),
        compiler_params=_params(),
        name="rel_bias",
    )(rel_bias)


def _attn_kernel(qt_ref, k_ref, vt_ref, bt_ref, lam_ref, sw_ref, o_ref,
                 s0_ref, s1_ref, m_ref, acc_ref, *, lambda_init, seq):
    tb = ATTN_BLOCK
    dh = DA_HEAD_DIM
    dv = DA_V_DIM
    nb = seq // tb

    def q_operand(i, c, q0, q1):
        blk = qt_ref[c * dh:(c + 1) * dh, i * tb + q0:i * tb + q1]
        pad = jnp.zeros_like(blk)
        return jnp.concatenate([blk, pad] if c == 0 else [pad, blk], axis=0)

    def v_operand(j, k0, k1):
        ones = jnp.ones((ATTN_ONES_ROWS, k1 - k0), BF16)
        return jnp.concatenate([vt_ref[:, j * tb + k0:j * tb + k1], ones], axis=0)

    s_refs = (s0_ref, s1_ref)
    half = tb // 2
    bt = BIAS_TILE
    nsub = tb // bt
    full = ((0, tb, 0, tb),)
    diag_parts = ((0, half, 0, half), (0, tb, half, tb))

    zero = jnp.minimum(pl.program_id(0), 0)

    def rows(lo, hi):
        return pl.ds(pl.multiple_of(zero + lo, BIAS_TILE), hi - lo)

    acc_rows = pl.ds(pl.multiple_of(zero, SUBLANES), dv + ATTN_ONES_ROWS)

    def scores(c, step):
        i, j, kind = step
        k0 = j * tb
        if kind != "diag":
            s_refs[c][rows(0, tb), 0:tb] = _dot(k_ref[k0:k0 + tb, :],
                                                q_operand(i, c, 0, tb))
        else:
            s_refs[c][rows(0, half), 0:tb] = _dot(k_ref[k0:k0 + half, :],
                                                  q_operand(i, c, 0, tb))
            s_refs[c][rows(half, tb), half:tb] = _dot(k_ref[k0 + half:k0 + tb, :],
                                                      q_operand(i, c, half, tb))

    def add_bias(c, kind):
        s_ref = s_refs[c]
        if kind == "near":
            near = (rows(tb - bt, tb), slice(0, bt))
            s_ref[near] = s_ref[near] + bt_ref[1]
        elif kind == "diag":
            for a in range(nsub):
                on = (rows(a * bt, (a + 1) * bt), slice(a * bt, (a + 1) * bt))
                s_ref[on] = s_ref[on] + bt_ref[0]
                if a + 1 < nsub:
                    off = (rows(a * bt, (a + 1) * bt), slice((a + 1) * bt, (a + 2) * bt))
                    s_ref[off] = s_ref[off] + bt_ref[1]
            for k0, k1, q0, q1 in diag_parts:
                for a in range(k0 // bt, k1 // bt):
                    for b in range(q0 // bt, q1 // bt):
                        if a > b:
                            s_ref[rows(a * bt, (a + 1) * bt), b * bt:(b + 1) * bt] = (
                                jnp.full((bt, bt), -jnp.inf, F32))

    def softmax_pv(c, step):
        i, j, kind = step
        for k0, k1, q0, q1 in (diag_parts if kind == "diag" else full):
            s = s_refs[c][rows(k0, k1), q0:q1]
            m_blk = jnp.max(s, axis=0, keepdims=True)
            if j == 0:
                m_new = m_blk
            else:
                m_old = m_ref[i, c, :, q0:q1]
                m_new = jnp.maximum(m_old, m_blk)
            pv = _dot(v_operand(j, k0, k1), jnp.exp2(s - m_new).astype(BF16))
            if j == 0:
                acc_ref[i, c, acc_rows, q0:q1] = pv
            else:
                acc_ref[i, c, acc_rows, q0:q1] = (acc_ref[i, c, acc_rows, q0:q1]
                                                  * jnp.exp2(m_old - m_new) + pv)
            m_ref[i, c, :, q0:q1] = m_new

    steps = []
    for j in range(nb):
        for i in range(j, nb):
            steps.append((i, j, "diag" if i == j else "near" if i == j + 1 else "far"))

    scores(0, steps[0])
    for t, step in enumerate(steps):
        scores(1, step)
        add_bias(0, step[2])
        softmax_pv(0, step)
        if t + 1 < len(steps):
            scores(0, steps[t + 1])
        add_bias(1, step[2])
        softmax_pv(1, step)

    lp = lam_ref[...]
    lam = (jnp.exp(jnp.sum(lp[0:1] * lp[1:2], axis=-1, keepdims=True))
           - jnp.exp(jnp.sum(lp[2:3] * lp[3:4], axis=-1, keepdims=True))
           + lambda_init)
    for i in range(nb):
        a0 = acc_ref[i, 0, acc_rows, :]
        a1 = acc_ref[i, 1, acc_rows, :]
        o = a0[0:dv] / a0[dv:dv + 1] - lam * (a1[0:dv] / a1[dv:dv + 1])
        o = o * lax.rsqrt(jnp.mean(o * o, axis=0, keepdims=True) + EPS)
        o = o * sw_ref[...] * (1.0 - lambda_init)
        o_ref[i * tb:(i + 1) * tb, :] = o.T.astype(BF16)


def _mixers_kernel(qt_ref, k_ref, vt_ref, bt_ref, lam_ref, sw_ref, xbc_ref, cum_ref,
                   src_ref, o_attn_ref, o_scan_ref, s0_ref, s1_ref, m_ref, acc_ref,
                   state_ref, *, lambda_init, seq):
    @pl.when(pl.program_id(1) == 0)
    def _():
        state_ref[...] = jnp.zeros_like(state_ref)

    _ssd_step(xbc_ref, cum_ref, src_ref, o_scan_ref, state_ref)
    _attn_kernel(qt_ref, k_ref, vt_ref, bt_ref, lam_ref, sw_ref, o_attn_ref,
                 s0_ref, s1_ref, m_ref, acc_ref, lambda_init=lambda_init, seq=seq)


def _mixers(q_t, k, v_t, tiles, lam_p, sw, xbc, cum, src, *, batch, seq, lambda_init):
    nb = seq // ATTN_BLOCK
    step_rows = SSD_CHUNKS_PER_STEP * SSD_CHUNK
    assert seq // step_rows == DA_HEADS
    seq_head = pl.BlockSpec((seq, DA_V_DIM), lambda b, h: (b, h))
    head_seq = pl.BlockSpec((DA_V_DIM, seq), lambda b, h: (h, b))
    acc_rows = DA_V_DIM + ATTN_ONES_ROWS

    def scan_rows(n):
        return pl.BlockSpec((step_rows, n), lambda b, h: (b * DA_HEADS + h, 0))

    return pl.pallas_call(
        functools.partial(_mixers_kernel, lambda_init=lambda_init, seq=seq),
        out_shape=(jax.ShapeDtypeStruct((batch * seq, DA_WIDTH), BF16),
                   jax.ShapeDtypeStruct((batch * seq, SSD_WIDTH), BF16)),
        grid=(batch, DA_HEADS),
        in_specs=[head_seq, seq_head, head_seq,
                  pl.BlockSpec((None, 2, BIAS_TILE, BIAS_TILE),
                               lambda b, h: (h, 0, 0, 0)),
                  _resident((4, DA_HEAD_DIM)), _resident((DA_V_DIM, ATTN_BLOCK)),
                  scan_rows(CONV_CH), scan_rows(DT_PAD), scan_rows(DT_PAD)],
        out_specs=(seq_head, scan_rows(SSD_WIDTH)),
        scratch_shapes=[pltpu.VMEM((ATTN_BLOCK, ATTN_BLOCK + LANES), F32),
                        pltpu.VMEM((ATTN_BLOCK, ATTN_BLOCK + LANES), F32),
                        pltpu.VMEM((nb, 2, 1, ATTN_BLOCK), F32),
                        pltpu.VMEM((nb, 2, acc_rows, ATTN_BLOCK), F32),
                        pltpu.VMEM((SSD_HEADS // 2, SSD_STATE, 2 * SSD_HEAD_DIM), F32)],
        compiler_params=_params("parallel", "arbitrary"),
        name="mixers",
    )(q_t, k, v_t, tiles, lam_p, sw, xbc, cum, src)


def _mem_kv_kernel(mem_ref, nw_ref, wk_ref, wv_ref, k_ref, v_ref):
    mn = _rms(mem_ref[...], nw_ref[...])
    k_ref[...] = _dot(mn, wk_ref[...]).astype(BF16)
    v_ref[...] = _dot(mn, wv_ref[...]).astype(BF16)


def _mem_kv(mem, nw, wk, wv):
    t = mem.shape[0]
    row = pl.BlockSpec((MEM_LEN, D_MODEL), lambda i: (i, 0))
    return pl.pallas_call(
        _mem_kv_kernel,
        out_shape=(jax.ShapeDtypeStruct((t, D_MODEL), BF16),
                   jax.ShapeDtypeStruct((t, D_MODEL), BF16)),
        grid=(t // MEM_LEN,),
        in_specs=[row, _resident((1, D_MODEL)), _resident((D_MODEL, D_MODEL)),
                  _resident((D_MODEL, D_MODEL))],
        out_specs=(row, row),
        compiler_params=_params("parallel"),
        name="mem_kv",
    )(mem, nw, wk, wv)


def _mix_out_cross_kernel(x_ref, ys_ref, xs_ref, z_ref, ya_ref, dskip_ref, snw_ref,
                          ws_ref, wa_ref, nw_ref, wq_ref, k_ref, v_ref, wo_ref, o_ref):
    y_ssd = _ssd_gate_norm(ys_ref[...], xs_ref[...], z_ref[...], dskip_ref[...],
                           snw_ref[...])
    x = (x_ref[...] + _dot(ya_ref[...].astype(F32), wa_ref[...])
         + _dot(y_ssd, ws_ref[...]))
    h = _rms(x, nw_ref[...])
    q = (_dot(h, wq_ref[...]) * (CROSS_HEAD_DIM ** -0.5)).astype(BF16)
    outs = []
    for hh in range(CROSS_HEADS):
        sl = slice(hh * CROSS_HEAD_DIM, (hh + 1) * CROSS_HEAD_DIM)
        s = _dot_nt(q[:, sl], k_ref[:, sl])
        p = jnp.exp(s - jnp.max(s, axis=-1, keepdims=True))
        o = _dot(p.astype(BF16), v_ref[:, sl])
        outs.append(o / jnp.sum(p, axis=-1, keepdims=True))
    o_ref[...] = x + _dot(jnp.concatenate(outs, axis=-1), wo_ref[...])


def _mix_out_cross(x1, y_scan, xbc, z, y_attn, dskip, snw, w_out, nw, wq, k, v, wo,
                   *, seq):
    t = x1.shape[0]
    tiles_per_batch = seq // ROW_TILE
    row = pl.BlockSpec((ROW_TILE, D_MODEL), lambda i: (i, 0))
    mem = pl.BlockSpec((MEM_LEN, D_MODEL), lambda i: (i // tiles_per_batch, 0))
    assert SSD_WIDTH == D_MODEL
    assert SSD_WIDTH == DA_WIDTH

    def w_half(j):
        return pl.BlockSpec((SSD_WIDTH, D_MODEL), lambda i: (j, 0),
                            pipeline_mode=pl.Buffered(1))

    return pl.pallas_call(
        _mix_out_cross_kernel,
        out_shape=jax.ShapeDtypeStruct((t, D_MODEL), F32),
        grid=(t // ROW_TILE,),
        in_specs=[row, row, row, row, row, _resident((1, SSD_WIDTH)),
                  _resident((1, SSD_WIDTH)), w_half(0), w_half(1),
                  _resident((1, D_MODEL)), _resident((D_MODEL, D_MODEL)), mem, mem,
                  _resident((D_MODEL, D_MODEL))],
        out_specs=row,
        compiler_params=_params("parallel"),
        name="mix_out_cross",
    )(x1, y_scan, xbc, z, y_attn, dskip, snw, w_out, w_out, nw, wq, k, v, wo)


def kernel(x, mem, norm_ffn1_w, ffn1_w_in, ffn1_w_out, norm_mix_w, w_in_mix, conv_w, conv_b, dt_bias, a_log, d_skip, ssd_norm_w, lambda_q1, lambda_k1, lambda_q2, lambda_k2, subln_w, rel_bias, w_out_mix, norm_cross_w, norm_mem_w, w_cq, w_ck, w_cv, w_co, norm_ffn2_w, ffn2_w_in, ffn2_w_out, norm_final_w):
    batch, seq, _ = x.shape
    depth = ffn1_w_in.shape[0]
    t = batch * seq
    xf = x.reshape(t, D_MODEL)
    memf = mem.reshape(batch * MEM_LEN, D_MODEL)
    fw = norm_final_w.reshape(1, D_MODEL)
    tiles = _rel_bias_tiles(rel_bias)

    def vec(p):
        return p.reshape(1, -1)

    def pad_lanes(p):
        return jnp.pad(p.reshape(1, -1), ((0, 0), (0, DT_PAD - p.shape[-1])))

    for l in range(depth):
        lambda_init = 0.8 - 0.6 * math.exp(-0.3 * l)
        xf = _ffn(xf, vec(norm_ffn1_w[l]), ffn1_w_in[l], ffn1_w_out[l], fw,
                  final=False, name="ffn1")

        z, xbc, cum, src, q_t, k, v_t = _in_proj(
            xf, vec(norm_mix_w[l]), w_in_mix[l].T,
            conv_w[l], vec(conv_b[l]), pad_lanes(dt_bias[l]), pad_lanes(a_log[l]),
            seq=seq)

        lam_p = jnp.stack([lambda_q1[l], lambda_k1[l], lambda_q2[l], lambda_k2[l]])
        sw = jnp.broadcast_to(subln_w[l].reshape(DA_V_DIM, 1), (DA_V_DIM, ATTN_BLOCK))
        y_attn, y_scan = _mixers(q_t, k, v_t, tiles, lam_p, sw, xbc, cum, src,
                                 batch=batch, seq=seq, lambda_init=lambda_init)

        mk, mv = _mem_kv(memf, vec(norm_mem_w[l]), w_ck[l], w_cv[l])
        xf = _mix_out_cross(xf, y_scan, xbc, z, y_attn,
                            vec(jnp.repeat(d_skip[l], SSD_HEAD_DIM)),
                            vec(ssd_norm_w[l]), w_out_mix[l],
                            vec(norm_cross_w[l]), w_cq[l], mk, mv, w_co[l], seq=seq)

        xf = _ffn(xf, vec(norm_ffn2_w[l]), ffn2_w_in[l], ffn2_w_out[l], fw,
                  final=(l == depth - 1), name="ffn2")
    return xf.reshape(batch, seq, D_MODEL)
```
